```python
import math
import jax
import jax.numpy as jnp
from jax import lax
import numpy as np

D_MODEL = 1024
BATCH = 4
SEQ = 4096
DEPTH = 4
DEC_BATCH = 128
DEC_SEQ = 8
PAST_LEN = 2048
PAGE_SIZE = 128

D_MIX = D_MODEL
HEAD_DIM = 64
A_WIDTH = D_MIX // 4
A_CONV = 3
GDN_WIDTH = D_MIX // 4
GDN_HEADS = GDN_WIDTH // HEAD_DIM
GDN_DK = HEAD_DIM
GDN_DV = HEAD_DIM
GDN_QK = GDN_HEADS * GDN_DK
GDN_CONV = 4
GDN_CONV_CH = 2 * GDN_QK + GDN_WIDTH
GDN_CHUNK = 64
NSA_WIDTH = D_MIX // 2
NSA_HEADS = NSA_WIDTH // HEAD_DIM
NSA_KV_HEADS = 2
NSA_GROUP = NSA_HEADS // NSA_KV_HEADS
NSA_HD = HEAD_DIM
NSA_KV_DIM = NSA_KV_HEADS * NSA_HD
CMP_STRIDE = 16
CMP_BLOCK = 2 * CMP_STRIDE
SEL_BLOCK = 64
SEL_TOPN = 8
WINDOW = 512
Q_BLOCK = 128
D_FF = 2816
FFN_CONV = 3
ALPHA = (2.0 * DEPTH) ** 0.25
BETA_INIT = (8.0 * DEPTH) ** -0.25
LN_EPS = 1e-5
IN_SIZES = (A_WIDTH, A_WIDTH, A_WIDTH, GDN_QK, GDN_QK, GDN_WIDTH, GDN_WIDTH, GDN_HEADS, GDN_HEADS, NSA_WIDTH, 4 * NSA_KV_DIM, 2 * NSA_KV_DIM, 3 * NSA_HEADS)
N_IN = sum(IN_SIZES)

kernel_name = 'hybrid_conv_gdn_nsa_decoder_step'


def layer_norm(x, g, b):
    xf = x.astype(jnp.float32)
    mu = jnp.mean(xf, axis=-1, keepdims=True)
    var = jnp.mean(jnp.square(xf - mu), axis=-1, keepdims=True)
    y = (xf - mu) * lax.rsqrt(var + LN_EPS) * g.astype(jnp.float32) + b.astype(jnp.float32)
    return y.astype(x.dtype)


def l2_normalize(x):
    return x * lax.rsqrt(jnp.sum(jnp.square(x), axis=-1, keepdims=True) + 1e-6)


def gated_rms_norm(o, gain, gate):
    of = o.astype(jnp.float32)
    of = of * lax.rsqrt(jnp.mean(jnp.square(of), axis=-1, keepdims=True) + 1e-6)
    return (of * gain.astype(jnp.float32) * jax.nn.silu(gate.astype(jnp.float32))).astype(gate.dtype)


def masked_softmax(s, mask):
    s = jnp.where(mask, s.astype(jnp.float32), -jnp.inf)
    m = jnp.max(s, axis=-1, keepdims=True)
    m = jnp.where(jnp.isfinite(m), m, 0.0)
    p = jnp.exp(s - m)
    return p / jnp.maximum(jnp.sum(p, axis=-1, keepdims=True), 1e-30)


def causal_dwconv(x, buf, w):
    k = w.shape[0]
    t = x.shape[1]
    xp = jnp.concatenate([buf.astype(x.dtype), x], axis=1)
    y = sum(xp[:, i:i + t] * w[i] for i in range(k))
    return y, xp[:, t:]


def split_in(proj):
    return jnp.split(proj, np.cumsum(IN_SIZES)[:-1].tolist(), axis=-1)


def gated_delta_rule(q, k, v, a_in, b_in, s0, a_log, dt_bias):
    f32 = jnp.float32
    b, t, h, dk = q.shape
    dv = v.shape[-1]
    q = l2_normalize(q.astype(f32)) * (dk ** -0.5)
    k = l2_normalize(k.astype(f32))
    v = v.astype(f32)
    g = -jnp.exp(a_log.astype(f32)) * jax.nn.softplus(a_in.astype(f32) + dt_bias.astype(f32))
    beta = jax.nn.sigmoid(b_in.astype(f32))
    c = GDN_CHUNK if t % GDN_CHUNK == 0 else t
    n = t // c

    def to_chunks(z):
        return jnp.moveaxis(z.reshape((b, n, c) + z.shape[2:]), 3, 2)

    qc, kc, vc, gc, bc = (to_chunks(z) for z in (q, k, v, g, beta))
    gc = jnp.cumsum(gc, axis=-1)
    incl = jnp.tril(jnp.ones((c, c), dtype=bool))
    strict = jnp.tril(jnp.ones((c, c), dtype=bool), -1)
    diff = gc[..., :, None] - gc[..., None, :]
    decay = jnp.where(incl, jnp.exp(jnp.where(incl, diff, 0.0)), 0.0)
    kb = kc * bc[..., None]
    m = jnp.eye(c, dtype=f32) + jnp.where(strict, jnp.einsum('bnhik,bnhjk->bnhij', kb, kc) * decay, 0.0)
    rhs = jnp.concatenate([vc * bc[..., None], kb * jnp.exp(gc)[..., None]], axis=-1)
    sol = lax.linalg.triangular_solve(m, rhs, left_side=True, lower=True, unit_diagonal=True)
    u, w = sol[..., :dv], sol[..., dv:]
    a_qk = jnp.einsum('bnhik,bnhjk->bnhij', qc, kc) * decay
    q_dec = qc * jnp.exp(gc)[..., None]
    k_dec = kc * jnp.exp(gc[..., -1:] - gc)[..., None]
    g_last = jnp.exp(gc[..., -1])

    def step(state, xs):
        u_n, w_n, q_n, k_n, a_n, gl_n = xs
        v_new = u_n - jnp.einsum('bhik,bhkv->bhiv', w_n, state)
        o_n = jnp.einsum('bhik,bhkv->bhiv', q_n, state) + jnp.einsum('bhij,bhjv->bhiv', a_n, v_new)
        state = state * gl_n[..., None, None] + jnp.einsum('bhik,bhiv->bhkv', k_n, v_new)
        return state, o_n

    xs = tuple(jnp.moveaxis(z, 1, 0) for z in (u, w, q_dec, k_dec, a_qk, g_last))
    s_final, o = lax.scan(step, s0.astype(f32), xs)
    o = jnp.moveaxis(jnp.moveaxis(o, 0, 1), 2, 3).reshape(b, t, h, dv)
    return o, s_final


def compress_blocks(rows, pe, w1, b1, w2):
    b, t_pad, kvh, hd = rows.shape
    r = rows.reshape(b, t_pad // CMP_STRIDE, CMP_STRIDE, kvh, hd)
    blocks = jnp.concatenate([r[:, :-1], r[:, 1:]], axis=2) + pe[:, None, :]
    hid = jax.nn.gelu(jnp.einsum('bclhd,lde->bche', blocks, w1) + b1)
    return jnp.einsum('bche,ed->bchd', hid, w2)


def select_attend(qg, idx, q_pos, ks, vs):
    b, kvh = ks.shape[:2]
    tq = qg.shape[1]
    n = idx.shape[-1]
    b_ix = jnp.arange(b)[:, None, None, None]
    h_ix = jnp.arange(kvh)[None, :, None, None]
    kg = ks[b_ix, h_ix, idx].reshape(b, kvh, tq, n * SEL_BLOCK, NSA_HD)
    vg = vs[b_ix, h_ix, idx].reshape(b, kvh, tq, n * SEL_BLOCK, NSA_HD)
    k_pos = (idx[..., None] * SEL_BLOCK + jnp.arange(SEL_BLOCK)).reshape(b, kvh, 1, tq, n * SEL_BLOCK)
    s = jnp.einsum('bqhgd,bhqkd->bhgqk', qg, kg) * (NSA_HD ** -0.5)
    p = masked_softmax(s, k_pos <= q_pos[:, None])
    return jnp.einsum('bhgqk,bhqkd->bqhgd', p.astype(vg.dtype), vg)


def nsa_compressed_selected(qg, kv, q_pos, cmp_pe, cmp_w1, cmp_b1, cmp_w2):
    b, t = kv.shape[:2]
    tq = qg.shape[1]
    t_pad = -(-t // SEL_BLOCK) * SEL_BLOCK
    kv = jnp.pad(kv, ((0, 0), (0, t_pad - t), (0, 0), (0, 0), (0, 0)))
    kc = compress_blocks(kv[:, :, 0], cmp_pe[0], cmp_w1[0], cmp_b1[0], cmp_w2[0])
    vc = compress_blocks(kv[:, :, 1], cmp_pe[1], cmp_w1[1], cmp_b1[1], cmp_w2[1])
    nc = kc.shape[1]
    cmp_start = jnp.arange(nc) * CMP_STRIDE
    s = jnp.einsum('bqhgd,bchd->bhgqc', qg, kc) * (NSA_HD ** -0.5)
    p = masked_softmax(s, cmp_start[None, :] + (CMP_BLOCK - 1) <= q_pos[:, None])
    o_cmp = jnp.einsum('bhgqc,bchd->bqhgd', p.astype(vc.dtype), vc)
    ns = t_pad // SEL_BLOCK
    sel_start = jnp.arange(ns) * SEL_BLOCK
    overlap = ((cmp_start[:, None] < sel_start[None, :] + SEL_BLOCK) & (cmp_start[:, None] + CMP_BLOCK > sel_start[None, :])).astype(jnp.float32)
    imp = jnp.einsum('bhgqc,cn->bhqn', p, overlap)
    blk = jnp.arange(ns)[None, :]
    forced = (blk == 0) | (blk == q_pos[:, None] // SEL_BLOCK)
    valid = sel_start[None, :] <= q_pos[:, None]
    imp = jnp.where(forced, jnp.inf, jnp.where(valid, imp, -jnp.inf))
    n_top = min(SEL_TOPN, ns)
    _, idx = lax.top_k(imp, n_top)
    ks = jnp.moveaxis(kv[:, :, 2].reshape(b, ns, SEL_BLOCK, NSA_KV_HEADS, NSA_HD), 3, 1)
    vs = jnp.moveaxis(kv[:, :, 3].reshape(b, ns, SEL_BLOCK, NSA_KV_HEADS, NSA_HD), 3, 1)
    if tq % Q_BLOCK == 0:
        nb = tq // Q_BLOCK
        qb = jnp.moveaxis(qg.reshape(b, nb, Q_BLOCK, NSA_KV_HEADS, NSA_GROUP, NSA_HD), 1, 0)
        ib = jnp.moveaxis(idx.reshape(b, NSA_KV_HEADS, nb, Q_BLOCK, n_top), 2, 0)
        pb = q_pos.reshape(nb, Q_BLOCK)
        ob = lax.map(lambda a: select_attend(a[0], a[1], a[2], ks, vs), (qb, ib, pb))
        o_slc = jnp.moveaxis(ob, 0, 1).reshape(b, tq, NSA_KV_HEADS, NSA_GROUP, NSA_HD)
    else:
        o_slc = select_attend(qg, idx, q_pos, ks, vs)
    return o_cmp, o_slc


def window_banded(qg, kvw):
    b, t = kvw.shape[:2]
    nb = t // Q_BLOCK
    nw = WINDOW // Q_BLOCK
    kp = jnp.pad(kvw, ((0, 0), (WINDOW, 0), (0, 0), (0, 0), (0, 0))).reshape(b, nb + nw, Q_BLOCK, 2, NSA_KV_HEADS, NSA_HD)
    band = jnp.concatenate([kp[:, i:i + nb] for i in range(nw + 1)], axis=2)
    qb = qg.reshape(b, nb, Q_BLOCK, NSA_KV_HEADS, NSA_GROUP, NSA_HD)
    start = jnp.arange(nb)[:, None] * Q_BLOCK
    qpos = start + jnp.arange(Q_BLOCK)
    kpos = start - WINDOW + jnp.arange((nw + 1) * Q_BLOCK)
    qp, kp_ = qpos[:, :, None], kpos[:, None, :]
    mask = (kp_ <= qp) & (kp_ > qp - WINDOW) & (kp_ >= 0)
    s = jnp.einsum('bnqhgd,bnkhd->bnhgqk', qb, band[:, :, :, 0]) * (NSA_HD ** -0.5)
    p = masked_softmax(s, mask[None, :, None, None])
    o = jnp.einsum('bnhgqk,bnkhd->bnqhgd', p.astype(band.dtype), band[:, :, :, 1])
    return o.reshape(b, t, NSA_KV_HEADS, NSA_GROUP, NSA_HD)


def window_dense(qg, kvw_all, q_pos, k_pos):
    s = jnp.einsum('bqhgd,bkhd->bhgqk', qg, kvw_all[:, :, 0]) * (NSA_HD ** -0.5)
    mask = (k_pos[None, :] <= q_pos[:, None]) & (k_pos[None, :] > q_pos[:, None] - WINDOW)
    p = masked_softmax(s, mask)
    return jnp.einsum('bhgqk,bkhd->bqhgd', p.astype(kvw_all.dtype), kvw_all[:, :, 1])


def layer(x, q_pos, past, w_in, conv_a_w, gdn_conv_w, gdn_a_log, gdn_dt_bias, gdn_norm_g,
          cmp_pe, cmp_w1, cmp_b1, cmp_w2, w_out, ln1_g, ln1_b, w_up, ffn_conv_w, w_down, ln2_g, ln2_b):
    b, t, _ = x.shape
    if past is None:
        conv_a_buf = jnp.zeros((b, A_CONV - 1, A_WIDTH), x.dtype)
        gdn_conv_buf = jnp.zeros((b, GDN_CONV - 1, GDN_CONV_CH), x.dtype)
        s0 = jnp.zeros((b, GDN_HEADS, GDN_DK, GDN_DV), jnp.float32)
        ffn_buf = jnp.zeros((b, FFN_CONV - 1, D_FF), x.dtype)
        kv_past, win_buf = None, None
    else:
        conv_a_buf, gdn_conv_buf, s0, kv_past, win_buf, ffn_buf = past
    (a_b, a_c, a_h, g_q, g_k, g_v, g_gate, g_a, g_b, n_q, n_kv, n_win, n_gate) = split_in(x @ w_in)
    z, conv_a_new = causal_dwconv(a_c * a_h, conv_a_buf, conv_a_w)
    y_a = a_b * z
    qkv, gdn_conv_new = causal_dwconv(jnp.concatenate([g_q, g_k, g_v], axis=-1), gdn_conv_buf, gdn_conv_w)
    qkv = jax.nn.silu(qkv)
    q_b, k_b, v_b = jnp.split(qkv, [GDN_QK, 2 * GDN_QK], axis=-1)
    o_b, s_new = gated_delta_rule(q_b.reshape(b, t, GDN_HEADS, GDN_DK), k_b.reshape(b, t, GDN_HEADS, GDN_DK),
                                  v_b.reshape(b, t, GDN_HEADS, GDN_DV), g_a, g_b, s0, gdn_a_log, gdn_dt_bias)
    y_b = gated_rms_norm(o_b, gdn_norm_g, g_gate.reshape(b, t, GDN_HEADS, GDN_DV)).reshape(b, t, GDN_WIDTH)
    kv_new = n_kv.reshape(b, t, 4, NSA_KV_HEADS, NSA_HD)
    win_new = n_win.reshape(b, t, 2, NSA_KV_HEADS, NSA_HD)
    qg = n_q.reshape(b, t, NSA_KV_HEADS, NSA_GROUP, NSA_HD)
    kv_all = kv_new if kv_past is None else jnp.concatenate([kv_past, kv_new], axis=1)
    o_cmp, o_slc = nsa_compressed_selected(qg, kv_all, q_pos, cmp_pe, cmp_w1, cmp_b1, cmp_w2)
    if win_buf is None:
        o_win = window_banded(qg, win_new)
        win_state = win_new[:, t - min(WINDOW, t):]
    else:
        lw = win_buf.shape[1]
        win_all = jnp.concatenate([win_buf, win_new], axis=1)
        k_pos = (q_pos[0] - lw) + jnp.arange(lw + t)
        o_win = window_dense(qg, win_all, q_pos, k_pos)
        win_state = win_all[:, t:]
    gates = jax.nn.sigmoid(n_gate.reshape(b, t, 3, NSA_KV_HEADS, NSA_GROUP))[..., None]
    y_c = (gates[:, :, 0] * o_cmp + gates[:, :, 1] * o_slc + gates[:, :, 2] * o_win).reshape(b, t, NSA_WIDTH)
    mix = jnp.concatenate([y_a, y_b, y_c], axis=-1) @ w_out
    x = layer_norm(ALPHA * x + mix, ln1_g, ln1_b)
    gate, val = jnp.split(x @ w_up, 2, axis=-1)
    gate, ffn_new = causal_dwconv(gate, ffn_buf, ffn_conv_w)
    x = layer_norm(ALPHA * x + (jax.nn.silu(gate) * val) @ w_down, ln2_g, ln2_b)
    return x, (kv_new, win_state, conv_a_new, gdn_conv_new, s_new.astype(x.dtype), ffn_new)


def stack_layers(states, i):
    return jnp.stack([s[i] for s in states], axis=0)


def setup_inputs(seed: int = 0) -> dict:
    key = jax.random.key(seed)
    ks = jax.random.split(key, 32)
    f32 = jnp.float32
    n_pages = PAST_LEN // PAGE_SIZE
    n_used = DEC_BATCH * n_pages
    n_pool = n_used + max(1, n_used // 4)
    win_len = min(WINDOW, PAST_LEN)

    def nrm(k, shape, scale):
        return jax.random.normal(k, shape, f32) * scale

    dt = jnp.exp(jax.random.uniform(ks[14], (DEPTH, GDN_HEADS), f32, math.log(1e-3), math.log(1e-1)))
    return {
        'x_prompt': nrm(ks[0], (BATCH, SEQ, D_MODEL), 1.0),
        'x_sample': nrm(ks[1], (DEC_BATCH, DEC_SEQ, D_MODEL), 1.0),
        'cache_nsa_kv': nrm(ks[2], (DEPTH, n_pool, PAGE_SIZE, 4, NSA_KV_HEADS, NSA_HD), 1.0),
        'cache_nsa_win': nrm(ks[3], (DEPTH, DEC_BATCH, win_len, 2, NSA_KV_HEADS, NSA_HD), 1.0),
        'state_conv_a': nrm(ks[4], (DEPTH, DEC_BATCH, A_CONV - 1, A_WIDTH), 1.0),
        'state_gdn_conv': nrm(ks[5], (DEPTH, DEC_BATCH, GDN_CONV - 1, GDN_CONV_CH), 1.0),
        'state_gdn': nrm(ks[6], (DEPTH, DEC_BATCH, GDN_HEADS, GDN_DK, GDN_DV), 0.3),
        'state_ffn_conv': nrm(ks[7], (DEPTH, DEC_BATCH, FFN_CONV - 1, D_FF), 1.0),
        'page_table': jax.random.permutation(ks[8], n_pool)[:n_used].reshape(DEC_BATCH, n_pages).astype(jnp.int32),
        'ln_emb_g': 1.0 + nrm(ks[9], (D_MODEL,), 0.02),
        'ln_emb_b': nrm(ks[10], (D_MODEL,), 0.02),
        'w_in': nrm(ks[11], (DEPTH, D_MODEL, N_IN), D_MODEL ** -0.5),
        'conv_a_w': nrm(ks[12], (DEPTH, A_CONV, A_WIDTH), A_CONV ** -0.5),
        'gdn_conv_w': nrm(ks[13], (DEPTH, GDN_CONV, GDN_CONV_CH), GDN_CONV ** -0.5),
        'gdn_a_log': jnp.log(jax.random.uniform(ks[15], (DEPTH, GDN_HEADS), f32, 1.0, 16.0)),
        'gdn_dt_bias': jnp.log(jnp.expm1(dt)),
        'gdn_norm_g': 1.0 + nrm(ks[16], (DEPTH, GDN_DV), 0.02),
        'cmp_pe': nrm(ks[17], (DEPTH, 2, CMP_BLOCK, NSA_HD), 0.02),
        'cmp_w1': nrm(ks[18], (DEPTH, 2, CMP_BLOCK, NSA_HD, NSA_HD), (CMP_BLOCK * NSA_HD) ** -0.5),
        'cmp_b1': nrm(ks[19], (DEPTH, 2, NSA_HD), 0.02),
        'cmp_w2': nrm(ks[20], (DEPTH, 2, NSA_HD, NSA_HD), NSA_HD ** -0.5),
        'w_out': nrm(ks[21], (DEPTH, D_MIX, D_MODEL), BETA_INIT * D_MIX ** -0.5),
        'ln1_g': 1.0 + nrm(ks[22], (DEPTH, D_MODEL), 0.02),
        'ln1_b': nrm(ks[23], (DEPTH, D_MODEL), 0.02),
        'w_up': nrm(ks[24], (DEPTH, D_MODEL, 2 * D_FF), D_MODEL ** -0.5),
        'ffn_conv_w': nrm(ks[25], (DEPTH, FFN_CONV, D_FF), FFN_CONV ** -0.5),
        'w_down': nrm(ks[26], (DEPTH, D_FF, D_MODEL), BETA_INIT * D_FF ** -0.5),
        'ln2_g': 1.0 + nrm(ks[27], (DEPTH, D_MODEL), 0.02),
        'ln2_b': nrm(ks[28], (DEPTH, D_MODEL), 0.02),
    }


def reference(x_prompt, x_sample, cache_nsa_kv, cache_nsa_win, state_conv_a, state_gdn_conv, state_gdn,
              state_ffn_conv, page_table, ln_emb_g, ln_emb_b, w_in, conv_a_w, gdn_conv_w, gdn_a_log,
              gdn_dt_bias, gdn_norm_g, cmp_pe, cmp_w1, cmp_b1, cmp_w2, w_out, ln1_g, ln1_b, w_up,
              ffn_conv_w, w_down, ln2_g, ln2_b):
    dec_b, dec_t = x_sample.shape[:2]
    past_len = page_table.shape[1] * cache_nsa_kv.shape[2]
    pos_p = jnp.arange(x_prompt.shape[1], dtype=jnp.int32)
    pos_s = past_len + jnp.arange(dec_t, dtype=jnp.int32)
    xp = layer_norm(x_prompt, ln_emb_g, ln_emb_b)
    xs = layer_norm(x_sample, ln_emb_g, ln_emb_b)
    st_p, st_s = [], []
    for l in range(DEPTH):
        wl = (w_in[l], conv_a_w[l], gdn_conv_w[l], gdn_a_log[l], gdn_dt_bias[l], gdn_norm_g[l],
              cmp_pe[l], cmp_w1[l], cmp_b1[l], cmp_w2[l], w_out[l], ln1_g[l], ln1_b[l],
              w_up[l], ffn_conv_w[l], w_down[l], ln2_g[l], ln2_b[l])
        kv_past = cache_nsa_kv[l, page_table].reshape(dec_b, past_len, 4, NSA_KV_HEADS, NSA_HD)
        past = (state_conv_a[l], state_gdn_conv[l], state_gdn[l], kv_past, cache_nsa_win[l], state_ffn_conv[l])
        xp, sp = layer(xp, pos_p, None, *wl)
        xs, ss = layer(xs, pos_s, past, *wl)
        st_p.append(sp)
        st_s.append(ss)
    return (xp, xs,
            stack_layers(st_p, 0), stack_layers(st_s, 0),
            stack_layers(st_p, 1), stack_layers(st_s, 1),
            stack_layers(st_p, 2), stack_layers(st_s, 2),
            stack_layers(st_p, 3), stack_layers(st_s, 3),
            stack_layers(st_p, 4), stack_layers(st_s, 4),
            stack_layers(st_p, 5), stack_layers(st_s, 5))
```

```python
import functools
import math

import jax
import jax.numpy as jnp
import numpy as np
from jax import lax
from jax.experimental import pallas as pl
from jax.experimental.pallas import tpu as pltpu

F32 = jnp.float32
BF16 = jnp.bfloat16

D_MODEL = 1024
DEPTH = 4
HEAD_DIM = 64
A_WIDTH = 256
A_CONV = 3
GDN_WIDTH = 256
GDN_HEADS = 4
GDN_DK = 64
GDN_DV = 64
GDN_QK = 256
GDN_CONV = 4
GDN_CONV_CH = 768
GDN_CHUNK = 64
NSA_WIDTH = 512
NSA_HEADS = 8
NSA_KV_HEADS = 2
NSA_GROUP = 4
NSA_HD = 64
NSA_KV_DIM = 128
CMP_STRIDE = 16
CMP_BLOCK = 32
SEL_BLOCK = 64
SEL_TOPN = 8
WINDOW = 512
Q_BLOCK = 128
D_FF = 2816
FFN_CONV = 3
ALPHA = (2.0 * DEPTH) ** 0.25
LN_EPS = 1e-5
IN_SIZES = (A_WIDTH, A_WIDTH, A_WIDTH, GDN_QK, GDN_QK, GDN_WIDTH, GDN_WIDTH, GDN_HEADS, GDN_HEADS,
            NSA_WIDTH, 4 * NSA_KV_DIM, 2 * NSA_KV_DIM, 3 * NSA_HEADS)
N_IN = sum(IN_SIZES)

VMEM_LIMIT_BYTES = 56 * 1024 * 1024


def _ln_rows(z, g, b):
    mu = jnp.mean(z, axis=-1, keepdims=True)
    zc = z - mu
    var = jnp.mean(zc * zc, axis=-1, keepdims=True)
    return zc * lax.rsqrt(var + LN_EPS) * g + b


def _ln_kernel(x_ref, g_ref, b_ref, o_ref):
    o_ref[...] = _ln_rows(x_ref[...], g_ref[...], b_ref[...])


def layer_norm_rows(x, g, b, tm=512):
    m, d = x.shape
    return pl.pallas_call(
        _ln_kernel,
        grid=(m // tm,),
        in_specs=[pl.BlockSpec((tm, d), lambda i: (i, 0)),
                  pl.BlockSpec((1, d), lambda i: (0, 0)),
                  pl.BlockSpec((1, d), lambda i: (0, 0))],
        out_specs=pl.BlockSpec((tm, d), lambda i: (i, 0)),
        out_shape=jax.ShapeDtypeStruct((m, d), F32),
        name="ln_rows",
    )(x, g.reshape(1, d), b.reshape(1, d))


def _dense_kernel(x_ref, w_ref, o_ref):
    o_ref[...] = jnp.dot(x_ref[...].astype(BF16), w_ref[...], preferred_element_type=F32)


def dense(x, w, tm, tn, name):
    m, k = x.shape
    n = w.shape[1]
    return pl.pallas_call(
        _dense_kernel,
        grid=(n // tn, m // tm),
        in_specs=[pl.BlockSpec((tm, k), lambda j, i: (i, 0)),
                  pl.BlockSpec((k, tn), lambda j, i: (0, j))],
        out_specs=pl.BlockSpec((tm, tn), lambda j, i: (i, j)),
        out_shape=jax.ShapeDtypeStruct((m, n), F32),
        compiler_params=pltpu.CompilerParams(vmem_limit_bytes=VMEM_LIMIT_BYTES),
        name=name,
    )(x, w)


def _dense_res_ln_kernel(y_ref, w_ref, x_ref, g_ref, b_ref, o_ref):
    acc = jnp.dot(y_ref[...].astype(BF16), w_ref[...], preferred_element_type=F32)
    o_ref[...] = _ln_rows(ALPHA * x_ref[...] + acc, g_ref[...], b_ref[...])


def dense_res_ln(y, w, x, g, b, tm, name):
    m, k = y.shape
    d = w.shape[1]
    return pl.pallas_call(
        _dense_res_ln_kernel,
        grid=(m // tm,),
        in_specs=[pl.BlockSpec((tm, k), lambda i: (i, 0)),
                  pl.BlockSpec((k, d), lambda i: (0, 0)),
                  pl.BlockSpec((tm, d), lambda i: (i, 0)),
                  pl.BlockSpec((1, d), lambda i: (0, 0)),
                  pl.BlockSpec((1, d), lambda i: (0, 0))],
        out_specs=pl.BlockSpec((tm, d), lambda i: (i, 0)),
        out_shape=jax.ShapeDtypeStruct((m, d), F32),
        compiler_params=pltpu.CompilerParams(vmem_limit_bytes=VMEM_LIMIT_BYTES),
        name=name,
    )(y, w, x, g.reshape(1, d), b.reshape(1, d))


def l2_normalize(x):
    return x * lax.rsqrt(jnp.sum(jnp.square(x), axis=-1, keepdims=True) + 1e-6)


def gated_rms_norm(o, gain, gate):
    of = o.astype(F32)
    of = of * lax.rsqrt(jnp.mean(jnp.square(of), axis=-1, keepdims=True) + 1e-6)
    return (of * gain.astype(F32) * jax.nn.silu(gate.astype(F32))).astype(gate.dtype)


def masked_softmax(s, mask):
    s = jnp.where(mask, s.astype(F32), -jnp.inf)
    m = jnp.max(s, axis=-1, keepdims=True)
    m = jnp.where(jnp.isfinite(m), m, 0.0)
    p = jnp.exp(s - m)
    return p / jnp.maximum(jnp.sum(p, axis=-1, keepdims=True), 1e-30)


def causal_dwconv(x, buf, w):
    k = w.shape[0]
    t = x.shape[1]
    xp = jnp.concatenate([buf.astype(x.dtype), x], axis=1)
    y = sum(xp[:, i:i + t] * w[i] for i in range(k))
    return y, xp[:, t:]


def split_in(proj):
    return jnp.split(proj, np.cumsum(IN_SIZES)[:-1].tolist(), axis=-1)


def gated_delta_rule(q, k, v, a_in, b_in, s0, a_log, dt_bias):
    b, t, h, dk = q.shape
    dv = v.shape[-1]
    q = l2_normalize(q.astype(F32)) * (dk ** -0.5)
    k = l2_normalize(k.astype(F32))
    v = v.astype(F32)
    g = -jnp.exp(a_log.astype(F32)) * jax.nn.softplus(a_in.astype(F32) + dt_bias.astype(F32))
    beta = jax.nn.sigmoid(b_in.astype(F32))
    c = GDN_CHUNK if t % GDN_CHUNK == 0 else t
    n = t // c

    def to_chunks(z):
        return jnp.moveaxis(z.reshape((b, n, c) + z.shape[2:]), 3, 2)

    qc, kc, vc, gc, bc = (to_chunks(z) for z in (q, k, v, g, beta))
    gc = jnp.cumsum(gc, axis=-1)
    incl = jnp.tril(jnp.ones((c, c), dtype=bool))
    strict = jnp.tril(jnp.ones((c, c), dtype=bool), -1)
    diff = gc[..., :, None] - gc[..., None, :]
    decay = jnp.where(incl, jnp.exp(jnp.where(incl, diff, 0.0)), 0.0)
    kb = kc * bc[..., None]
    m = jnp.eye(c, dtype=F32) + jnp.where(strict, jnp.einsum('bnhik,bnhjk->bnhij', kb, kc) * decay, 0.0)
    rhs = jnp.concatenate([vc * bc[..., None], kb * jnp.exp(gc)[..., None]], axis=-1)
    sol = lax.linalg.triangular_solve(m, rhs, left_side=True, lower=True, unit_diagonal=True)
    u, w = sol[..., :dv], sol[..., dv:]
    a_qk = jnp.einsum('bnhik,bnhjk->bnhij', qc, kc) * decay
    q_dec = qc * jnp.exp(gc)[..., None]
    k_dec = kc * jnp.exp(gc[..., -1:] - gc)[..., None]
    g_last = jnp.exp(gc[..., -1])

    def step(state, xs):
        u_n, w_n, q_n, k_n, a_n, gl_n = xs
        v_new = u_n - jnp.einsum('bhik,bhkv->bhiv', w_n, state)
        o_n = jnp.einsum('bhik,bhkv->bhiv', q_n, state) + jnp.einsum('bhij,bhjv->bhiv', a_n, v_new)
        state = state * gl_n[..., None, None] + jnp.einsum('bhik,bhiv->bhkv', k_n, v_new)
        return state, o_n

    xs = tuple(jnp.moveaxis(z, 1, 0) for z in (u, w, q_dec, k_dec, a_qk, g_last))
    s_final, o = lax.scan(step, s0.astype(F32), xs)
    o = jnp.moveaxis(jnp.moveaxis(o, 0, 1), 2, 3).reshape(b, t, h, dv)
    return o, s_final


def compress_blocks(rows, pe, w1, b1, w2):
    b, t_pad, kvh, hd = rows.shape
    r = rows.reshape(b, t_pad // CMP_STRIDE, CMP_STRIDE, kvh, hd)
    blocks = jnp.concatenate([r[:, :-1], r[:, 1:]], axis=2) + pe[:, None, :]
    hid = jax.nn.gelu(jnp.einsum('bclhd,lde->bche', blocks, w1) + b1)
    return jnp.einsum('bche,ed->bchd', hid, w2)


def select_attend(qg, idx, q_pos, ks, vs):
    b, kvh = ks.shape[:2]
    tq = qg.shape[1]
    n = idx.shape[-1]
    b_ix = jnp.arange(b)[:, None, None, None]
    h_ix = jnp.arange(kvh)[None, :, None, None]
    kg = ks[b_ix, h_ix, idx].reshape(b, kvh, tq, n * SEL_BLOCK, NSA_HD)
    vg = vs[b_ix, h_ix, idx].reshape(b, kvh, tq, n * SEL_BLOCK, NSA_HD)
    k_pos = (idx[..., None] * SEL_BLOCK + jnp.arange(SEL_BLOCK)).reshape(b, kvh, 1, tq, n * SEL_BLOCK)
    s = jnp.einsum('bqhgd,bhqkd->bhgqk', qg, kg) * (NSA_HD ** -0.5)
    p = masked_softmax(s, k_pos <= q_pos[:, None])
    return jnp.einsum('bhgqk,bhqkd->bqhgd', p.astype(vg.dtype), vg)


def nsa_compressed_selected(qg, kv, q_pos, cmp_pe, cmp_w1, cmp_b1, cmp_w2):
    b, t = kv.shape[:2]
    tq = qg.shape[1]
    t_pad = -(-t // SEL_BLOCK) * SEL_BLOCK
    kv = jnp.pad(kv, ((0, 0), (0, t_pad - t), (0, 0), (0, 0), (0, 0)))
    kc = compress_blocks(kv[:, :, 0], cmp_pe[0], cmp_w1[0], cmp_b1[0], cmp_w2[0])
    vc = compress_blocks(kv[:, :, 1], cmp_pe[1], cmp_w1[1], cmp_b1[1], cmp_w2[1])
    nc = kc.shape[1]
    cmp_start = jnp.arange(nc) * CMP_STRIDE
    s = jnp.einsum('bqhgd,bchd->bhgqc', qg, kc) * (NSA_HD ** -0.5)
    p = masked_softmax(s, cmp_start[None, :] + (CMP_BLOCK - 1) <= q_pos[:, None])
    o_cmp = jnp.einsum('bhgqc,bchd->bqhgd', p.astype(vc.dtype), vc)
    ns = t_pad // SEL_BLOCK
    sel_start = jnp.arange(ns) * SEL_BLOCK
    overlap = ((cmp_start[:, None] < sel_start[None, :] + SEL_BLOCK)
               & (cmp_start[:, None] + CMP_BLOCK > sel_start[None, :])).astype(F32)
    imp = jnp.einsum('bhgqc,cn->bhqn', p, overlap)
    blk = jnp.arange(ns)[None, :]
    forced = (blk == 0) | (blk == q_pos[:, None] // SEL_BLOCK)
    valid = sel_start[None, :] <= q_pos[:, None]
    imp = jnp.where(forced, jnp.inf, jnp.where(valid, imp, -jnp.inf))
    n_top = min(SEL_TOPN, ns)
    _, idx = lax.top_k(imp, n_top)
    ks = jnp.moveaxis(kv[:, :, 2].reshape(b, ns, SEL_BLOCK, NSA_KV_HEADS, NSA_HD), 3, 1)
    vs = jnp.moveaxis(kv[:, :, 3].reshape(b, ns, SEL_BLOCK, NSA_KV_HEADS, NSA_HD), 3, 1)
    if tq % Q_BLOCK == 0:
        nb = tq // Q_BLOCK
        qb = jnp.moveaxis(qg.reshape(b, nb, Q_BLOCK, NSA_KV_HEADS, NSA_GROUP, NSA_HD), 1, 0)
        ib = jnp.moveaxis(idx.reshape(b, NSA_KV_HEADS, nb, Q_BLOCK, n_top), 2, 0)
        pb = q_pos.reshape(nb, Q_BLOCK)
        ob = lax.map(lambda a: select_attend(a[0], a[1], a[2], ks, vs), (qb, ib, pb))
        o_slc = jnp.moveaxis(ob, 0, 1).reshape(b, tq, NSA_KV_HEADS, NSA_GROUP, NSA_HD)
    else:
        o_slc = select_attend(qg, idx, q_pos, ks, vs)
    return o_cmp, o_slc


def window_banded(qg, kvw):
    b, t = kvw.shape[:2]
    nb = t // Q_BLOCK
    nw = WINDOW // Q_BLOCK
    kp = jnp.pad(kvw, ((0, 0), (WINDOW, 0), (0, 0), (0, 0), (0, 0))).reshape(b, nb + nw, Q_BLOCK, 2, NSA_KV_HEADS, NSA_HD)
    band = jnp.concatenate([kp[:, i:i + nb] for i in range(nw + 1)], axis=2)
    qb = qg.reshape(b, nb, Q_BLOCK, NSA_KV_HEADS, NSA_GROUP, NSA_HD)
    start = jnp.arange(nb)[:, None] * Q_BLOCK
    qpos = start + jnp.arange(Q_BLOCK)
    kpos = start - WINDOW + jnp.arange((nw + 1) * Q_BLOCK)
    qp, kp_ = qpos[:, :, None], kpos[:, None, :]
    mask = (kp_ <= qp) & (kp_ > qp - WINDOW) & (kp_ >= 0)
    s = jnp.einsum('bnqhgd,bnkhd->bnhgqk', qb, band[:, :, :, 0]) * (NSA_HD ** -0.5)
    p = masked_softmax(s, mask[None, :, None, None])
    o = jnp.einsum('bnhgqk,bnkhd->bnqhgd', p.astype(band.dtype), band[:, :, :, 1])
    return o.reshape(b, t, NSA_KV_HEADS, NSA_GROUP, NSA_HD)


def window_dense(qg, kvw_all, q_pos, k_pos):
    s = jnp.einsum('bqhgd,bkhd->bhgqk', qg, kvw_all[:, :, 0]) * (NSA_HD ** -0.5)
    mask = (k_pos[None, :] <= q_pos[:, None]) & (k_pos[None, :] > q_pos[:, None] - WINDOW)
    p = masked_softmax(s, mask)
    return jnp.einsum('bhgqk,bkhd->bqhgd', p.astype(kvw_all.dtype), kvw_all[:, :, 1])


def layer(x, q_pos, past, w_in, conv_a_w, gdn_conv_w, gdn_a_log, gdn_dt_bias, gdn_norm_g,
          cmp_pe, cmp_w1, cmp_b1, cmp_w2, w_out, ln1_g, ln1_b, w_up, ffn_conv_w, w_down, ln2_g, ln2_b):
    b, t, _ = x.shape
    tm = 512
    if past is None:
        conv_a_buf = jnp.zeros((b, A_CONV - 1, A_WIDTH), x.dtype)
        gdn_conv_buf = jnp.zeros((b, GDN_CONV - 1, GDN_CONV_CH), x.dtype)
        s0 = jnp.zeros((b, GDN_HEADS, GDN_DK, GDN_DV), F32)
        ffn_buf = jnp.zeros((b, FFN_CONV - 1, D_FF), x.dtype)
        kv_past, win_buf = None, None
    else:
        conv_a_buf, gdn_conv_buf, s0, kv_past, win_buf, ffn_buf = past
    x2 = x.reshape(b * t, D_MODEL)
    proj = dense(x2, w_in, tm, N_IN, "in_proj").reshape(b, t, N_IN)
    (a_b, a_c, a_h, g_q, g_k, g_v, g_gate, g_a, g_b, n_q, n_kv, n_win, n_gate) = split_in(proj)
    z, conv_a_new = causal_dwconv(a_c * a_h, conv_a_buf, conv_a_w)
    y_a = a_b * z
    qkv, gdn_conv_new = causal_dwconv(jnp.concatenate([g_q, g_k, g_v], axis=-1), gdn_conv_buf, gdn_conv_w)
    qkv = jax.nn.silu(qkv)
    q_b, k_b, v_b = jnp.split(qkv, [GDN_QK, 2 * GDN_QK], axis=-1)
    o_b, s_new = gated_delta_rule(q_b.reshape(b, t, GDN_HEADS, GDN_DK), k_b.reshape(b, t, GDN_HEADS, GDN_DK),
                                  v_b.reshape(b, t, GDN_HEADS, GDN_DV), g_a, g_b, s0, gdn_a_log, gdn_dt_bias)
    y_b = gated_rms_norm(o_b, gdn_norm_g, g_gate.reshape(b, t, GDN_HEADS, GDN_DV)).reshape(b, t, GDN_WIDTH)
    kv_new = n_kv.reshape(b, t, 4, NSA_KV_HEADS, NSA_HD)
    win_new = n_win.reshape(b, t, 2, NSA_KV_HEADS, NSA_HD)
    qg = n_q.reshape(b, t, NSA_KV_HEADS, NSA_GROUP, NSA_HD)
    kv_all = kv_new if kv_past is None else jnp.concatenate([kv_past, kv_new], axis=1)
    o_cmp, o_slc = nsa_compressed_selected(qg, kv_all, q_pos, cmp_pe, cmp_w1, cmp_b1, cmp_w2)
    if win_buf is None:
        o_win = window_banded(qg, win_new)
        win_state = win_new[:, t - min(WINDOW, t):]
    else:
        lw = win_buf.shape[1]
        win_all = jnp.concatenate([win_buf, win_new], axis=1)
        k_pos = (q_pos[0] - lw) + jnp.arange(lw + t)
        o_win = window_dense(qg, win_all, q_pos, k_pos)
        win_state = win_all[:, t:]
    gates = jax.nn.sigmoid(n_gate.reshape(b, t, 3, NSA_KV_HEADS, NSA_GROUP))[..., None]
    y_c = (gates[:, :, 0] * o_cmp + gates[:, :, 1] * o_slc + gates[:, :, 2] * o_win).reshape(b, t, NSA_WIDTH)
    y = jnp.concatenate([y_a, y_b, y_c], axis=-1).reshape(b * t, D_MODEL)
    x1 = dense_res_ln(y, w_out, x2, ln1_g, ln1_b, tm, "out_proj_ln")
    up = dense(x1, w_up, tm, D_FF, "ffn_up").reshape(b, t, 2 * D_FF)
    gate, val = jnp.split(up, 2, axis=-1)
    gate, ffn_new = causal_dwconv(gate, ffn_buf, ffn_conv_w)
    hmid = (jax.nn.silu(gate) * val).reshape(b * t, D_FF)
    x2o = dense_res_ln(hmid, w_down, x1, ln2_g, ln2_b, tm, "ffn_down_ln")
    return x2o.reshape(b, t, D_MODEL), (kv_new, win_state, conv_a_new, gdn_conv_new, s_new.astype(x.dtype), ffn_new)


def stack_layers(states, i):
    return jnp.stack([s[i] for s in states], axis=0)


def kernel(x_prompt, x_sample, cache_nsa_kv, cache_nsa_win, state_conv_a, state_gdn_conv, state_gdn, state_ffn_conv, page_table, ln_emb_g, ln_emb_b, w_in, conv_a_w, gdn_conv_w, gdn_a_log, gdn_dt_bias, gdn_norm_g, cmp_pe, cmp_w1, cmp_b1, cmp_w2, w_out, ln1_g, ln1_b, w_up, ffn_conv_w, w_down, ln2_g, ln2_b):
    dec_b, dec_t = x_sample.shape[:2]
    past_len = page_table.shape[1] * cache_nsa_kv.shape[2]
    pos_p = jnp.arange(x_prompt.shape[1], dtype=jnp.int32)
    pos_s = past_len + jnp.arange(dec_t, dtype=jnp.int32)
    xp = layer_norm_rows(x_prompt.reshape(-1, D_MODEL), ln_emb_g, ln_emb_b).reshape(x_prompt.shape)
    xs = layer_norm_rows(x_sample.reshape(-1, D_MODEL), ln_emb_g, ln_emb_b).reshape(x_sample.shape)
    w_in_b, w_out_b, w_up_b, w_down_b = (w.astype(BF16) for w in (w_in, w_out, w_up, w_down))
    st_p, st_s = [], []
    for l in range(DEPTH):
        wl = (w_in_b[l], conv_a_w[l], gdn_conv_w[l], gdn_a_log[l], gdn_dt_bias[l], gdn_norm_g[l],
              cmp_pe[l], cmp_w1[l], cmp_b1[l], cmp_w2[l], w_out_b[l], ln1_g[l], ln1_b[l],
              w_up_b[l], ffn_conv_w[l], w_down_b[l], ln2_g[l], ln2_b[l])
        kv_past = cache_nsa_kv[l, page_table].reshape(dec_b, past_len, 4, NSA_KV_HEADS, NSA_HD)
        past = (state_conv_a[l], state_gdn_conv[l], state_gdn[l], kv_past, cache_nsa_win[l], state_ffn_conv[l])
        xp, sp = layer(xp, pos_p, None, *wl)
        xs, ss = layer(xs, pos_s, past, *wl)
        st_p.append(sp)
        st_s.append(ss)
    return (xp, xs,
            stack_layers(st_p, 0), stack_layers(st_s, 0),
            stack_layers(st_p, 1), stack_layers(st_s, 1),
            stack_layers(st_p, 2), stack_layers(st_s, 2),
            stack_layers(st_p, 3), stack_layers(st_s, 3),
            stack_layers(st_p, 4), stack_layers(st_s, 4),
            stack_layers(st_p, 5), stack_layers(st_s, 5))
```

```python
import functools
import math

import jax
import jax.numpy as jnp
import numpy as np
from jax import lax
from jax.experimental import pallas as pl
from jax.experimental.pallas import tpu as pltpu

F32 = jnp.float32
BF16 = jnp.bfloat16

D_MODEL = 1024
DEPTH = 4
HEAD_DIM = 64
A_WIDTH = 256
A_CONV = 3
GDN_WIDTH = 256
GDN_HEADS = 4
GDN_DK = 64
GDN_DV = 64
GDN_QK = 256
GDN_CONV = 4
GDN_CONV_CH = 768
GDN_CHUNK = 64
NSA_WIDTH = 512
NSA_HEADS = 8
NSA_KV_HEADS = 2
NSA_GROUP = 4
NSA_HD = 64
NSA_KV_DIM = 128
CMP_STRIDE = 16
CMP_BLOCK = 32
SEL_BLOCK = 64
SEL_TOPN = 8
WINDOW = 512
Q_BLOCK = 128
D_FF = 2816
FFN_CONV = 3
ALPHA = (2.0 * DEPTH) ** 0.25
LN_EPS = 1e-5
IN_SIZES = (A_WIDTH, A_WIDTH, A_WIDTH, GDN_QK, GDN_QK, GDN_WIDTH, GDN_WIDTH, GDN_HEADS, GDN_HEADS,
            NSA_WIDTH, 4 * NSA_KV_DIM, 2 * NSA_KV_DIM, 3 * NSA_HEADS)
N_IN = sum(IN_SIZES)

VMEM_LIMIT_BYTES = 56 * 1024 * 1024


def _ln_rows(z, g, b):
    mu = jnp.mean(z, axis=-1, keepdims=True)
    zc = z - mu
    var = jnp.mean(zc * zc, axis=-1, keepdims=True)
    return zc * lax.rsqrt(var + LN_EPS) * g + b


def _ln_kernel(x_ref, g_ref, b_ref, o_ref):
    o_ref[...] = _ln_rows(x_ref[...], g_ref[...], b_ref[...])


def layer_norm_rows(x, g, b, tm=512):
    m, d = x.shape
    return pl.pallas_call(
        _ln_kernel,
        grid=(m // tm,),
        in_specs=[pl.BlockSpec((tm, d), lambda i: (i, 0)),
                  pl.BlockSpec((1, d), lambda i: (0, 0)),
                  pl.BlockSpec((1, d), lambda i: (0, 0))],
        out_specs=pl.BlockSpec((tm, d), lambda i: (i, 0)),
        out_shape=jax.ShapeDtypeStruct((m, d), F32),
        name="ln_rows",
    )(x, g.reshape(1, d), b.reshape(1, d))


def _dense_kernel(x_ref, w_ref, o_ref):
    o_ref[...] = jnp.dot(x_ref[...].astype(BF16), w_ref[...], preferred_element_type=F32)


def dense(x, w, tm, tn, name):
    m, k = x.shape
    n = w.shape[1]
    return pl.pallas_call(
        _dense_kernel,
        grid=(n // tn, m // tm),
        in_specs=[pl.BlockSpec((tm, k), lambda j, i: (i, 0)),
                  pl.BlockSpec((k, tn), lambda j, i: (0, j))],
        out_specs=pl.BlockSpec((tm, tn), lambda j, i: (i, j)),
        out_shape=jax.ShapeDtypeStruct((m, n), F32),
        compiler_params=pltpu.CompilerParams(vmem_limit_bytes=VMEM_LIMIT_BYTES),
        name=name,
    )(x, w)


def _dense_res_ln_kernel(y_ref, w_ref, x_ref, g_ref, b_ref, o_ref):
    acc = jnp.dot(y_ref[...].astype(BF16), w_ref[...], preferred_element_type=F32)
    o_ref[...] = _ln_rows(ALPHA * x_ref[...] + acc, g_ref[...], b_ref[...])


def dense_res_ln(y, w, x, g, b, tm, name):
    m, k = y.shape
    d = w.shape[1]
    return pl.pallas_call(
        _dense_res_ln_kernel,
        grid=(m // tm,),
        in_specs=[pl.BlockSpec((tm, k), lambda i: (i, 0)),
                  pl.BlockSpec((k, d), lambda i: (0, 0)),
                  pl.BlockSpec((tm, d), lambda i: (i, 0)),
                  pl.BlockSpec((1, d), lambda i: (0, 0)),
                  pl.BlockSpec((1, d), lambda i: (0, 0))],
        out_specs=pl.BlockSpec((tm, d), lambda i: (i, 0)),
        out_shape=jax.ShapeDtypeStruct((m, d), F32),
        compiler_params=pltpu.CompilerParams(vmem_limit_bytes=VMEM_LIMIT_BYTES),
        name=name,
    )(y, w, x, g.reshape(1, d), b.reshape(1, d))


COL_Q = 0
COL_KV = 512
COL_AB = 1024
COL_AC = 1280
COL_AH = 1536
COL_GQKV = 1792
COL_GGATE = 2560
COL_WIN = 2816
COL_SMALL = 3072
N_PROJ = 3200
SM_GA, SM_GB, SM_NG = 0, 4, 8


def permute_w_in(w_in):
    offs = np.concatenate([[0], np.cumsum(IN_SIZES)])
    grp = lambda k: w_in[..., offs[k]:offs[k + 1]]
    pad = jnp.zeros(w_in.shape[:-1] + (N_PROJ - COL_SMALL - 32,), w_in.dtype)
    order = [grp(9), grp(10), grp(0), grp(1), grp(2), grp(3), grp(4), grp(5), grp(6), grp(11),
             grp(7), grp(8), grp(12), pad]
    return jnp.concatenate(order, axis=-1)


NEG = -1e30
KEY_CHUNK = 128


def _block_diag4(a, b):
    z = jnp.zeros_like(a)
    rows = [jnp.concatenate([a, z, z, z], -1), jnp.concatenate([z, a, z, z], -1),
            jnp.concatenate([z, z, b, z], -1), jnp.concatenate([z, z, z, b], -1)]
    return jnp.concatenate(rows, -2)


def compress_params(cmp_pe, cmp_w1, cmp_b1, cmp_w2):
    w1 = _block_diag4(cmp_w1[0], cmp_w1[1]).astype(BF16)
    w2 = _block_diag4(cmp_w2[0], cmp_w2[1]).astype(BF16)
    pe = jnp.concatenate([cmp_pe[0], cmp_pe[0], cmp_pe[1], cmp_pe[1]], -1)
    b1 = jnp.concatenate([cmp_b1[0], cmp_b1[0], cmp_b1[1], cmp_b1[1]], -1).reshape(1, 256)
    return w1, pe, b1, w2


def _compress_rows(row_loader, ncp, w1_ref, pe_ref, b1_ref, w2_ref):
    acc_lo = jnp.zeros((ncp, 256), F32)
    acc_hi = jnp.zeros((ncp, 256), F32)
    for l in range(CMP_STRIDE):
        x = row_loader(l)
        acc_lo += jnp.dot((x + pe_ref[l:l + 1, :]).astype(BF16), w1_ref[l], preferred_element_type=F32)
        acc_hi += jnp.dot((x + pe_ref[l + CMP_STRIDE:l + CMP_STRIDE + 1, :]).astype(BF16),
                          w1_ref[l + CMP_STRIDE], preferred_element_type=F32)
    hid = jax.nn.gelu(acc_lo + pltpu.roll(acc_hi, ncp - 1, axis=0) + b1_ref[...])
    return jnp.dot(hid.astype(BF16), w2_ref[...], preferred_element_type=F32)


def _cmp_attend_and_select(q_rows, kc, vc, pos, ovl, n_heads, tq, ncp, ns):
    s = lax.dot_general(q_rows, kc, (((1,), (1,)), ((), ())), preferred_element_type=F32)
    cend = lax.broadcasted_iota(jnp.int32, (tq, ncp), 1) * CMP_STRIDE + (CMP_BLOCK - 1)
    s3 = jnp.where((cend <= pos)[None], s.reshape(n_heads, tq, ncp), -jnp.inf)
    m = jnp.max(s3, axis=-1, keepdims=True)
    m = jnp.where(m == -jnp.inf, 0.0, m)
    p = jnp.exp(s3 - m)
    pn = p / jnp.maximum(jnp.sum(p, axis=-1, keepdims=True), 1e-30)
    o_cmp = jnp.dot(pn.reshape(n_heads * tq, ncp).astype(BF16), vc, preferred_element_type=F32)
    psum = pn[0]
    for g in range(1, n_heads):
        psum = psum + pn[g]
    p_hi = psum.astype(BF16)
    p_lo = (psum - p_hi.astype(F32)).astype(BF16)
    imp = (jnp.dot(p_hi, ovl, preferred_element_type=F32) + jnp.dot(p_lo, ovl, preferred_element_type=F32))
    blk = lax.broadcasted_iota(jnp.int32, (tq, ns), 1)
    blk_f = blk.astype(F32)
    forced = (blk == 0) | (blk == lax.shift_right_logical(pos, 6))
    valid = blk * SEL_BLOCK <= pos
    v = jnp.where(forced, jnp.inf, jnp.where(valid, imp, -jnp.inf))
    sel = jnp.zeros((tq, ns), F32)
    for _ in range(min(SEL_TOPN, ns)):
        mx = jnp.max(v, axis=-1, keepdims=True)
        idx = jnp.min(jnp.where(v == mx, blk_f, float(ns)), axis=-1, keepdims=True)
        hit = blk_f == idx
        sel = jnp.where(hit, 1.0, sel)
        v = jnp.where(hit, -jnp.inf, v)
    return o_cmp, sel


def _online_step(s, v_aug, m_s, acc_s):
    m_prev = m_s[...]
    m_new = jnp.maximum(m_prev, jnp.max(s, axis=-1, keepdims=True))
    p = jnp.exp(s - m_new)
    acc_s[...] = jnp.exp(m_prev - m_new) * acc_s[...] + jnp.dot(p.astype(BF16), v_aug, preferred_element_type=F32)
    m_s[...] = m_new


def _nsa_prompt_kernel(q_ref, cmp_ref, slc_ref, win_ref, sm_ref, w1_ref, pe_ref, b1_ref, w2_ref, ovl_ref, exp_ref,
                       y_ref, kc_s, vc_s, ks_s, vs_s, kw_s, vw_s, m_s, acc_s, *, t_len):
    tq = KEY_CHUNK
    i = pl.program_id(1)
    ncp = t_len // CMP_STRIDE
    ns = t_len // SEL_BLOCK
    ng = NSA_GROUP
    rows = ng * tq

    @pl.when(i == 0)
    def _prepare_sequence():
        kvc = _compress_rows(lambda l: cmp_ref[0, l], ncp, w1_ref, pe_ref, b1_ref, w2_ref)
        kc_s[...] = kvc[:, 0:128].astype(BF16)
        vc_s[...] = kvc[:, 128:256].astype(BF16)
        ones = jnp.ones((512, 64), BF16)

        def cast_rows(r, carry):
            sl = pl.ds(pl.multiple_of(r * 512, 512), 512)
            for h in range(NSA_KV_HEADS):
                ks_s[h, sl, :] = slc_ref[sl, 64 * h:64 * h + 64].astype(BF16)
                vs_s[h, sl, 0:64] = slc_ref[sl, 128 + 64 * h:192 + 64 * h].astype(BF16)
                vs_s[h, sl, 64:128] = ones
                kw_s[h, sl, :] = win_ref[sl, 64 * h:64 * h + 64].astype(BF16)
                vw_s[h, sl, 0:64] = win_ref[sl, 128 + 64 * h:192 + 64 * h].astype(BF16)
                vw_s[h, sl, 64:128] = ones
            return carry

        lax.fori_loop(0, t_len // 512, cast_rows, 0)

    pos = i * tq + lax.broadcasted_iota(jnp.int32, (tq, 1), 0)
    lane = lax.broadcasted_iota(jnp.int32, (tq, KEY_CHUNK), 1)
    gates = jax.nn.sigmoid(sm_ref[:, SM_NG:SM_NG + 3 * NSA_HEADS])
    nt = (((1,), (1,)), ((), ()))

    for h in range(NSA_KV_HEADS):
        qh = q_ref[:, 256 * h:256 * h + 256] * (NSA_HD ** -0.5)
        q_rows = jnp.concatenate([qh[:, 64 * g:64 * g + 64] for g in range(ng)], axis=0).astype(BF16)
        o_cmp, sel = _cmp_attend_and_select(q_rows, kc_s[:, 64 * h:64 * h + 64], vc_s[:, 64 * h:64 * h + 64],
                                            pos, ovl_ref[...], ng, tq, ncp, ns)
        sel_b = sel.astype(BF16)

        m_s[...] = jnp.full((rows, KEY_CHUNK), NEG, F32)
        acc_s[...] = jnp.zeros((rows, KEY_CHUNK), F32)

        def sel_body(j, carry):
            off = pl.multiple_of(j * KEY_CHUNK, KEY_CHUNK)
            s = lax.dot_general(q_rows, ks_s[h, pl.ds(off, KEY_CHUNK), :], nt, preferred_element_type=F32)
            chosen = jnp.dot(sel_b, exp_ref[:, pl.ds(off, KEY_CHUNK)], preferred_element_type=F32)
            bias = jnp.where(off + lane <= pos, (chosen - 1.0) * (-NEG), NEG)
            s = (s.reshape(ng, tq, KEY_CHUNK) + bias[None]).reshape(rows, KEY_CHUNK)
            _online_step(s, vs_s[h, pl.ds(off, KEY_CHUNK), :], m_s, acc_s)
            return carry

        lax.fori_loop(0, i + 1, sel_body, 0)
        acc = acc_s[...]
        o_slc = acc[:, 0:64] / acc[:, 64:128]

        m_s[...] = jnp.full((rows, KEY_CHUNK), NEG, F32)
        acc_s[...] = jnp.zeros((rows, KEY_CHUNK), F32)
        for d in range(WINDOW // KEY_CHUNK + 1):
            @pl.when(i >= d)
            def _window_chunk():
                off = pl.multiple_of((i - d) * KEY_CHUNK, KEY_CHUNK)
                s = lax.dot_general(q_rows, kw_s[h, pl.ds(off, KEY_CHUNK), :], nt, preferred_element_type=F32)
                kpos = off + lane
                bias = jnp.where(kpos <= pos, jnp.where(kpos > pos - WINDOW, 0.0, NEG), NEG)
                s = (s.reshape(ng, tq, KEY_CHUNK) + bias[None]).reshape(rows, KEY_CHUNK)
                _online_step(s, vw_s[h, pl.ds(off, KEY_CHUNK), :], m_s, acc_s)
        acc = acc_s[...]
        o_win = acc[:, 0:64] / acc[:, 64:128]

        for g in range(ng):
            hh = ng * h + g
            r = slice(g * tq, (g + 1) * tq)
            y_ref[:, 64 * hh:64 * hh + 64] = (gates[:, hh:hh + 1] * o_cmp[r]
                                              + gates[:, NSA_HEADS + hh:NSA_HEADS + hh + 1] * o_slc[r]
                                              + gates[:, 2 * NSA_HEADS + hh:2 * NSA_HEADS + hh + 1] * o_win[r])


def nsa_prompt(proj, n_seq, t_len, cparams):
    tq = KEY_CHUNK
    nt = t_len // tq
    ncp = t_len // CMP_STRIDE
    ns = t_len // SEL_BLOCK
    w1, pe, b1, w2 = cparams
    cstart = np.arange(ncp)[:, None] * CMP_STRIDE
    sstart = np.arange(ns)[None, :] * SEL_BLOCK
    ovl = jnp.asarray((cstart < sstart + SEL_BLOCK) & (cstart + CMP_BLOCK > sstart), BF16)
    expand = jnp.asarray(np.arange(t_len)[None, :] // SEL_BLOCK == np.arange(ns)[:, None], BF16)
    cmp_rows = proj[:, COL_KV:COL_KV + 256].reshape(n_seq, ncp, CMP_STRIDE, 256).transpose(0, 2, 1, 3)
    once = pl.Buffered(1)
    const = lambda shape: pl.BlockSpec(shape, lambda b, i: (0,) * len(shape), pipeline_mode=once)
    return pl.pallas_call(
        functools.partial(_nsa_prompt_kernel, t_len=t_len),
        grid=(n_seq, nt),
        in_specs=[pl.BlockSpec((tq, 512), lambda b, i: (b * nt + i, COL_Q // 512)),
                  pl.BlockSpec((1, CMP_STRIDE, ncp, 256), lambda b, i: (b, 0, 0, 0), pipeline_mode=once),
                  pl.BlockSpec((t_len, 256), lambda b, i: (b, (COL_KV + 256) // 256), pipeline_mode=once),
                  pl.BlockSpec((t_len, 256), lambda b, i: (b, COL_WIN // 256), pipeline_mode=once),
                  pl.BlockSpec((tq, 128), lambda b, i: (b * nt + i, COL_SMALL // 128)),
                  const((CMP_BLOCK, 256, 256)), const((CMP_BLOCK, 256)), const((1, 256)), const((256, 256)),
                  const((ncp, ns)), const((ns, t_len))],
        out_specs=pl.BlockSpec((tq, NSA_WIDTH), lambda b, i: (b * nt + i, 0)),
        out_shape=jax.ShapeDtypeStruct((n_seq * t_len, NSA_WIDTH), F32),
        scratch_shapes=[pltpu.VMEM((ncp, 128), BF16), pltpu.VMEM((ncp, 128), BF16),
                        pltpu.VMEM((NSA_KV_HEADS, t_len, 64), BF16), pltpu.VMEM((NSA_KV_HEADS, t_len, 128), BF16),
                        pltpu.VMEM((NSA_KV_HEADS, t_len, 64), BF16), pltpu.VMEM((NSA_KV_HEADS, t_len, 128), BF16),
                        pltpu.VMEM((NSA_GROUP * tq, KEY_CHUNK), F32), pltpu.VMEM((NSA_GROUP * tq, KEY_CHUNK), F32)],
        compiler_params=pltpu.CompilerParams(dimension_semantics=("arbitrary", "arbitrary"),
                                             vmem_limit_bytes=VMEM_LIMIT_BYTES),
        name="nsa_prompt",
    )(proj, cmp_rows, proj, proj, proj, w1, pe, b1, w2, ovl, expand)


def l2_normalize(x):
    return x * lax.rsqrt(jnp.sum(jnp.square(x), axis=-1, keepdims=True) + 1e-6)


def gated_rms_norm(o, gain, gate):
    of = o.astype(F32)
    of = of * lax.rsqrt(jnp.mean(jnp.square(of), axis=-1, keepdims=True) + 1e-6)
    return (of * gain.astype(F32) * jax.nn.silu(gate.astype(F32))).astype(gate.dtype)


def masked_softmax(s, mask):
    s = jnp.where(mask, s.astype(F32), -jnp.inf)
    m = jnp.max(s, axis=-1, keepdims=True)
    m = jnp.where(jnp.isfinite(m), m, 0.0)
    p = jnp.exp(s - m)
    return p / jnp.maximum(jnp.sum(p, axis=-1, keepdims=True), 1e-30)


def causal_dwconv(x, buf, w):
    k = w.shape[0]
    t = x.shape[1]
    xp = jnp.concatenate([buf.astype(x.dtype), x], axis=1)
    y = sum(xp[:, i:i + t] * w[i] for i in range(k))
    return y, xp[:, t:]


def split_in(proj):
    return jnp.split(proj, np.cumsum(IN_SIZES)[:-1].tolist(), axis=-1)


def gated_delta_rule(q, k, v, a_in, b_in, s0, a_log, dt_bias):
    b, t, h, dk = q.shape
    dv = v.shape[-1]
    q = l2_normalize(q.astype(F32)) * (dk ** -0.5)
    k = l2_normalize(k.astype(F32))
    v = v.astype(F32)
    g = -jnp.exp(a_log.astype(F32)) * jax.nn.softplus(a_in.astype(F32) + dt_bias.astype(F32))
    beta = jax.nn.sigmoid(b_in.astype(F32))
    c = GDN_CHUNK if t % GDN_CHUNK == 0 else t
    n = t // c

    def to_chunks(z):
        return jnp.moveaxis(z.reshape((b, n, c) + z.shape[2:]), 3, 2)

    qc, kc, vc, gc, bc = (to_chunks(z) for z in (q, k, v, g, beta))
    gc = jnp.cumsum(gc, axis=-1)
    incl = jnp.tril(jnp.ones((c, c), dtype=bool))
    strict = jnp.tril(jnp.ones((c, c), dtype=bool), -1)
    diff = gc[..., :, None] - gc[..., None, :]
    decay = jnp.where(incl, jnp.exp(jnp.where(incl, diff, 0.0)), 0.0)
    kb = kc * bc[..., None]
    m = jnp.eye(c, dtype=F32) + jnp.where(strict, jnp.einsum('bnhik,bnhjk->bnhij', kb, kc) * decay, 0.0)
    rhs = jnp.concatenate([vc * bc[..., None], kb * jnp.exp(gc)[..., None]], axis=-1)
    sol = lax.linalg.triangular_solve(m, rhs, left_side=True, lower=True, unit_diagonal=True)
    u, w = sol[..., :dv], sol[..., dv:]
    a_qk = jnp.einsum('bnhik,bnhjk->bnhij', qc, kc) * decay
    q_dec = qc * jnp.exp(gc)[..., None]
    k_dec = kc * jnp.exp(gc[..., -1:] - gc)[..., None]
    g_last = jnp.exp(gc[..., -1])

    def step(state, xs):
        u_n, w_n, q_n, k_n, a_n, gl_n = xs
        v_new = u_n - jnp.einsum('bhik,bhkv->bhiv', w_n, state)
        o_n = jnp.einsum('bhik,bhkv->bhiv', q_n, state) + jnp.einsum('bhij,bhjv->bhiv', a_n, v_new)
        state = state * gl_n[..., None, None] + jnp.einsum('bhik,bhiv->bhkv', k_n, v_new)
        return state, o_n

    xs = tuple(jnp.moveaxis(z, 1, 0) for z in (u, w, q_dec, k_dec, a_qk, g_last))
    s_final, o = lax.scan(step, s0.astype(F32), xs)
    o = jnp.moveaxis(jnp.moveaxis(o, 0, 1), 2, 3).reshape(b, t, h, dv)
    return o, s_final


def compress_blocks(rows, pe, w1, b1, w2):
    b, t_pad, kvh, hd = rows.shape
    r = rows.reshape(b, t_pad // CMP_STRIDE, CMP_STRIDE, kvh, hd)
    blocks = jnp.concatenate([r[:, :-1], r[:, 1:]], axis=2) + pe[:, None, :]
    hid = jax.nn.gelu(jnp.einsum('bclhd,lde->bche', blocks, w1) + b1)
    return jnp.einsum('bche,ed->bchd', hid, w2)


def select_attend(qg, idx, q_pos, ks, vs):
    b, kvh = ks.shape[:2]
    tq = qg.shape[1]
    n = idx.shape[-1]
    b_ix = jnp.arange(b)[:, None, None, None]
    h_ix = jnp.arange(kvh)[None, :, None, None]
    kg = ks[b_ix, h_ix, idx].reshape(b, kvh, tq, n * SEL_BLOCK, NSA_HD)
    vg = vs[b_ix, h_ix, idx].reshape(b, kvh, tq, n * SEL_BLOCK, NSA_HD)
    k_pos = (idx[..., None] * SEL_BLOCK + jnp.arange(SEL_BLOCK)).reshape(b, kvh, 1, tq, n * SEL_BLOCK)
    s = jnp.einsum('bqhgd,bhqkd->bhgqk', qg, kg) * (NSA_HD ** -0.5)
    p = masked_softmax(s, k_pos <= q_pos[:, None])
    return jnp.einsum('bhgqk,bhqkd->bqhgd', p.astype(vg.dtype), vg)


def nsa_compressed_selected(qg, kv, q_pos, cmp_pe, cmp_w1, cmp_b1, cmp_w2):
    b, t = kv.shape[:2]
    tq = qg.shape[1]
    t_pad = -(-t // SEL_BLOCK) * SEL_BLOCK
    kv = jnp.pad(kv, ((0, 0), (0, t_pad - t), (0, 0), (0, 0), (0, 0)))
    kc = compress_blocks(kv[:, :, 0], cmp_pe[0], cmp_w1[0], cmp_b1[0], cmp_w2[0])
    vc = compress_blocks(kv[:, :, 1], cmp_pe[1], cmp_w1[1], cmp_b1[1], cmp_w2[1])
    nc = kc.shape[1]
    cmp_start = jnp.arange(nc) * CMP_STRIDE
    s = jnp.einsum('bqhgd,bchd->bhgqc', qg, kc) * (NSA_HD ** -0.5)
    p = masked_softmax(s, cmp_start[None, :] + (CMP_BLOCK - 1) <= q_pos[:, None])
    o_cmp = jnp.einsum('bhgqc,bchd->bqhgd', p.astype(vc.dtype), vc)
    ns = t_pad // SEL_BLOCK
    sel_start = jnp.arange(ns) * SEL_BLOCK
    overlap = ((cmp_start[:, None] < sel_start[None, :] + SEL_BLOCK)
               & (cmp_start[:, None] + CMP_BLOCK > sel_start[None, :])).astype(F32)
    imp = jnp.einsum('bhgqc,cn->bhqn', p, overlap)
    blk = jnp.arange(ns)[None, :]
    forced = (blk == 0) | (blk == q_pos[:, None] // SEL_BLOCK)
    valid = sel_start[None, :] <= q_pos[:, None]
    imp = jnp.where(forced, jnp.inf, jnp.where(valid, imp, -jnp.inf))
    n_top = min(SEL_TOPN, ns)
    _, idx = lax.top_k(imp, n_top)
    ks = jnp.moveaxis(kv[:, :, 2].reshape(b, ns, SEL_BLOCK, NSA_KV_HEADS, NSA_HD), 3, 1)
    vs = jnp.moveaxis(kv[:, :, 3].reshape(b, ns, SEL_BLOCK, NSA_KV_HEADS, NSA_HD), 3, 1)
    if tq % Q_BLOCK == 0:
        nb = tq // Q_BLOCK
        qb = jnp.moveaxis(qg.reshape(b, nb, Q_BLOCK, NSA_KV_HEADS, NSA_GROUP, NSA_HD), 1, 0)
        ib = jnp.moveaxis(idx.reshape(b, NSA_KV_HEADS, nb, Q_BLOCK, n_top), 2, 0)
        pb = q_pos.reshape(nb, Q_BLOCK)
        ob = lax.map(lambda a: select_attend(a[0], a[1], a[2], ks, vs), (qb, ib, pb))
        o_slc = jnp.moveaxis(ob, 0, 1).reshape(b, tq, NSA_KV_HEADS, NSA_GROUP, NSA_HD)
    else:
        o_slc = select_attend(qg, idx, q_pos, ks, vs)
    return o_cmp, o_slc


def window_banded(qg, kvw):
    b, t = kvw.shape[:2]
    nb = t // Q_BLOCK
    nw = WINDOW // Q_BLOCK
    kp = jnp.pad(kvw, ((0, 0), (WINDOW, 0), (0, 0), (0, 0), (0, 0))).reshape(b, nb + nw, Q_BLOCK, 2, NSA_KV_HEADS, NSA_HD)
    band = jnp.concatenate([kp[:, i:i + nb] for i in range(nw + 1)], axis=2)
    qb = qg.reshape(b, nb, Q_BLOCK, NSA_KV_HEADS, NSA_GROUP, NSA_HD)
    start = jnp.arange(nb)[:, None] * Q_BLOCK
    qpos = start + jnp.arange(Q_BLOCK)
    kpos = start - WINDOW + jnp.arange((nw + 1) * Q_BLOCK)
    qp, kp_ = qpos[:, :, None], kpos[:, None, :]
    mask = (kp_ <= qp) & (kp_ > qp - WINDOW) & (kp_ >= 0)
    s = jnp.einsum('bnqhgd,bnkhd->bnhgqk', qb, band[:, :, :, 0]) * (NSA_HD ** -0.5)
    p = masked_softmax(s, mask[None, :, None, None])
    o = jnp.einsum('bnhgqk,bnkhd->bnqhgd', p.astype(band.dtype), band[:, :, :, 1])
    return o.reshape(b, t, NSA_KV_HEADS, NSA_GROUP, NSA_HD)


def window_dense(qg, kvw_all, q_pos, k_pos):
    s = jnp.einsum('bqhgd,bkhd->bhgqk', qg, kvw_all[:, :, 0]) * (NSA_HD ** -0.5)
    mask = (k_pos[None, :] <= q_pos[:, None]) & (k_pos[None, :] > q_pos[:, None] - WINDOW)
    p = masked_softmax(s, mask)
    return jnp.einsum('bhgqk,bkhd->bqhgd', p.astype(kvw_all.dtype), kvw_all[:, :, 1])


def layer(x, q_pos, past, w_in, conv_a_w, gdn_conv_w, gdn_a_log, gdn_dt_bias, gdn_norm_g,
          cmp_pe, cmp_w1, cmp_b1, cmp_w2, w_out, ln1_g, ln1_b, w_up, ffn_conv_w, w_down, ln2_g, ln2_b):
    b, t, _ = x.shape
    tm = 512
    if past is None:
        conv_a_buf = jnp.zeros((b, A_CONV - 1, A_WIDTH), x.dtype)
        gdn_conv_buf = jnp.zeros((b, GDN_CONV - 1, GDN_CONV_CH), x.dtype)
        s0 = jnp.zeros((b, GDN_HEADS, GDN_DK, GDN_DV), F32)
        ffn_buf = jnp.zeros((b, FFN_CONV - 1, D_FF), x.dtype)
        kv_past, win_buf = None, None
    else:
        conv_a_buf, gdn_conv_buf, s0, kv_past, win_buf, ffn_buf = past
    x2 = x.reshape(b * t, D_MODEL)
    proj2 = dense(x2, w_in, tm, N_PROJ, "in_proj")
    proj = proj2.reshape(b, t, N_PROJ)
    col = lambda c, n: proj[..., c:c + n]
    a_b, a_c, a_h = col(COL_AB, A_WIDTH), col(COL_AC, A_WIDTH), col(COL_AH, A_WIDTH)
    g_qkv, g_gate = col(COL_GQKV, GDN_CONV_CH), col(COL_GGATE, GDN_WIDTH)
    g_a, g_b = col(COL_SMALL + SM_GA, GDN_HEADS), col(COL_SMALL + SM_GB, GDN_HEADS)
    n_q, n_kv, n_win = col(COL_Q, NSA_WIDTH), col(COL_KV, 4 * NSA_KV_DIM), col(COL_WIN, 2 * NSA_KV_DIM)
    n_gate = col(COL_SMALL + SM_NG, 3 * NSA_HEADS)
    z, conv_a_new = causal_dwconv(a_c * a_h, conv_a_buf, conv_a_w)
    y_a = a_b * z
    qkv, gdn_conv_new = causal_dwconv(g_qkv, gdn_conv_buf, gdn_conv_w)
    qkv = jax.nn.silu(qkv)
    q_b, k_b, v_b = jnp.split(qkv, [GDN_QK, 2 * GDN_QK], axis=-1)
    o_b, s_new = gated_delta_rule(q_b.reshape(b, t, GDN_HEADS, GDN_DK), k_b.reshape(b, t, GDN_HEADS, GDN_DK),
                                  v_b.reshape(b, t, GDN_HEADS, GDN_DV), g_a, g_b, s0, gdn_a_log, gdn_dt_bias)
    y_b = gated_rms_norm(o_b, gdn_norm_g, g_gate.reshape(b, t, GDN_HEADS, GDN_DV)).reshape(b, t, GDN_WIDTH)
    kv_new = n_kv.reshape(b, t, 4, NSA_KV_HEADS, NSA_HD)
    win_new = n_win.reshape(b, t, 2, NSA_KV_HEADS, NSA_HD)
    if past is None:
        y_c = nsa_prompt(proj2, b, t, compress_params(cmp_pe, cmp_w1, cmp_b1, cmp_w2)).reshape(b, t, NSA_WIDTH)
        win_state = win_new[:, t - min(WINDOW, t):]
    else:
        qg = n_q.reshape(b, t, NSA_KV_HEADS, NSA_GROUP, NSA_HD)
        kv_all = jnp.concatenate([kv_past, kv_new], axis=1)
        o_cmp, o_slc = nsa_compressed_selected(qg, kv_all, q_pos, cmp_pe, cmp_w1, cmp_b1, cmp_w2)
        lw = win_buf.shape[1]
        win_all = jnp.concatenate([win_buf, win_new], axis=1)
        k_pos = (q_pos[0] - lw) + jnp.arange(lw + t)
        o_win = window_dense(qg, win_all, q_pos, k_pos)
        win_state = win_all[:, t:]
        gates = jax.nn.sigmoid(n_gate.reshape(b, t, 3, NSA_KV_HEADS, NSA_GROUP))[..., None]
        y_c = (gates[:, :, 0] * o_cmp + gates[:, :, 1] * o_slc + gates[:, :, 2] * o_win).reshape(b, t, NSA_WIDTH)
    y = jnp.concatenate([y_a, y_b, y_c], axis=-1).reshape(b * t, D_MODEL)
    x1 = dense_res_ln(y, w_out, x2, ln1_g, ln1_b, tm, "out_proj_ln")
    up = dense(x1, w_up, tm, D_FF, "ffn_up").reshape(b, t, 2 * D_FF)
    gate, val = jnp.split(up, 2, axis=-1)
    gate, ffn_new = causal_dwconv(gate, ffn_buf, ffn_conv_w)
    hmid = (jax.nn.silu(gate) * val).reshape(b * t, D_FF)
    x2o = dense_res_ln(hmid, w_down, x1, ln2_g, ln2_b, tm, "ffn_down_ln")
    return x2o.reshape(b, t, D_MODEL), (kv_new, win_state, conv_a_new, gdn_conv_new, s_new.astype(x.dtype), ffn_new)


def stack_layers(states, i):
    return jnp.stack([s[i] for s in states], axis=0)


def kernel(x_prompt, x_sample, cache_nsa_kv, cache_nsa_win, state_conv_a, state_gdn_conv, state_gdn, state_ffn_conv, page_table, ln_emb_g, ln_emb_b, w_in, conv_a_w, gdn_conv_w, gdn_a_log, gdn_dt_bias, gdn_norm_g, cmp_pe, cmp_w1, cmp_b1, cmp_w2, w_out, ln1_g, ln1_b, w_up, ffn_conv_w, w_down, ln2_g, ln2_b):
    dec_b, dec_t = x_sample.shape[:2]
    past_len = page_table.shape[1] * cache_nsa_kv.shape[2]
    pos_p = jnp.arange(x_prompt.shape[1], dtype=jnp.int32)
    pos_s = past_len + jnp.arange(dec_t, dtype=jnp.int32)
    xp = layer_norm_rows(x_prompt.reshape(-1, D_MODEL), ln_emb_g, ln_emb_b).reshape(x_prompt.shape)
    xs = layer_norm_rows(x_sample.reshape(-1, D_MODEL), ln_emb_g, ln_emb_b).reshape(x_sample.shape)
    w_in_b, w_out_b, w_up_b, w_down_b = (w.astype(BF16) for w in (permute_w_in(w_in), w_out, w_up, w_down))
    st_p, st_s = [], []
    for l in range(DEPTH):
        wl = (w_in_b[l], conv_a_w[l], gdn_conv_w[l], gdn_a_log[l], gdn_dt_bias[l], gdn_norm_g[l],
              cmp_pe[l], cmp_w1[l], cmp_b1[l], cmp_w2[l], w_out_b[l], ln1_g[l], ln1_b[l],
              w_up_b[l], ffn_conv_w[l], w_down_b[l], ln2_g[l], ln2_b[l])
        kv_past = cache_nsa_kv[l, page_table].reshape(dec_b, past_len, 4, NSA_KV_HEADS, NSA_HD)
        past = (state_conv_a[l], state_gdn_conv[l], state_gdn[l], kv_past, cache_nsa_win[l], state_ffn_conv[l])
        xp, sp = layer(xp, pos_p, None, *wl)
        xs, ss = layer(xs, pos_s, past, *wl)
        st_p.append(sp)
        st_s.append(ss)
    return (xp, xs,
            stack_layers(st_p, 0), stack_layers(st_s, 0),
            stack_layers(st_p, 1), stack_layers(st_s, 1),
            stack_layers(st_p, 2), stack_layers(st_s, 2),
            stack_layers(st_p, 3), stack_layers(st_s, 3),
            stack_layers(st_p, 4), stack_layers(st_s, 4),
            stack_layers(st_p, 5), stack_layers(st_s, 5))
```

```python
import functools
import math

import jax
import jax.numpy as jnp
import numpy as np
from jax import lax
from jax.experimental import pallas as pl
from jax.experimental.pallas import tpu as pltpu

F32 = jnp.float32
BF16 = jnp.bfloat16

D_MODEL = 1024
DEPTH = 4
HEAD_DIM = 64
A_WIDTH = 256
A_CONV = 3
GDN_WIDTH = 256
GDN_HEADS = 4
GDN_DK = 64
GDN_DV = 64
GDN_QK = 256
GDN_CONV = 4
GDN_CONV_CH = 768
GDN_CHUNK = 64
NSA_WIDTH = 512
NSA_HEADS = 8
NSA_KV_HEADS = 2
NSA_GROUP = 4
NSA_HD = 64
NSA_KV_DIM = 128
CMP_STRIDE = 16
CMP_BLOCK = 32
SEL_BLOCK = 64
SEL_TOPN = 8
WINDOW = 512
Q_BLOCK = 128
D_FF = 2816
FFN_CONV = 3
ALPHA = (2.0 * DEPTH) ** 0.25
LN_EPS = 1e-5
IN_SIZES = (A_WIDTH, A_WIDTH, A_WIDTH, GDN_QK, GDN_QK, GDN_WIDTH, GDN_WIDTH, GDN_HEADS, GDN_HEADS,
            NSA_WIDTH, 4 * NSA_KV_DIM, 2 * NSA_KV_DIM, 3 * NSA_HEADS)
N_IN = sum(IN_SIZES)

VMEM_LIMIT_BYTES = 56 * 1024 * 1024


def _ln_rows(z, g, b):
    mu = jnp.mean(z, axis=-1, keepdims=True)
    zc = z - mu
    var = jnp.mean(zc * zc, axis=-1, keepdims=True)
    return zc * lax.rsqrt(var + LN_EPS) * g + b


def _ln_kernel(x_ref, g_ref, b_ref, o_ref):
    o_ref[...] = _ln_rows(x_ref[...], g_ref[...], b_ref[...])


def layer_norm_rows(x, g, b, tm=512):
    m, d = x.shape
    return pl.pallas_call(
        _ln_kernel,
        grid=(m // tm,),
        in_specs=[pl.BlockSpec((tm, d), lambda i: (i, 0)),
                  pl.BlockSpec((1, d), lambda i: (0, 0)),
                  pl.BlockSpec((1, d), lambda i: (0, 0))],
        out_specs=pl.BlockSpec((tm, d), lambda i: (i, 0)),
        out_shape=jax.ShapeDtypeStruct((m, d), F32),
        name="ln_rows",
    )(x, g.reshape(1, d), b.reshape(1, d))


def _dense_kernel(x_ref, w_ref, o_ref):
    o_ref[...] = jnp.dot(x_ref[...].astype(BF16), w_ref[...], preferred_element_type=F32)


def dense(x, w, tm, tn, name):
    m, k = x.shape
    n = w.shape[1]
    return pl.pallas_call(
        _dense_kernel,
        grid=(n // tn, m // tm),
        in_specs=[pl.BlockSpec((tm, k), lambda j, i: (i, 0)),
                  pl.BlockSpec((k, tn), lambda j, i: (0, j))],
        out_specs=pl.BlockSpec((tm, tn), lambda j, i: (i, j)),
        out_shape=jax.ShapeDtypeStruct((m, n), F32),
        compiler_params=pltpu.CompilerParams(vmem_limit_bytes=VMEM_LIMIT_BYTES),
        name=name,
    )(x, w)


def _dense_res_ln_kernel(y_ref, w_ref, x_ref, g_ref, b_ref, o_ref):
    acc = jnp.dot(y_ref[...].astype(BF16), w_ref[...], preferred_element_type=F32)
    o_ref[...] = _ln_rows(ALPHA * x_ref[...] + acc, g_ref[...], b_ref[...])


def dense_res_ln(y, w, x, g, b, tm, name):
    m, k = y.shape
    d = w.shape[1]
    return pl.pallas_call(
        _dense_res_ln_kernel,
        grid=(m // tm,),
        in_specs=[pl.BlockSpec((tm, k), lambda i: (i, 0)),
                  pl.BlockSpec((k, d), lambda i: (0, 0)),
                  pl.BlockSpec((tm, d), lambda i: (i, 0)),
                  pl.BlockSpec((1, d), lambda i: (0, 0)),
                  pl.BlockSpec((1, d), lambda i: (0, 0))],
        out_specs=pl.BlockSpec((tm, d), lambda i: (i, 0)),
        out_shape=jax.ShapeDtypeStruct((m, d), F32),
        compiler_params=pltpu.CompilerParams(vmem_limit_bytes=VMEM_LIMIT_BYTES),
        name=name,
    )(y, w, x, g.reshape(1, d), b.reshape(1, d))


SUBLANES = 8


def _conv_rows_carry(x, tail, w_ref, ksize):
    row8 = lax.broadcasted_iota(jnp.int32, (SUBLANES, x.shape[1]), 0)
    y = x * w_ref[ksize - 1:ksize, :]
    for k in range(1, ksize):
        rolled = pltpu.roll(x, k, axis=0)
        first = jnp.where(row8 < k, pltpu.roll(tail, k, axis=0), rolled[0:SUBLANES])
        y = y + jnp.concatenate([first, rolled[SUBLANES:]], axis=0) * w_ref[ksize - 1 - k:ksize - k, :]
    return y


def _conv_rows_blocked(x, buf, w_ref, ksize, step):
    r = x.shape[0]
    y = x * w_ref[ksize - 1:ksize, :]
    for k in range(1, ksize):
        prev = jnp.concatenate([buf[(ksize - 1 - k) * step:(ksize - 1) * step], x[0:r - k * step]], axis=0)
        y = y + prev * w_ref[ksize - 1 - k:ksize - k, :]
    return y


def _conv_tile(x, buf_ref, st_ref, tail_s, w_ref, ksize, mode):
    kind, param = mode
    if kind == "carry":
        tm = x.shape[0]

        @pl.when(pl.program_id(0) % param == 0)
        def _sequence_start():
            tail_s[...] = buf_ref[0]

        y = _conv_rows_carry(x, tail_s[...], w_ref, ksize)
        tail_s[...] = x[tm - SUBLANES:tm]
        st_ref[0] = x[tm - SUBLANES:tm]
        return y
    y = _conv_rows_blocked(x, buf_ref[0], w_ref, ksize, param)
    st_ref[0] = x[x.shape[0] - (ksize - 1) * param:]
    return y


def _conv_specs(mode, ksize, c, tm):
    kind, param = mode
    if kind == "carry":
        rows = SUBLANES
        idx = lambda i: (i // param, 0, 0)
    else:
        rows = (ksize - 1) * param
        idx = lambda i: (0, 0, 0)
    return pl.BlockSpec((1, rows, c), idx), rows


def _mix_out_ln_kernel(ab_ref, ac_ref, ah_ref, yb_ref, yc_ref, x_ref, w_ref, cw_ref, g_ref, b_ref, buf_ref,
                       o_ref, st_ref, tail_s, *, mode):
    u = ac_ref[...] * ah_ref[...]
    z = _conv_tile(u, buf_ref, st_ref, tail_s, cw_ref, A_CONV, mode)
    y = jnp.concatenate([ab_ref[...] * z, yb_ref[...], yc_ref[...]], axis=1).astype(BF16)
    acc = jnp.dot(y, w_ref[...], preferred_element_type=F32)
    o_ref[...] = _ln_rows(ALPHA * x_ref[...] + acc, g_ref[...], b_ref[...])


def mix_out_ln(proj, y_b, y_c, x, w_out, conv_w, g, b, buf, tm, mode):
    m = x.shape[0]
    buf_spec, st_rows = _conv_specs(mode, A_CONV, A_WIDTH, tm)
    n_st = buf.shape[0]
    row = lambda c, w: pl.BlockSpec((tm, w), lambda i: (i, c // w))
    const = lambda shape: pl.BlockSpec(shape, lambda i: (0,) * len(shape))
    return pl.pallas_call(
        functools.partial(_mix_out_ln_kernel, mode=mode),
        grid=(m // tm,),
        in_specs=[row(COL_AB, A_WIDTH), row(COL_AC, A_WIDTH), row(COL_AH, A_WIDTH),
                  row(0, GDN_WIDTH), row(0, NSA_WIDTH), row(0, D_MODEL),
                  const((D_MODEL, D_MODEL)), const((A_CONV, A_WIDTH)), const((1, D_MODEL)), const((1, D_MODEL)),
                  buf_spec],
        out_specs=[row(0, D_MODEL), pl.BlockSpec((1, st_rows, A_WIDTH), buf_spec.index_map)],
        out_shape=[jax.ShapeDtypeStruct((m, D_MODEL), F32), jax.ShapeDtypeStruct((n_st, st_rows, A_WIDTH), F32)],
        scratch_shapes=[pltpu.VMEM((SUBLANES, A_WIDTH), F32)],
        compiler_params=pltpu.CompilerParams(dimension_semantics=("arbitrary",), vmem_limit_bytes=VMEM_LIMIT_BYTES),
        name="mix_out_ln",
    )(proj, proj, proj, y_b, y_c, x, w_out, conv_w, g.reshape(1, -1), b.reshape(1, -1), buf)


def _ffn_up_kernel(x_ref, wg_ref, wv_ref, cw_ref, buf_ref, h_ref, st_ref, tail_s, *, mode):
    xb = x_ref[...].astype(BF16)
    gate = jnp.dot(xb, wg_ref[...], preferred_element_type=F32)
    val = jnp.dot(xb, wv_ref[...], preferred_element_type=F32)
    gate = _conv_tile(gate, buf_ref, st_ref, tail_s, cw_ref, FFN_CONV, mode)
    h_ref[...] = (jax.nn.silu(gate) * val).astype(BF16)


def ffn_up_act(x, w_up, conv_w, buf, tm, tn, mode):
    m = x.shape[0]
    kind, param = mode
    nj = D_FF // tn
    if kind == "carry":
        assert nj == 1
        st_rows, st_idx = SUBLANES, (lambda i, j: (i // param, 0, j))
    else:
        assert tm == m
        st_rows, st_idx = (FFN_CONV - 1) * param, (lambda i, j: (0, 0, j))
    n_st = buf.shape[0]
    once = pl.Buffered(1) if nj == 1 else None
    return pl.pallas_call(
        functools.partial(_ffn_up_kernel, mode=mode),
        grid=(m // tm, nj),
        in_specs=[pl.BlockSpec((tm, D_MODEL), lambda i, j: (i, 0)),
                  pl.BlockSpec((D_MODEL, tn), lambda i, j: (0, j), pipeline_mode=once),
                  pl.BlockSpec((D_MODEL, tn), lambda i, j: (0, nj + j), pipeline_mode=once),
                  pl.BlockSpec((FFN_CONV, tn), lambda i, j: (0, j)),
                  pl.BlockSpec((1, st_rows, tn), st_idx)],
        out_specs=[pl.BlockSpec((tm, tn), lambda i, j: (i, j)),
                   pl.BlockSpec((1, st_rows, tn), st_idx)],
        out_shape=[jax.ShapeDtypeStruct((m, D_FF), BF16), jax.ShapeDtypeStruct((n_st, st_rows, D_FF), F32)],
        scratch_shapes=[pltpu.VMEM((SUBLANES, tn), F32)],
        compiler_params=pltpu.CompilerParams(dimension_semantics=("arbitrary", "arbitrary"),
                                             vmem_limit_bytes=VMEM_LIMIT_BYTES),
        name="ffn_up_act",
    )(x, w_up, w_up, conv_w, buf)


def _conv_silu_kernel(x_ref, cw_ref, buf_ref, o_ref, st_ref, tail_s, *, mode):
    o_ref[...] = jax.nn.silu(_conv_tile(x_ref[...], buf_ref, st_ref, tail_s, cw_ref, GDN_CONV, mode))


def gdn_conv_silu(proj, conv_w, buf, tm, mode):
    m = proj.shape[0]
    buf_spec, st_rows = _conv_specs(mode, GDN_CONV, GDN_CONV_CH, tm)
    n_st = buf.shape[0]
    return pl.pallas_call(
        functools.partial(_conv_silu_kernel, mode=mode),
        grid=(m // tm,),
        in_specs=[pl.BlockSpec((tm, GDN_CONV_CH), lambda i: (i, COL_GQKV // GDN_CONV_CH)),
                  pl.BlockSpec((GDN_CONV, GDN_CONV_CH), lambda i: (0, 0)),
                  buf_spec],
        out_specs=[pl.BlockSpec((tm, GDN_CONV_CH), lambda i: (i, 0)),
                   pl.BlockSpec((1, st_rows, GDN_CONV_CH), buf_spec.index_map)],
        out_shape=[jax.ShapeDtypeStruct((m, GDN_CONV_CH), F32),
                   jax.ShapeDtypeStruct((n_st, st_rows, GDN_CONV_CH), F32)],
        scratch_shapes=[pltpu.VMEM((SUBLANES, GDN_CONV_CH), F32)],
        compiler_params=pltpu.CompilerParams(dimension_semantics=("arbitrary",), vmem_limit_bytes=VMEM_LIMIT_BYTES),
        name="gdn_conv_silu",
    )(proj, conv_w, buf)


GDN_ROWS = 128


def _mm(a, b):
    return jnp.dot(a, b, preferred_element_type=F32)


def _split3(x):
    hi = x.astype(BF16)
    r1 = x - hi.astype(F32)
    mid = r1.astype(BF16)
    return hi, mid, (r1 - mid.astype(F32)).astype(BF16)


def _mm_exact_lhs(c, x):
    hi, mid, lo = _split3(x)
    return _mm(c, hi) + _mm(c, mid) + _mm(c, lo)


def _mm_exact_rhs(x, c):
    hi, mid, lo = _split3(x)
    return _mm(hi, c) + _mm(mid, c) + _mm(lo, c)


def _mm3(a, b):
    ah = a.astype(BF16)
    al = (a - ah.astype(F32)).astype(BF16)
    bh = b.astype(BF16)
    bl = (b - bh.astype(F32)).astype(BF16)
    return _mm(ah, bh) + _mm(ah, bl) + _mm(al, bh)


def _gdn_kernel(qkv_ref, gate_ref, sm_ref, alog_ref, dt_ref, gain_ref, ea_ref, eb_ref, tril_ref, cones_ref,
                hones_ref, s0_ref, y_ref, sout_ref, s_s, o_s, *, chunk, carry_state, tiles_per_seq):
    rws = GDN_ROWS
    nchunk = rws // chunk
    shift = chunk.bit_length() - 1
    nt = (((1,), (1,)), ((), ()))
    tn = (((0,), (0,)), ((), ()))

    if carry_state:
        @pl.when(pl.program_id(0) % tiles_per_seq == 0)
        def _sequence_start():
            s_s[...] = s0_ref[0]

    hones = hones_ref[...]

    def head_sum(x):
        xh = x.astype(BF16)
        return _mm(xh, hones) + _mm((x - xh.astype(F32)).astype(BF16), hones)

    qkv = qkv_ref[...]
    q, k, v = qkv[:, 0:GDN_QK], qkv[:, GDN_QK:2 * GDN_QK], qkv[:, 2 * GDN_QK:]
    q = q * lax.rsqrt(head_sum(q * q) + 1e-6) * (GDN_DK ** -0.5)
    k = k * lax.rsqrt(head_sum(k * k) + 1e-6)
    sm = sm_ref[...]
    g = -jnp.exp(alog_ref[...]) * jax.nn.softplus(_mm_exact_rhs(sm, ea_ref[...]) + dt_ref[...])
    beta = jax.nn.sigmoid(_mm_exact_rhs(sm, eb_ref[...]))
    gc = _mm_exact_lhs(tril_ref[...], g)
    gcl = _mm_exact_lhs(cones_ref[...], g)

    row = lax.broadcasted_iota(jnp.int32, (rws, rws), 0)
    col = lax.broadcasted_iota(jnp.int32, (rws, rws), 1)
    same = lax.shift_right_logical(row, shift) == lax.shift_right_logical(col, shift)
    eye = jnp.where(row == col, 1.0, 0.0)

    for h in range(GDN_HEADS):
        sl = slice(GDN_DK * h, GDN_DK * (h + 1))
        x = gc[:, 128 * h:128 * (h + 1)]
        gcol = x[:, 0:GDN_DK]
        gend = gcl[:, 128 * h:128 * h + GDN_DK]
        beta_h = beta[:, 128 * h:128 * h + GDN_DK]
        incl = same & (row >= col)
        decay = jnp.where(incl, jnp.exp(jnp.where(incl, x - x.T, 0.0)), 0.0)
        qh, kh, vh = q[:, sl], k[:, sl], v[:, sl]
        kb = kh * beta_h
        khb = kh.astype(BF16)
        a = jnp.where(same & (row > col), lax.dot_general(kb.astype(BF16), khb, nt, preferred_element_type=F32) * decay, 0.0)
        minv = eye - a
        apow = _mm3(a, a)
        for step in range(shift - 1):
            minv = _mm3(minv, eye + apow)
            if step < shift - 2:
                apow = _mm3(apow, apow)
        egc = jnp.exp(gcol)
        u = _mm3(minv, vh * beta_h)
        w = _mm3(minv, kb * egc)
        a_qk = lax.dot_general(qh.astype(BF16), khb, nt, preferred_element_type=F32) * decay
        q_dec = (qh * egc).astype(BF16)
        k_dec = (kh * jnp.exp(gend - gcol)).astype(BF16)
        g_last = jnp.exp(gend)
        wb = w.astype(BF16)
        v_new, o_state = [], []
        for n in range(nchunk):
            r = slice(n * chunk, (n + 1) * chunk)
            s_old = s_s[h] if carry_state else s0_ref[n, h]
            sb = s_old.astype(BF16)
            vn = u[r] - _mm(wb[r], sb)
            o_state.append(_mm(q_dec[r], sb))
            s_new = s_old * g_last[n * chunk:n * chunk + 1, :] + lax.dot_general(
                k_dec[r], vn.astype(BF16), tn, preferred_element_type=F32)
            if carry_state:
                s_s[h] = s_new
            else:
                sout_ref[n, h] = s_new
            v_new.append(vn)
        v_all = jnp.concatenate(v_new, axis=0) if nchunk > 1 else v_new[0]
        o_all = jnp.concatenate(o_state, axis=0) if nchunk > 1 else o_state[0]
        o_s[:, sl] = o_all + _mm(a_qk.astype(BF16), v_all.astype(BF16))

    if carry_state:
        sout_ref[0] = s_s[...]
    o = o_s[...]
    o = o * lax.rsqrt(head_sum(o * o) * (1.0 / GDN_DV) + 1e-6)
    y_ref[...] = o * gain_ref[...] * jax.nn.silu(gate_ref[...])


def gated_delta(qkv_act, proj, s0, a_log, dt_bias, norm_g, chunk, carry_state, tiles_per_seq):
    m = qkv_act.shape[0]
    rws = GDN_ROWS
    idx = np.arange(rws)
    same = (idx[:, None] // chunk) == (idx[None, :] // chunk)
    tril = jnp.asarray(same & (idx[:, None] >= idx[None, :]), BF16)
    cones = jnp.asarray(same, BF16)
    lane = np.arange(GDN_WIDTH)
    hones = jnp.asarray(lane[:, None] // GDN_DK == lane[None, :] // GDN_DK, BF16)
    lane2 = np.arange(4 * 128) // 128
    smr = np.arange(128)
    ea = jnp.asarray(smr[:, None] == SM_GA + lane2[None, :], BF16)
    eb = jnp.asarray(smr[:, None] == SM_GB + lane2[None, :], BF16)
    alog_x = jnp.repeat(a_log.astype(F32), 128).reshape(1, 512)
    dt_x = jnp.repeat(dt_bias.astype(F32), 128).reshape(1, 512)
    gain_x = jnp.tile(norm_g.astype(F32), GDN_HEADS).reshape(1, GDN_WIDTH)
    if carry_state:
        s_blk, s_idx = (1, GDN_HEADS, GDN_DK, GDN_DV), (lambda i: (i // tiles_per_seq, 0, 0, 0))
    else:
        s_blk, s_idx = (rws // chunk, GDN_HEADS, GDN_DK, GDN_DV), (lambda i: (i, 0, 0, 0))
    const = lambda shape: pl.BlockSpec(shape, lambda i: (0,) * len(shape))
    return pl.pallas_call(
        functools.partial(_gdn_kernel, chunk=chunk, carry_state=carry_state, tiles_per_seq=tiles_per_seq),
        grid=(m // rws,),
        in_specs=[pl.BlockSpec((rws, GDN_CONV_CH), lambda i: (i, 0)),
                  pl.BlockSpec((rws, GDN_WIDTH), lambda i: (i, COL_GGATE // GDN_WIDTH)),
                  pl.BlockSpec((rws, 128), lambda i: (i, COL_SMALL // 128)),
                  const((1, 512)), const((1, 512)), const((1, GDN_WIDTH)),
                  const((128, 512)), const((128, 512)), const((rws, rws)), const((rws, rws)),
                  const((GDN_WIDTH, GDN_WIDTH)),
                  pl.BlockSpec(s_blk, s_idx)],
        out_specs=[pl.BlockSpec((rws, GDN_WIDTH), lambda i: (i, 0)), pl.BlockSpec(s_blk, s_idx)],
        out_shape=[jax.ShapeDtypeStruct((m, GDN_WIDTH), F32), jax.ShapeDtypeStruct(s0.shape, F32)],
        scratch_shapes=[pltpu.VMEM((GDN_HEADS, GDN_DK, GDN_DV), F32), pltpu.VMEM((rws, GDN_WIDTH), F32)],
        compiler_params=pltpu.CompilerParams(dimension_semantics=("arbitrary",), vmem_limit_bytes=VMEM_LIMIT_BYTES),
        name="gated_delta",
    )(qkv_act, proj, proj, alog_x, dt_x, gain_x, ea, eb, tril, cones, hones, s0)


COL_Q = 0
COL_KV = 512
COL_AB = 1024
COL_AC = 1280
COL_GQKV = 1536
COL_AH = 2304
COL_GGATE = 2560
COL_WIN = 2816
COL_SMALL = 3072
N_PROJ = 3200
SM_GA, SM_GB, SM_NG = 0, 4, 8


def permute_w_in(w_in):
    offs = np.concatenate([[0], np.cumsum(IN_SIZES)])
    grp = lambda k: w_in[..., offs[k]:offs[k + 1]]
    pad = jnp.zeros(w_in.shape[:-1] + (N_PROJ - COL_SMALL - 32,), w_in.dtype)
    order = [grp(9), grp(10), grp(0), grp(1), grp(3), grp(4), grp(5), grp(2), grp(6), grp(11),
             grp(7), grp(8), grp(12), pad]
    return jnp.concatenate(order, axis=-1)


NEG = -1e30
KEY_CHUNK = 128


def _block_diag4(a, b):
    z = jnp.zeros_like(a)
    rows = [jnp.concatenate([a, z, z, z], -1), jnp.concatenate([z, a, z, z], -1),
            jnp.concatenate([z, z, b, z], -1), jnp.concatenate([z, z, z, b], -1)]
    return jnp.concatenate(rows, -2)


def compress_params(cmp_pe, cmp_w1, cmp_b1, cmp_w2):
    w1 = _block_diag4(cmp_w1[0], cmp_w1[1]).astype(BF16)
    w2 = _block_diag4(cmp_w2[0], cmp_w2[1]).astype(BF16)
    pe = jnp.concatenate([cmp_pe[0], cmp_pe[0], cmp_pe[1], cmp_pe[1]], -1)
    b1 = jnp.concatenate([cmp_b1[0], cmp_b1[0], cmp_b1[1], cmp_b1[1]], -1).reshape(1, 256)
    return w1, pe, b1, w2


def _compress_rows(row_loader, ncp, w1_ref, pe_ref, b1_ref, w2_ref):
    acc_lo = jnp.zeros((ncp, 256), F32)
    acc_hi = jnp.zeros((ncp, 256), F32)
    for l in range(CMP_STRIDE):
        x = row_loader(l)
        acc_lo += jnp.dot((x + pe_ref[l:l + 1, :]).astype(BF16), w1_ref[l], preferred_element_type=F32)
        acc_hi += jnp.dot((x + pe_ref[l + CMP_STRIDE:l + CMP_STRIDE + 1, :]).astype(BF16),
                          w1_ref[l + CMP_STRIDE], preferred_element_type=F32)
    hid = jax.nn.gelu(acc_lo + pltpu.roll(acc_hi, ncp - 1, axis=0) + b1_ref[...])
    return jnp.dot(hid.astype(BF16), w2_ref[...], preferred_element_type=F32)


def _cmp_attend_and_select(q_rows, kc, vc, pos, ovl, n_heads, tq, ncp, ns):
    s = lax.dot_general(q_rows, kc, (((1,), (1,)), ((), ())), preferred_element_type=F32)
    cend = lax.broadcasted_iota(jnp.int32, (tq, ncp), 1) * CMP_STRIDE + (CMP_BLOCK - 1)
    s3 = jnp.where((cend <= pos)[None], s.reshape(n_heads, tq, ncp), -jnp.inf)
    m = jnp.max(s3, axis=-1, keepdims=True)
    m = jnp.where(m == -jnp.inf, 0.0, m)
    p = jnp.exp(s3 - m)
    pn = p / jnp.maximum(jnp.sum(p, axis=-1, keepdims=True), 1e-30)
    o_cmp = jnp.dot(pn.reshape(n_heads * tq, ncp).astype(BF16), vc, preferred_element_type=F32)
    psum = pn[0]
    for g in range(1, n_heads):
        psum = psum + pn[g]
    p_hi = psum.astype(BF16)
    p_lo = (psum - p_hi.astype(F32)).astype(BF16)
    imp = (jnp.dot(p_hi, ovl, preferred_element_type=F32) + jnp.dot(p_lo, ovl, preferred_element_type=F32))
    blk = lax.broadcasted_iota(jnp.int32, (tq, ns), 1)
    blk_f = blk.astype(F32)
    forced = (blk == 0) | (blk == lax.shift_right_logical(pos, 6))
    valid = blk * SEL_BLOCK <= pos
    v = jnp.where(forced, jnp.inf, jnp.where(valid, imp, -jnp.inf))
    sel = jnp.zeros((tq, ns), F32)
    for _ in range(min(SEL_TOPN, ns)):
        mx = jnp.max(v, axis=-1, keepdims=True)
        idx = jnp.min(jnp.where(v == mx, blk_f, float(ns)), axis=-1, keepdims=True)
        hit = blk_f == idx
        sel = jnp.where(hit, 1.0, sel)
        v = jnp.where(hit, -jnp.inf, v)
    return o_cmp, sel


def _online_step(s, v_aug, m_s, acc_s):
    m_prev = m_s[...]
    m_new = jnp.maximum(m_prev, jnp.max(s, axis=-1, keepdims=True))
    p = jnp.exp(s - m_new)
    acc_s[...] = jnp.exp(m_prev - m_new) * acc_s[...] + jnp.dot(p.astype(BF16), v_aug, preferred_element_type=F32)
    m_s[...] = m_new


def _nsa_prompt_kernel(q_ref, cmp_ref, slc_ref, win_ref, sm_ref, w1_ref, pe_ref, b1_ref, w2_ref, ovl_ref, exp_ref,
                       y_ref, kc_s, vc_s, ks_s, vs_s, kw_s, vw_s, m_s, acc_s, *, t_len):
    tq = KEY_CHUNK
    i = pl.program_id(1)
    ncp = t_len // CMP_STRIDE
    ns = t_len // SEL_BLOCK
    ng = NSA_GROUP
    rows = ng * tq

    @pl.when(i == 0)
    def _prepare_sequence():
        kvc = _compress_rows(lambda l: cmp_ref[0, l], ncp, w1_ref, pe_ref, b1_ref, w2_ref)
        kc_s[...] = kvc[:, 0:128].astype(BF16)
        vc_s[...] = kvc[:, 128:256].astype(BF16)
        ones = jnp.ones((512, 64), BF16)

        def cast_rows(r, carry):
            sl = pl.ds(pl.multiple_of(r * 512, 512), 512)
            for h in range(NSA_KV_HEADS):
                ks_s[h, sl, :] = slc_ref[sl, 64 * h:64 * h + 64].astype(BF16)
                vs_s[h, sl, 0:64] = slc_ref[sl, 128 + 64 * h:192 + 64 * h].astype(BF16)
                vs_s[h, sl, 64:128] = ones
                kw_s[h, sl, :] = win_ref[sl, 64 * h:64 * h + 64].astype(BF16)
                vw_s[h, sl, 0:64] = win_ref[sl, 128 + 64 * h:192 + 64 * h].astype(BF16)
                vw_s[h, sl, 64:128] = ones
            return carry

        lax.fori_loop(0, t_len // 512, cast_rows, 0)

    pos = i * tq + lax.broadcasted_iota(jnp.int32, (tq, 1), 0)
    lane = lax.broadcasted_iota(jnp.int32, (tq, KEY_CHUNK), 1)
    gates = jax.nn.sigmoid(sm_ref[:, SM_NG:SM_NG + 3 * NSA_HEADS])
    nt = (((1,), (1,)), ((), ()))

    for h in range(NSA_KV_HEADS):
        qh = q_ref[:, 256 * h:256 * h + 256] * (NSA_HD ** -0.5)
        q_rows = jnp.concatenate([qh[:, 64 * g:64 * g + 64] for g in range(ng)], axis=0).astype(BF16)
        o_cmp, sel = _cmp_attend_and_select(q_rows, kc_s[:, 64 * h:64 * h + 64], vc_s[:, 64 * h:64 * h + 64],
                                            pos, ovl_ref[...], ng, tq, ncp, ns)
        sel_b = sel.astype(BF16)

        m_s[...] = jnp.full((rows, KEY_CHUNK), NEG, F32)
        acc_s[...] = jnp.zeros((rows, KEY_CHUNK), F32)

        def sel_body(j, carry):
            off = pl.multiple_of(j * KEY_CHUNK, KEY_CHUNK)
            s = lax.dot_general(q_rows, ks_s[h, pl.ds(off, KEY_CHUNK), :], nt, preferred_element_type=F32)
            chosen = jnp.dot(sel_b, exp_ref[:, pl.ds(off, KEY_CHUNK)], preferred_element_type=F32)
            bias = jnp.where(off + lane <= pos, (chosen - 1.0) * (-NEG), NEG)
            s = (s.reshape(ng, tq, KEY_CHUNK) + bias[None]).reshape(rows, KEY_CHUNK)
            _online_step(s, vs_s[h, pl.ds(off, KEY_CHUNK), :], m_s, acc_s)
            return carry

        lax.fori_loop(0, i + 1, sel_body, 0)
        acc = acc_s[...]
        o_slc = acc[:, 0:64] / acc[:, 64:128]

        m_s[...] = jnp.full((rows, KEY_CHUNK), NEG, F32)
        acc_s[...] = jnp.zeros((rows, KEY_CHUNK), F32)
        for d in range(WINDOW // KEY_CHUNK + 1):
            @pl.when(i >= d)
            def _window_chunk():
                off = pl.multiple_of((i - d) * KEY_CHUNK, KEY_CHUNK)
                s = lax.dot_general(q_rows, kw_s[h, pl.ds(off, KEY_CHUNK), :], nt, preferred_element_type=F32)
                kpos = off + lane
                bias = jnp.where(kpos <= pos, jnp.where(kpos > pos - WINDOW, 0.0, NEG), NEG)
                s = (s.reshape(ng, tq, KEY_CHUNK) + bias[None]).reshape(rows, KEY_CHUNK)
                _online_step(s, vw_s[h, pl.ds(off, KEY_CHUNK), :], m_s, acc_s)
        acc = acc_s[...]
        o_win = acc[:, 0:64] / acc[:, 64:128]

        for g in range(ng):
            hh = ng * h + g
            r = slice(g * tq, (g + 1) * tq)
            y_ref[:, 64 * hh:64 * hh + 64] = (gates[:, hh:hh + 1] * o_cmp[r]
                                              + gates[:, NSA_HEADS + hh:NSA_HEADS + hh + 1] * o_slc[r]
                                              + gates[:, 2 * NSA_HEADS + hh:2 * NSA_HEADS + hh + 1] * o_win[r])


def nsa_prompt(proj, n_seq, t_len, cparams):
    tq = KEY_CHUNK
    nt = t_len // tq
    ncp = t_len // CMP_STRIDE
    ns = t_len // SEL_BLOCK
    w1, pe, b1, w2 = cparams
    cstart = np.arange(ncp)[:, None] * CMP_STRIDE
    sstart = np.arange(ns)[None, :] * SEL_BLOCK
    ovl = jnp.asarray((cstart < sstart + SEL_BLOCK) & (cstart + CMP_BLOCK > sstart), BF16)
    expand = jnp.asarray(np.arange(t_len)[None, :] // SEL_BLOCK == np.arange(ns)[:, None], BF16)
    cmp_rows = proj[:, COL_KV:COL_KV + 256].reshape(n_seq, ncp, CMP_STRIDE, 256).transpose(0, 2, 1, 3)
    once = pl.Buffered(1)
    const = lambda shape: pl.BlockSpec(shape, lambda b, i: (0,) * len(shape), pipeline_mode=once)
    return pl.pallas_call(
        functools.partial(_nsa_prompt_kernel, t_len=t_len),
        grid=(n_seq, nt),
        in_specs=[pl.BlockSpec((tq, 512), lambda b, i: (b * nt + i, COL_Q // 512)),
                  pl.BlockSpec((1, CMP_STRIDE, ncp, 256), lambda b, i: (b, 0, 0, 0), pipeline_mode=once),
                  pl.BlockSpec((t_len, 256), lambda b, i: (b, (COL_KV + 256) // 256), pipeline_mode=once),
                  pl.BlockSpec((t_len, 256), lambda b, i: (b, COL_WIN // 256), pipeline_mode=once),
                  pl.BlockSpec((tq, 128), lambda b, i: (b * nt + i, COL_SMALL // 128)),
                  const((CMP_BLOCK, 256, 256)), const((CMP_BLOCK, 256)), const((1, 256)), const((256, 256)),
                  const((ncp, ns)), const((ns, t_len))],
        out_specs=pl.BlockSpec((tq, NSA_WIDTH), lambda b, i: (b * nt + i, 0)),
        out_shape=jax.ShapeDtypeStruct((n_seq * t_len, NSA_WIDTH), F32),
        scratch_shapes=[pltpu.VMEM((ncp, 128), BF16), pltpu.VMEM((ncp, 128), BF16),
                        pltpu.VMEM((NSA_KV_HEADS, t_len, 64), BF16), pltpu.VMEM((NSA_KV_HEADS, t_len, 128), BF16),
                        pltpu.VMEM((NSA_KV_HEADS, t_len, 64), BF16), pltpu.VMEM((NSA_KV_HEADS, t_len, 128), BF16),
                        pltpu.VMEM((NSA_GROUP * tq, KEY_CHUNK), F32), pltpu.VMEM((NSA_GROUP * tq, KEY_CHUNK), F32)],
        compiler_params=pltpu.CompilerParams(dimension_semantics=("arbitrary", "arbitrary"),
                                             vmem_limit_bytes=VMEM_LIMIT_BYTES),
        name="nsa_prompt",
    )(proj, cmp_rows, proj, proj, proj, w1, pe, b1, w2, ovl, expand)


PAGE_ROWS = 128
SEL_LANES = 128


def _softmax_segments(segs):
    m = segs[0].max(axis=-1, keepdims=True)
    for s in segs[1:]:
        m = jnp.maximum(m, s.max(axis=-1, keepdims=True))
    ps = [jnp.exp(s - m) for s in segs]
    den = ps[0].sum(axis=-1, keepdims=True)
    for p in ps[1:]:
        den = den + p.sum(axis=-1, keepdims=True)
    return ps, den


def _nsa_sample_kernel(pt_ref, *refs, n_pages, past_len, dec_t):
    del pt_ref
    pages = refs[:n_pages]
    (q_ref, kvn_ref, wn_ref, sm_ref, wc_ref, w1_ref, pe_ref, b1_ref, w2_ref, ovl_ref, exp_ref,
     y_ref, wst_ref, rk_s, rv_s, newpg_s, neww_s) = refs[n_pages:]
    ncp = past_len // CMP_STRIDE
    ng, nh = NSA_GROUP, NSA_KV_HEADS
    rows_h = ng * dec_t
    rows = nh * rows_h
    nt = (((1,), (1,)), ((), ()))

    for p in range(n_pages):
        rk_s[p * PAGE_ROWS:(p + 1) * PAGE_ROWS, :] = pages[p][:, 0:128]
        rv_s[p * PAGE_ROWS:(p + 1) * PAGE_ROWS, :] = pages[p][:, 128:256]
    kvc = _compress_rows(
        lambda l: jnp.concatenate([rk_s[pl.ds(l, ncp, stride=CMP_STRIDE), :],
                                   rv_s[pl.ds(l, ncp, stride=CMP_STRIDE), :]], axis=1),
        ncp, w1_ref, pe_ref, b1_ref, w2_ref)
    kc = kvc[:, 0:128].astype(BF16)
    vc = kvc[:, 128:256].astype(BF16)

    newpg_s[...] = jnp.zeros((PAGE_ROWS, 256), F32)
    newpg_s[0:dec_t, :] = kvn_ref[0][:, 256:512]
    neww_s[...] = jnp.zeros((PAGE_ROWS, 256), F32)
    neww_s[0:dec_t, :] = wn_ref[0]

    qf = q_ref[0] * (NSA_HD ** -0.5)
    zero = jnp.zeros((dec_t, 64), F32)
    qrows = []
    for h in range(nh):
        for g in range(ng):
            piece = qf[:, 64 * (ng * h + g):64 * (ng * h + g) + 64]
            qrows.append(jnp.concatenate([piece, zero] if h == 0 else [zero, piece], axis=1))
    q_bd = jnp.concatenate(qrows, axis=0).astype(BF16)
    t_row = lax.broadcasted_iota(jnp.int32, (rows, 1), 0) & (dec_t - 1)
    pos = past_len + t_row
    head0 = lax.broadcasted_iota(jnp.int32, (rows, 128), 0) < rows_h
    lane_lo = lax.broadcasted_iota(jnp.int32, (rows, 128), 1) < 64
    own = head0 == lane_lo

    def own_half(x):
        x = jnp.where(own, x, 0.0)
        return x[:, 0:64] + x[:, 64:128]

    s = lax.dot_general(q_bd, kc, nt, preferred_element_type=F32)
    cend = lax.broadcasted_iota(jnp.int32, (rows, ncp), 1) * CMP_STRIDE + (CMP_BLOCK - 1)
    s = jnp.where(cend <= pos, s, -jnp.inf)
    m = jnp.max(s, axis=-1, keepdims=True)
    m = jnp.where(m == -jnp.inf, 0.0, m)
    p = jnp.exp(s - m)
    pn = p / jnp.maximum(jnp.sum(p, axis=-1, keepdims=True), 1e-30)
    o_cmp = own_half(_mm(pn.astype(BF16), vc))

    ovl = ovl_ref[...]
    blk = lax.broadcasted_iota(jnp.int32, (dec_t, SEL_LANES), 1)
    blk_f = blk.astype(F32)
    pos_t = past_len + lax.broadcasted_iota(jnp.int32, (dec_t, 1), 0)
    sel_rows = []
    for h in range(nh):
        psum = pn[h * rows_h:h * rows_h + dec_t]
        for g in range(1, ng):
            psum = psum + pn[h * rows_h + g * dec_t:h * rows_h + (g + 1) * dec_t]
        p_hi = psum.astype(BF16)
        imp = _mm(p_hi, ovl) + _mm((psum - p_hi.astype(F32)).astype(BF16), ovl)
        forced = (blk == 0) | (blk == lax.shift_right_logical(pos_t, 6))
        v = jnp.where(forced, jnp.inf, jnp.where(blk * SEL_BLOCK <= pos_t, imp, -jnp.inf))
        sel = jnp.zeros((dec_t, SEL_LANES), F32)
        for _ in range(SEL_TOPN):
            mx = jnp.max(v, axis=-1, keepdims=True)
            idx = jnp.min(jnp.where(v == mx, blk_f, float(SEL_LANES)), axis=-1, keepdims=True)
            hit = blk_f == idx
            sel = jnp.where(hit, 1.0, sel)
            v = jnp.where(hit, -jnp.inf, v)
        sel_rows += [sel] * ng
    sel_b = jnp.concatenate(sel_rows, axis=0).astype(BF16)

    lane = lax.broadcasted_iota(jnp.int32, (rows, PAGE_ROWS), 1)
    segs = []
    for pg in range(n_pages + 1):
        kpage = (pages[pg][:, 256:384] if pg < n_pages else newpg_s[:, 0:128]).astype(BF16)
        sc = lax.dot_general(q_bd, kpage, nt, preferred_element_type=F32)
        chosen = _mm(sel_b, exp_ref[:, pg * PAGE_ROWS:(pg + 1) * PAGE_ROWS])
        ok = chosen > 0.5
        if pg == n_pages:
            ok = ok & (past_len + lane <= pos)
        segs.append(jnp.where(ok, sc, NEG))
    ps, den = _softmax_segments(segs)
    acc = jnp.zeros((rows, 128), F32)
    for pg in range(n_pages + 1):
        vpage = (pages[pg][:, 384:512] if pg < n_pages else newpg_s[:, 128:256]).astype(BF16)
        acc = acc + _mm(ps[pg].astype(BF16), vpage)
    o_slc = own_half(acc) / den

    wlen = wc_ref.shape[1]
    wc = wc_ref[0]
    jw = lax.broadcasted_iota(jnp.int32, (rows, wlen), 1)
    s_old = lax.dot_general(q_bd, wc[:, 0:128].astype(BF16), nt, preferred_element_type=F32)
    s_old = jnp.where(past_len - wlen + jw > pos - WINDOW, s_old, NEG)
    s_new = lax.dot_general(q_bd, neww_s[:, 0:128].astype(BF16), nt, preferred_element_type=F32)
    s_new = jnp.where(past_len + lane <= pos, s_new, NEG)
    ps, den = _softmax_segments([s_old, s_new])
    acc = _mm(ps[0].astype(BF16), wc[:, 128:256].astype(BF16)) + _mm(ps[1].astype(BF16), neww_s[:, 128:256].astype(BF16))
    o_win = own_half(acc) / den

    gates = jax.nn.sigmoid(sm_ref[0][:, SM_NG:SM_NG + 3 * NSA_HEADS])
    for hh in range(NSA_HEADS):
        r = slice(hh * dec_t, (hh + 1) * dec_t)
        y_ref[0, :, 64 * hh:64 * hh + 64] = (gates[:, hh:hh + 1] * o_cmp[r]
                                             + gates[:, NSA_HEADS + hh:NSA_HEADS + hh + 1] * o_slc[r]
                                             + gates[:, 2 * NSA_HEADS + hh:2 * NSA_HEADS + hh + 1] * o_win[r])

    wst_ref[0, 0:wlen - dec_t, :] = wc[dec_t:wlen, :]
    wst_ref[0, wlen - dec_t:wlen, :] = wn_ref[0]


def nsa_sample(q, kv_new, win_new, sm, cache_kv, layer_idx, page_table, win_cache, cparams):
    bsz, dec_t, _ = q.shape
    n_pages = page_table.shape[1]
    past_len = n_pages * PAGE_ROWS
    wlen = win_cache.shape[1]
    assert dec_t < CMP_STRIDE and dec_t % SUBLANES == 0 and dec_t & (dec_t - 1) == 0
    assert wlen == WINDOW and wlen <= past_len
    ncp = past_len // CMP_STRIDE
    ns = -(-(past_len + dec_t) // SEL_BLOCK)
    assert ns <= SEL_LANES
    w1, pe, b1, w2 = cparams
    cstart = np.arange(ncp)[:, None] * CMP_STRIDE
    sstart = np.arange(SEL_LANES)[None, :] * SEL_BLOCK
    ovl = jnp.asarray((cstart < sstart + SEL_BLOCK) & (cstart + CMP_BLOCK > sstart) & (np.arange(SEL_LANES)[None, :] < ns), BF16)
    keys = np.arange((n_pages + 1) * PAGE_ROWS)
    expand = jnp.asarray(keys[None, :] // SEL_BLOCK == np.arange(SEL_LANES)[:, None], BF16)
    const = lambda shape: pl.BlockSpec(shape, lambda b, pt: (0,) * len(shape))
    seq = lambda r, c: pl.BlockSpec((1, r, c), lambda b, pt: (b, 0, 0))
    page_specs = [pl.BlockSpec((None, None, PAGE_ROWS, 512), functools.partial(
        lambda b, pt, p: (layer_idx, pt[b, p], 0, 0), p=p)) for p in range(n_pages)]
    grid_spec = pltpu.PrefetchScalarGridSpec(
        num_scalar_prefetch=1,
        grid=(bsz,),
        in_specs=page_specs + [seq(dec_t, 512), seq(dec_t, 512), seq(dec_t, 256), seq(dec_t, 128), seq(wlen, 256),
                               const((CMP_BLOCK, 256, 256)), const((CMP_BLOCK, 256)), const((1, 256)),
                               const((256, 256)), const((ncp, SEL_LANES)),
                               const((SEL_LANES, (n_pages + 1) * PAGE_ROWS))],
        out_specs=[seq(dec_t, NSA_WIDTH), seq(wlen, 256)],
        scratch_shapes=[pltpu.VMEM((past_len, 128), F32), pltpu.VMEM((past_len, 128), F32),
                        pltpu.VMEM((PAGE_ROWS, 256), F32), pltpu.VMEM((PAGE_ROWS, 256), F32)])
    return pl.pallas_call(
        functools.partial(_nsa_sample_kernel, n_pages=n_pages, past_len=past_len, dec_t=dec_t),
        grid_spec=grid_spec,
        out_shape=[jax.ShapeDtypeStruct((bsz, dec_t, NSA_WIDTH), F32), jax.ShapeDtypeStruct((bsz, wlen, 256), F32)],
        compiler_params=pltpu.CompilerParams(dimension_semantics=("arbitrary",), vmem_limit_bytes=VMEM_LIMIT_BYTES),
        name="nsa_sample",
    )(page_table, *([cache_kv] * n_pages), q, kv_new, win_new, sm, win_cache, w1, pe, b1, w2, ovl, expand)


def l2_normalize(x):
    return x * lax.rsqrt(jnp.sum(jnp.square(x), axis=-1, keepdims=True) + 1e-6)


def gated_rms_norm(o, gain, gate):
    of = o.astype(F32)
    of = of * lax.rsqrt(jnp.mean(jnp.square(of), axis=-1, keepdims=True) + 1e-6)
    return (of * gain.astype(F32) * jax.nn.silu(gate.astype(F32))).astype(gate.dtype)


def masked_softmax(s, mask):
    s = jnp.where(mask, s.astype(F32), -jnp.inf)
    m = jnp.max(s, axis=-1, keepdims=True)
    m = jnp.where(jnp.isfinite(m), m, 0.0)
    p = jnp.exp(s - m)
    return p / jnp.maximum(jnp.sum(p, axis=-1, keepdims=True), 1e-30)


def causal_dwconv(x, buf, w):
    k = w.shape[0]
    t = x.shape[1]
    xp = jnp.concatenate([buf.astype(x.dtype), x], axis=1)
    y = sum(xp[:, i:i + t] * w[i] for i in range(k))
    return y, xp[:, t:]


def split_in(proj):
    return jnp.split(proj, np.cumsum(IN_SIZES)[:-1].tolist(), axis=-1)


def gated_delta_rule(q, k, v, a_in, b_in, s0, a_log, dt_bias):
    b, t, h, dk = q.shape
    dv = v.shape[-1]
    q = l2_normalize(q.astype(F32)) * (dk ** -0.5)
    k = l2_normalize(k.astype(F32))
    v = v.astype(F32)
    g = -jnp.exp(a_log.astype(F32)) * jax.nn.softplus(a_in.astype(F32) + dt_bias.astype(F32))
    beta = jax.nn.sigmoid(b_in.astype(F32))
    c = GDN_CHUNK if t % GDN_CHUNK == 0 else t
    n = t // c

    def to_chunks(z):
        return jnp.moveaxis(z.reshape((b, n, c) + z.shape[2:]), 3, 2)

    qc, kc, vc, gc, bc = (to_chunks(z) for z in (q, k, v, g, beta))
    gc = jnp.cumsum(gc, axis=-1)
    incl = jnp.tril(jnp.ones((c, c), dtype=bool))
    strict = jnp.tril(jnp.ones((c, c), dtype=bool), -1)
    diff = gc[..., :, None] - gc[..., None, :]
    decay = jnp.where(incl, jnp.exp(jnp.where(incl, diff, 0.0)), 0.0)
    kb = kc * bc[..., None]
    m = jnp.eye(c, dtype=F32) + jnp.where(strict, jnp.einsum('bnhik,bnhjk->bnhij', kb, kc) * decay, 0.0)
    rhs = jnp.concatenate([vc * bc[..., None], kb * jnp.exp(gc)[..., None]], axis=-1)
    sol = lax.linalg.triangular_solve(m, rhs, left_side=True, lower=True, unit_diagonal=True)
    u, w = sol[..., :dv], sol[..., dv:]
    a_qk = jnp.einsum('bnhik,bnhjk->bnhij', qc, kc) * decay
    q_dec = qc * jnp.exp(gc)[..., None]
    k_dec = kc * jnp.exp(gc[..., -1:] - gc)[..., None]
    g_last = jnp.exp(gc[..., -1])

    def step(state, xs):
        u_n, w_n, q_n, k_n, a_n, gl_n = xs
        v_new = u_n - jnp.einsum('bhik,bhkv->bhiv', w_n, state)
        o_n = jnp.einsum('bhik,bhkv->bhiv', q_n, state) + jnp.einsum('bhij,bhjv->bhiv', a_n, v_new)
        state = state * gl_n[..., None, None] + jnp.einsum('bhik,bhiv->bhkv', k_n, v_new)
        return state, o_n

    xs = tuple(jnp.moveaxis(z, 1, 0) for z in (u, w, q_dec, k_dec, a_qk, g_last))
    s_final, o = lax.scan(step, s0.astype(F32), xs)
    o = jnp.moveaxis(jnp.moveaxis(o, 0, 1), 2, 3).reshape(b, t, h, dv)
    return o, s_final


def compress_blocks(rows, pe, w1, b1, w2):
    b, t_pad, kvh, hd = rows.shape
    r = rows.reshape(b, t_pad // CMP_STRIDE, CMP_STRIDE, kvh, hd)
    blocks = jnp.concatenate([r[:, :-1], r[:, 1:]], axis=2) + pe[:, None, :]
    hid = jax.nn.gelu(jnp.einsum('bclhd,lde->bche', blocks, w1) + b1)
    return jnp.einsum('bche,ed->bchd', hid, w2)


def select_attend(qg, idx, q_pos, ks, vs):
    b, kvh = ks.shape[:2]
    tq = qg.shape[1]
    n = idx.shape[-1]
    b_ix = jnp.arange(b)[:, None, None, None]
    h_ix = jnp.arange(kvh)[None, :, None, None]
    kg = ks[b_ix, h_ix, idx].reshape(b, kvh, tq, n * SEL_BLOCK, NSA_HD)
    vg = vs[b_ix, h_ix, idx].reshape(b, kvh, tq, n * SEL_BLOCK, NSA_HD)
    k_pos = (idx[..., None] * SEL_BLOCK + jnp.arange(SEL_BLOCK)).reshape(b, kvh, 1, tq, n * SEL_BLOCK)
    s = jnp.einsum('bqhgd,bhqkd->bhgqk', qg, kg) * (NSA_HD ** -0.5)
    p = masked_softmax(s, k_pos <= q_pos[:, None])
    return jnp.einsum('bhgqk,bhqkd->bqhgd', p.astype(vg.dtype), vg)


def nsa_compressed_selected(qg, kv, q_pos, cmp_pe, cmp_w1, cmp_b1, cmp_w2):
    b, t = kv.shape[:2]
    tq = qg.shape[1]
    t_pad = -(-t // SEL_BLOCK) * SEL_BLOCK
    kv = jnp.pad(kv, ((0, 0), (0, t_pad - t), (0, 0), (0, 0), (0, 0)))
    kc = compress_blocks(kv[:, :, 0], cmp_pe[0], cmp_w1[0], cmp_b1[0], cmp_w2[0])
    vc = compress_blocks(kv[:, :, 1], cmp_pe[1], cmp_w1[1], cmp_b1[1], cmp_w2[1])
    nc = kc.shape[1]
    cmp_start = jnp.arange(nc) * CMP_STRIDE
    s = jnp.einsum('bqhgd,bchd->bhgqc', qg, kc) * (NSA_HD ** -0.5)
    p = masked_softmax(s, cmp_start[None, :] + (CMP_BLOCK - 1) <= q_pos[:, None])
    o_cmp = jnp.einsum('bhgqc,bchd->bqhgd', p.astype(vc.dtype), vc)
    ns = t_pad // SEL_BLOCK
    sel_start = jnp.arange(ns) * SEL_BLOCK
    overlap = ((cmp_start[:, None] < sel_start[None, :] + SEL_BLOCK)
               & (cmp_start[:, None] + CMP_BLOCK > sel_start[None, :])).astype(F32)
    imp = jnp.einsum('bhgqc,cn->bhqn', p, overlap)
    blk = jnp.arange(ns)[None, :]
    forced = (blk == 0) | (blk == q_pos[:, None] // SEL_BLOCK)
    valid = sel_start[None, :] <= q_pos[:, None]
    imp = jnp.where(forced, jnp.inf, jnp.where(valid, imp, -jnp.inf))
    n_top = min(SEL_TOPN, ns)
    _, idx = lax.top_k(imp, n_top)
    ks = jnp.moveaxis(kv[:, :, 2].reshape(b, ns, SEL_BLOCK, NSA_KV_HEADS, NSA_HD), 3, 1)
    vs = jnp.moveaxis(kv[:, :, 3].reshape(b, ns, SEL_BLOCK, NSA_KV_HEADS, NSA_HD), 3, 1)
    if tq % Q_BLOCK == 0:
        nb = tq // Q_BLOCK
        qb = jnp.moveaxis(qg.reshape(b, nb, Q_BLOCK, NSA_KV_HEADS, NSA_GROUP, NSA_HD), 1, 0)
        ib = jnp.moveaxis(idx.reshape(b, NSA_KV_HEADS, nb, Q_BLOCK, n_top), 2, 0)
        pb = q_pos.reshape(nb, Q_BLOCK)
        ob = lax.map(lambda a: select_attend(a[0], a[1], a[2], ks, vs), (qb, ib, pb))
        o_slc = jnp.moveaxis(ob, 0, 1).reshape(b, tq, NSA_KV_HEADS, NSA_GROUP, NSA_HD)
    else:
        o_slc = select_attend(qg, idx, q_pos, ks, vs)
    return o_cmp, o_slc


def window_banded(qg, kvw):
    b, t = kvw.shape[:2]
    nb = t // Q_BLOCK
    nw = WINDOW // Q_BLOCK
    kp = jnp.pad(kvw, ((0, 0), (WINDOW, 0), (0, 0), (0, 0), (0, 0))).reshape(b, nb + nw, Q_BLOCK, 2, NSA_KV_HEADS, NSA_HD)
    band = jnp.concatenate([kp[:, i:i + nb] for i in range(nw + 1)], axis=2)
    qb = qg.reshape(b, nb, Q_BLOCK, NSA_KV_HEADS, NSA_GROUP, NSA_HD)
    start = jnp.arange(nb)[:, None] * Q_BLOCK
    qpos = start + jnp.arange(Q_BLOCK)
    kpos = start - WINDOW + jnp.arange((nw + 1) * Q_BLOCK)
    qp, kp_ = qpos[:, :, None], kpos[:, None, :]
    mask = (kp_ <= qp) & (kp_ > qp - WINDOW) & (kp_ >= 0)
    s = jnp.einsum('bnqhgd,bnkhd->bnhgqk', qb, band[:, :, :, 0]) * (NSA_HD ** -0.5)
    p = masked_softmax(s, mask[None, :, None, None])
    o = jnp.einsum('bnhgqk,bnkhd->bnqhgd', p.astype(band.dtype), band[:, :, :, 1])
    return o.reshape(b, t, NSA_KV_HEADS, NSA_GROUP, NSA_HD)


def window_dense(qg, kvw_all, q_pos, k_pos):
    s = jnp.einsum('bqhgd,bkhd->bhgqk', qg, kvw_all[:, :, 0]) * (NSA_HD ** -0.5)
    mask = (k_pos[None, :] <= q_pos[:, None]) & (k_pos[None, :] > q_pos[:, None] - WINDOW)
    p = masked_softmax(s, mask)
    return jnp.einsum('bhgqk,bkhd->bqhgd', p.astype(kvw_all.dtype), kvw_all[:, :, 1])


def layer(x, q_pos, past, w_in, conv_a_w, gdn_conv_w, gdn_a_log, gdn_dt_bias, gdn_norm_g,
          cmp_pe, cmp_w1, cmp_b1, cmp_w2, w_out, ln1_g, ln1_b, w_up, ffn_conv_w, w_down, ln2_g, ln2_b):
    raise NotImplementedError


TM = 512


def _to_tb(x, b, t):
    return x.reshape(b, t, -1).transpose(1, 0, 2).reshape(t * b, -1)


def _to_bt(x, b, t):
    return x.reshape(t, b, -1).transpose(1, 0, 2).reshape(b * t, -1)


def _blocked_buf(state):
    b, k1, c = state.shape
    return state.transpose(1, 0, 2).reshape(1, k1 * b, c)


def _unblocked(st, b):
    return st.reshape(-1, b, st.shape[-1]).transpose(1, 0, 2)


def layer_prompt(x, n_seq, t_len, lw):
    tiles = t_len // TM
    carry = ("carry", tiles)
    zeros = lambda c: jnp.zeros((n_seq, SUBLANES, c), F32)
    proj = dense(x, lw["w_in"], TM, N_PROJ, "in_proj")
    qkv, gconv = gdn_conv_silu(proj, lw["gdn_conv_w"], zeros(GDN_CONV_CH), TM, carry)
    y_b, s_new = gated_delta(qkv, proj, jnp.zeros((n_seq, GDN_HEADS, GDN_DK, GDN_DV), F32), lw["gdn_a_log"],
                             lw["gdn_dt_bias"], lw["gdn_norm_g"], GDN_CHUNK, True, t_len // GDN_ROWS)
    y_c = nsa_prompt(proj, n_seq, t_len, lw["cmp"])
    x1, conva = mix_out_ln(proj, y_b, y_c, x, lw["w_out"], lw["conv_a_w"], lw["ln1_g"], lw["ln1_b"],
                           zeros(A_WIDTH), TM, carry)
    h, ffnc = ffn_up_act(x1, lw["w_up"], lw["ffn_conv_w"], zeros(D_FF), TM, D_FF, carry)
    x2 = dense_res_ln(h, lw["w_down"], x1, lw["ln2_g"], lw["ln2_b"], TM, "ffn_down_ln")
    p3 = proj.reshape(n_seq, t_len, N_PROJ)
    kv_new = p3[:, :, COL_KV:COL_KV + 4 * NSA_KV_DIM].reshape(n_seq, t_len, 4, NSA_KV_HEADS, NSA_HD)
    wkeep = min(WINDOW, t_len)
    win_state = p3[:, t_len - wkeep:, COL_WIN:COL_WIN + 2 * NSA_KV_DIM].reshape(n_seq, wkeep, 2, NSA_KV_HEADS, NSA_HD)
    tail = lambda st, k: st[:, SUBLANES - (k - 1):, :]
    return x2, (kv_new, win_state, tail(conva, A_CONV), tail(gconv, GDN_CONV), s_new, tail(ffnc, FFN_CONV))


def layer_sample(x, bsz, dec_t, lw, layer_idx, cache_kv, page_table, win_cache, st_conv_a, st_gdn_conv, st_gdn,
                 st_ffn_conv):
    m = dec_t * bsz
    blocked = ("blocked", bsz)
    proj = dense(x, lw["w_in"], TM, N_PROJ, "in_proj")
    proj_bt = _to_bt(proj, bsz, dec_t)
    qkv, gconv = gdn_conv_silu(proj, lw["gdn_conv_w"], _blocked_buf(st_gdn_conv), m, blocked)
    y_b, s_new = gated_delta(_to_bt(qkv, bsz, dec_t), proj_bt, st_gdn, lw["gdn_a_log"], lw["gdn_dt_bias"],
                             lw["gdn_norm_g"], dec_t, False, 1)
    p3 = proj_bt.reshape(bsz, dec_t, N_PROJ)
    kv_new = p3[:, :, COL_KV:COL_KV + 4 * NSA_KV_DIM]
    y_c, win_state = nsa_sample(p3[:, :, COL_Q:COL_Q + NSA_WIDTH], kv_new,
                                p3[:, :, COL_WIN:COL_WIN + 2 * NSA_KV_DIM], p3[:, :, COL_SMALL:COL_SMALL + 128],
                                cache_kv, layer_idx, page_table, win_cache, lw["cmp"])
    x1, conva = mix_out_ln(proj, _to_tb(y_b, bsz, dec_t), _to_tb(y_c, bsz, dec_t), x, lw["w_out"], lw["conv_a_w"],
                           lw["ln1_g"], lw["ln1_b"], _blocked_buf(st_conv_a), m, blocked)
    h, ffnc = ffn_up_act(x1, lw["w_up"], lw["ffn_conv_w"], _blocked_buf(st_ffn_conv), m, D_FF // 2, blocked)
    x2 = dense_res_ln(h, lw["w_down"], x1, lw["ln2_g"], lw["ln2_b"], TM, "ffn_down_ln")
    wlen = win_cache.shape[1]
    return x2, (kv_new.reshape(bsz, dec_t, 4, NSA_KV_HEADS, NSA_HD),
                win_state.reshape(bsz, wlen, 2, NSA_KV_HEADS, NSA_HD),
                _unblocked(conva, bsz), _unblocked(gconv, bsz), s_new, _unblocked(ffnc, bsz))


def stack_layers(states, i):
    return jnp.stack([s[i] for s in states], axis=0)


def kernel(x_prompt, x_sample, cache_nsa_kv, cache_nsa_win, state_conv_a, state_gdn_conv, state_gdn, state_ffn_conv, page_table, ln_emb_g, ln_emb_b, w_in, conv_a_w, gdn_conv_w, gdn_a_log, gdn_dt_bias, gdn_norm_g, cmp_pe, cmp_w1, cmp_b1, cmp_w2, w_out, ln1_g, ln1_b, w_up, ffn_conv_w, w_down, ln2_g, ln2_b):
    n_seq, t_len = x_prompt.shape[:2]
    dec_b, dec_t = x_sample.shape[:2]
    depth = w_in.shape[0]
    xp = layer_norm_rows(x_prompt.reshape(-1, D_MODEL), ln_emb_g, ln_emb_b)
    xs = layer_norm_rows(_to_tb(x_sample.reshape(-1, D_MODEL), dec_b, dec_t), ln_emb_g, ln_emb_b)
    w_in_b, w_out_b, w_up_b, w_down_b = (w.astype(BF16) for w in (permute_w_in(w_in), w_out, w_up, w_down))
    cache_kv = cache_nsa_kv.reshape(cache_nsa_kv.shape[:3] + (4 * NSA_KV_DIM,))
    win_cache = cache_nsa_win.reshape(cache_nsa_win.shape[:3] + (2 * NSA_KV_DIM,))
    st_p, st_s = [], []
    for l in range(depth):
        lw = dict(w_in=w_in_b[l], conv_a_w=conv_a_w[l], gdn_conv_w=gdn_conv_w[l], gdn_a_log=gdn_a_log[l],
                  gdn_dt_bias=gdn_dt_bias[l], gdn_norm_g=gdn_norm_g[l],
                  cmp=compress_params(cmp_pe[l], cmp_w1[l], cmp_b1[l], cmp_w2[l]),
                  w_out=w_out_b[l], ln1_g=ln1_g[l], ln1_b=ln1_b[l], w_up=w_up_b[l], ffn_conv_w=ffn_conv_w[l],
                  w_down=w_down_b[l], ln2_g=ln2_g[l], ln2_b=ln2_b[l])
        xp, sp = layer_prompt(xp, n_seq, t_len, lw)
        xs, ss = layer_sample(xs, dec_b, dec_t, lw, l, cache_kv, page_table, win_cache[l], state_conv_a[l],
                              state_gdn_conv[l], state_gdn[l], state_ffn_conv[l])
        st_p.append(sp)
        st_s.append(ss)
    xp = xp.reshape(n_seq, t_len, D_MODEL)
    xs = _to_bt(xs, dec_b, dec_t).reshape(dec_b, dec_t, D_MODEL)
    return (xp, xs,
            stack_layers(st_p, 0), stack_layers(st_s, 0),
            stack_layers(st_p, 1), stack_layers(st_s, 1),
            stack_layers(st_p, 2), stack_layers(st_s, 2),
            stack_layers(st_p, 3), stack_layers(st_s, 3),
            stack_layers(st_p, 4), stack_layers(st_s, 4),
            stack_layers(st_p, 5), stack_layers(st_s, 5))
```

```python
import functools
import math

import jax
import jax.numpy as jnp
import numpy as np
from jax import lax
from jax.experimental import pallas as pl
from jax.experimental.pallas import tpu as pltpu

F32 = jnp.float32
BF16 = jnp.bfloat16

D_MODEL = 1024
DEPTH = 4
HEAD_DIM = 64
A_WIDTH = 256
A_CONV = 3
GDN_WIDTH = 256
GDN_HEADS = 4
GDN_DK = 64
GDN_DV = 64
GDN_QK = 256
GDN_CONV = 4
GDN_CONV_CH = 768
GDN_CHUNK = 64
NSA_WIDTH = 512
NSA_HEADS = 8
NSA_KV_HEADS = 2
NSA_GROUP = 4
NSA_HD = 64
NSA_KV_DIM = 128
CMP_STRIDE = 16
CMP_BLOCK = 32
SEL_BLOCK = 64
SEL_TOPN = 8
WINDOW = 512
Q_BLOCK = 128
D_FF = 2816
FFN_CONV = 3
ALPHA = (2.0 * DEPTH) ** 0.25
LN_EPS = 1e-5
IN_SIZES = (A_WIDTH, A_WIDTH, A_WIDTH, GDN_QK, GDN_QK, GDN_WIDTH, GDN_WIDTH, GDN_HEADS, GDN_HEADS,
            NSA_WIDTH, 4 * NSA_KV_DIM, 2 * NSA_KV_DIM, 3 * NSA_HEADS)
N_IN = sum(IN_SIZES)

VMEM_LIMIT_BYTES = 56 * 1024 * 1024


def _ln_rows(z, g, b):
    mu = jnp.mean(z, axis=-1, keepdims=True)
    zc = z - mu
    var = jnp.mean(zc * zc, axis=-1, keepdims=True)
    return zc * lax.rsqrt(var + LN_EPS) * g + b


def _ln_kernel(x_ref, g_ref, b_ref, o_ref):
    o_ref[...] = _ln_rows(x_ref[...], g_ref[...], b_ref[...])


def layer_norm_rows(x, g, b, tm=512):
    m, d = x.shape
    return pl.pallas_call(
        _ln_kernel,
        grid=(m // tm,),
        in_specs=[pl.BlockSpec((tm, d), lambda i: (i, 0)),
                  pl.BlockSpec((1, d), lambda i: (0, 0)),
                  pl.BlockSpec((1, d), lambda i: (0, 0))],
        out_specs=pl.BlockSpec((tm, d), lambda i: (i, 0)),
        out_shape=jax.ShapeDtypeStruct((m, d), F32),
        name="ln_rows",
    )(x, g.reshape(1, d), b.reshape(1, d))


def _dense_kernel(x_ref, w_ref, o_ref):
    o_ref[...] = jnp.dot(x_ref[...].astype(BF16), w_ref[...], preferred_element_type=F32)


def dense(x, w, tm, tn, name):
    m, k = x.shape
    n = w.shape[1]
    return pl.pallas_call(
        _dense_kernel,
        grid=(n // tn, m // tm),
        in_specs=[pl.BlockSpec((tm, k), lambda j, i: (i, 0)),
                  pl.BlockSpec((k, tn), lambda j, i: (0, j))],
        out_specs=pl.BlockSpec((tm, tn), lambda j, i: (i, j)),
        out_shape=jax.ShapeDtypeStruct((m, n), F32),
        compiler_params=pltpu.CompilerParams(vmem_limit_bytes=VMEM_LIMIT_BYTES),
        name=name,
    )(x, w)


def _dense_res_ln_kernel(y_ref, w_ref, x_ref, g_ref, b_ref, o_ref):
    acc = jnp.dot(y_ref[...].astype(BF16), w_ref[...], preferred_element_type=F32)
    o_ref[...] = _ln_rows(ALPHA * x_ref[...] + acc, g_ref[...], b_ref[...])


def dense_res_ln(y, w, x, g, b, tm, name):
    m, k = y.shape
    d = w.shape[1]
    return pl.pallas_call(
        _dense_res_ln_kernel,
        grid=(m // tm,),
        in_specs=[pl.BlockSpec((tm, k), lambda i: (i, 0)),
                  pl.BlockSpec((k, d), lambda i: (0, 0)),
                  pl.BlockSpec((tm, d), lambda i: (i, 0)),
                  pl.BlockSpec((1, d), lambda i: (0, 0)),
                  pl.BlockSpec((1, d), lambda i: (0, 0))],
        out_specs=pl.BlockSpec((tm, d), lambda i: (i, 0)),
        out_shape=jax.ShapeDtypeStruct((m, d), F32),
        compiler_params=pltpu.CompilerParams(vmem_limit_bytes=VMEM_LIMIT_BYTES),
        name=name,
    )(y, w, x, g.reshape(1, d), b.reshape(1, d))


SUBLANES = 8


def _conv_rows_carry(x, tail, w_ref, ksize):
    row8 = lax.broadcasted_iota(jnp.int32, (SUBLANES, x.shape[1]), 0)
    y = x * w_ref[ksize - 1:ksize, :]
    for k in range(1, ksize):
        rolled = pltpu.roll(x, k, axis=0)
        first = jnp.where(row8 < k, pltpu.roll(tail, k, axis=0), rolled[0:SUBLANES])
        y = y + jnp.concatenate([first, rolled[SUBLANES:]], axis=0) * w_ref[ksize - 1 - k:ksize - k, :]
    return y


def _conv_rows_blocked(x, buf, w_ref, ksize, step):
    r = x.shape[0]
    y = x * w_ref[ksize - 1:ksize, :]
    for k in range(1, ksize):
        prev = jnp.concatenate([buf[(ksize - 1 - k) * step:(ksize - 1) * step], x[0:r - k * step]], axis=0)
        y = y + prev * w_ref[ksize - 1 - k:ksize - k, :]
    return y


def _conv_tile(x, buf_ref, st_ref, tail_s, w_ref, ksize, mode):
    kind, param = mode
    if kind == "carry":
        tm = x.shape[0]

        @pl.when(pl.program_id(0) % param == 0)
        def _sequence_start():
            tail_s[...] = buf_ref[0]

        y = _conv_rows_carry(x, tail_s[...], w_ref, ksize)
        tail_s[...] = x[tm - SUBLANES:tm]
        st_ref[0] = x[tm - SUBLANES:tm]
        return y
    y = _conv_rows_blocked(x, buf_ref[0], w_ref, ksize, param)
    st_ref[0] = x[x.shape[0] - (ksize - 1) * param:]
    return y


def _conv_specs(mode, ksize, c, tm):
    kind, param = mode
    if kind == "carry":
        rows = SUBLANES
        idx = lambda i: (i // param, 0, 0)
    else:
        rows = (ksize - 1) * param
        idx = lambda i: (0, 0, 0)
    return pl.BlockSpec((1, rows, c), idx), rows


def _mix_out_ln_kernel(ab_ref, ac_ref, ah_ref, yb_ref, yc_ref, x_ref, w_ref, cw_ref, g_ref, b_ref, buf_ref,
                       o_ref, st_ref, tail_s, *, mode):
    u = ac_ref[...] * ah_ref[...]
    z = _conv_tile(u, buf_ref, st_ref, tail_s, cw_ref, A_CONV, mode)
    y = jnp.concatenate([ab_ref[...] * z, yb_ref[...], yc_ref[...]], axis=1).astype(BF16)
    acc = jnp.dot(y, w_ref[...], preferred_element_type=F32)
    o_ref[...] = _ln_rows(ALPHA * x_ref[...] + acc, g_ref[...], b_ref[...])


def mix_out_ln(proj, y_b, y_c, x, w_out, conv_w, g, b, buf, tm, mode):
    m = x.shape[0]
    buf_spec, st_rows = _conv_specs(mode, A_CONV, A_WIDTH, tm)
    n_st = buf.shape[0]
    row = lambda c, w: pl.BlockSpec((tm, w), lambda i: (i, c // w))
    const = lambda shape: pl.BlockSpec(shape, lambda i: (0,) * len(shape))
    return pl.pallas_call(
        functools.partial(_mix_out_ln_kernel, mode=mode),
        grid=(m // tm,),
        in_specs=[row(COL_AB, A_WIDTH), row(COL_AC, A_WIDTH), row(COL_AH, A_WIDTH),
                  row(0, GDN_WIDTH), row(0, NSA_WIDTH), row(0, D_MODEL),
                  const((D_MODEL, D_MODEL)), const((A_CONV, A_WIDTH)), const((1, D_MODEL)), const((1, D_MODEL)),
                  buf_spec],
        out_specs=[row(0, D_MODEL), pl.BlockSpec((1, st_rows, A_WIDTH), buf_spec.index_map)],
        out_shape=[jax.ShapeDtypeStruct((m, D_MODEL), F32), jax.ShapeDtypeStruct((n_st, st_rows, A_WIDTH), F32)],
        scratch_shapes=[pltpu.VMEM((SUBLANES, A_WIDTH), F32)],
        compiler_params=pltpu.CompilerParams(dimension_semantics=("arbitrary",), vmem_limit_bytes=VMEM_LIMIT_BYTES),
        name="mix_out_ln",
    )(proj, proj, proj, y_b, y_c, x, w_out, conv_w, g.reshape(1, -1), b.reshape(1, -1), buf)


def _ffn_up_kernel(x_ref, wg_ref, wv_ref, cw_ref, buf_ref, h_ref, st_ref, tail_s, *, mode):
    xb = x_ref[...].astype(BF16)
    gate = jnp.dot(xb, wg_ref[...], preferred_element_type=F32)
    val = jnp.dot(xb, wv_ref[...], preferred_element_type=F32)
    gate = _conv_tile(gate, buf_ref, st_ref, tail_s, cw_ref, FFN_CONV, mode)
    h_ref[...] = (jax.nn.silu(gate) * val).astype(BF16)


def ffn_up_act(x, w_up, conv_w, buf, tm, tn, mode):
    m = x.shape[0]
    kind, param = mode
    nj = D_FF // tn
    if kind == "carry":
        assert nj == 1
        st_rows, st_idx = SUBLANES, (lambda i, j: (i // param, 0, j))
    else:
        assert tm == m
        st_rows, st_idx = (FFN_CONV - 1) * param, (lambda i, j: (0, 0, j))
    n_st = buf.shape[0]
    once = pl.Buffered(1) if nj == 1 else None
    return pl.pallas_call(
        functools.partial(_ffn_up_kernel, mode=mode),
        grid=(m // tm, nj),
        in_specs=[pl.BlockSpec((tm, D_MODEL), lambda i, j: (i, 0)),
                  pl.BlockSpec((D_MODEL, tn), lambda i, j: (0, j), pipeline_mode=once),
                  pl.BlockSpec((D_MODEL, tn), lambda i, j: (0, nj + j), pipeline_mode=once),
                  pl.BlockSpec((FFN_CONV, tn), lambda i, j: (0, j)),
                  pl.BlockSpec((1, st_rows, tn), st_idx)],
        out_specs=[pl.BlockSpec((tm, tn), lambda i, j: (i, j)),
                   pl.BlockSpec((1, st_rows, tn), st_idx)],
        out_shape=[jax.ShapeDtypeStruct((m, D_FF), BF16), jax.ShapeDtypeStruct((n_st, st_rows, D_FF), F32)],
        scratch_shapes=[pltpu.VMEM((SUBLANES, tn), F32)],
        compiler_params=pltpu.CompilerParams(dimension_semantics=("arbitrary", "arbitrary"),
                                             vmem_limit_bytes=VMEM_LIMIT_BYTES),
        name="ffn_up_act",
    )(x, w_up, w_up, conv_w, buf)


def _conv_silu_kernel(x_ref, cw_ref, buf_ref, o_ref, st_ref, tail_s, *, mode):
    o_ref[...] = jax.nn.silu(_conv_tile(x_ref[...], buf_ref, st_ref, tail_s, cw_ref, GDN_CONV, mode))


def gdn_conv_silu(proj, conv_w, buf, tm, mode):
    m = proj.shape[0]
    buf_spec, st_rows = _conv_specs(mode, GDN_CONV, GDN_CONV_CH, tm)
    n_st = buf.shape[0]
    return pl.pallas_call(
        functools.partial(_conv_silu_kernel, mode=mode),
        grid=(m // tm,),
        in_specs=[pl.BlockSpec((tm, GDN_CONV_CH), lambda i: (i, COL_GQKV // GDN_CONV_CH)),
                  pl.BlockSpec((GDN_CONV, GDN_CONV_CH), lambda i: (0, 0)),
                  buf_spec],
        out_specs=[pl.BlockSpec((tm, GDN_CONV_CH), lambda i: (i, 0)),
                   pl.BlockSpec((1, st_rows, GDN_CONV_CH), buf_spec.index_map)],
        out_shape=[jax.ShapeDtypeStruct((m, GDN_CONV_CH), F32),
                   jax.ShapeDtypeStruct((n_st, st_rows, GDN_CONV_CH), F32)],
        scratch_shapes=[pltpu.VMEM((SUBLANES, GDN_CONV_CH), F32)],
        compiler_params=pltpu.CompilerParams(dimension_semantics=("arbitrary",), vmem_limit_bytes=VMEM_LIMIT_BYTES),
        name="gdn_conv_silu",
    )(proj, conv_w, buf)


GDN_ROWS = 128


def _mm(a, b):
    return jnp.dot(a, b, preferred_element_type=F32)


def _split3(x):
    hi = x.astype(BF16)
    r1 = x - hi.astype(F32)
    mid = r1.astype(BF16)
    return hi, mid, (r1 - mid.astype(F32)).astype(BF16)


def _mm_exact_lhs(c, x):
    hi, mid, lo = _split3(x)
    return _mm(c, hi) + _mm(c, mid) + _mm(c, lo)


def _mm_exact_rhs(x, c):
    hi, mid, lo = _split3(x)
    return _mm(hi, c) + _mm(mid, c) + _mm(lo, c)


def _mm3(a, b):
    ah = a.astype(BF16)
    al = (a - ah.astype(F32)).astype(BF16)
    bh = b.astype(BF16)
    bl = (b - bh.astype(F32)).astype(BF16)
    return _mm(ah, bh) + _mm(ah, bl) + _mm(al, bh)


def _gdn_kernel(qkv_ref, gate_ref, sm_ref, alog_ref, dt_ref, gain_ref, ea_ref, eb_ref, tril_ref, cones_ref,
                hones_ref, s0_ref, y_ref, sout_ref, s_s, o_s, *, chunk, carry_state, tiles_per_seq):
    rws = GDN_ROWS
    nchunk = rws // chunk
    shift = chunk.bit_length() - 1
    nt = (((1,), (1,)), ((), ()))
    tn = (((0,), (0,)), ((), ()))

    if carry_state:
        @pl.when(pl.program_id(0) % tiles_per_seq == 0)
        def _sequence_start():
            s_s[...] = s0_ref[0]

    hones = hones_ref[...]

    def head_sum(x):
        xh = x.astype(BF16)
        return _mm(xh, hones) + _mm((x - xh.astype(F32)).astype(BF16), hones)

    qkv = qkv_ref[...]
    q, k, v = qkv[:, 0:GDN_QK], qkv[:, GDN_QK:2 * GDN_QK], qkv[:, 2 * GDN_QK:]
    q = q * lax.rsqrt(head_sum(q * q) + 1e-6) * (GDN_DK ** -0.5)
    k = k * lax.rsqrt(head_sum(k * k) + 1e-6)
    sm = sm_ref[...]
    g = -jnp.exp(alog_ref[...]) * jax.nn.softplus(_mm_exact_rhs(sm, ea_ref[...]) + dt_ref[...])
    beta = jax.nn.sigmoid(_mm_exact_rhs(sm, eb_ref[...]))
    gc = _mm_exact_lhs(tril_ref[...], g)
    gcl = _mm_exact_lhs(cones_ref[...], g)

    row = lax.broadcasted_iota(jnp.int32, (rws, rws), 0)
    col = lax.broadcasted_iota(jnp.int32, (rws, rws), 1)
    same = lax.shift_right_logical(row, shift) == lax.shift_right_logical(col, shift)
    eye = jnp.where(row == col, 1.0, 0.0)

    heads = range(GDN_HEADS)
    hsl = [slice(GDN_DK * h, GDN_DK * (h + 1)) for h in heads]
    incl = same & (row >= col)
    strict = same & (row > col)
    xs = [gc[:, 128 * h:128 * (h + 1)] for h in heads]
    gcols = [x[:, 0:GDN_DK] for x in xs]
    gends = [gcl[:, 128 * h:128 * h + GDN_DK] for h in heads]
    betas = [beta[:, 128 * h:128 * h + GDN_DK] for h in heads]
    decays = [jnp.where(incl, jnp.exp(jnp.where(incl, x - x.T, 0.0)), 0.0) for x in xs]
    kbs = [k[:, hsl[h]] * betas[h] for h in heads]
    khbs = [k[:, hsl[h]].astype(BF16) for h in heads]
    a_s = [jnp.where(strict, lax.dot_general(kbs[h].astype(BF16), khbs[h], nt, preferred_element_type=F32) * decays[h], 0.0)
           for h in heads]
    minvs = [eye - a for a in a_s]
    apows = [_mm3(a, a) for a in a_s]
    for step in range(shift - 1):
        minvs = [_mm3(minvs[h], eye + apows[h]) for h in heads]
        if step < shift - 2:
            apows = [_mm3(ap, ap) for ap in apows]
    egcs = [jnp.exp(gc_h) for gc_h in gcols]
    us = [_mm3(minvs[h], v[:, hsl[h]] * betas[h]) for h in heads]
    wbs = [_mm3(minvs[h], kbs[h] * egcs[h]).astype(BF16) for h in heads]
    a_qks = [(lax.dot_general(q[:, hsl[h]].astype(BF16), khbs[h], nt, preferred_element_type=F32) * decays[h]).astype(BF16)
             for h in heads]
    q_decs = [(q[:, hsl[h]] * egcs[h]).astype(BF16) for h in heads]
    k_decs = [(k[:, hsl[h]] * jnp.exp(gends[h] - gcols[h])).astype(BF16) for h in heads]
    g_lasts = [jnp.exp(g_h) for g_h in gends]
    states = [s_s[h] for h in heads] if carry_state else None
    v_new = [[] for _ in heads]
    o_state = [[] for _ in heads]
    for n in range(nchunk):
        r = slice(n * chunk, (n + 1) * chunk)
        olds = states if carry_state else [s0_ref[n, h] for h in heads]
        sbs = [s_old.astype(BF16) for s_old in olds]
        vns = [us[h][r] - _mm(wbs[h][r], sbs[h]) for h in heads]
        for h in heads:
            o_state[h].append(_mm(q_decs[h][r], sbs[h]))
            v_new[h].append(vns[h])
        news = [olds[h] * g_lasts[h][n * chunk:n * chunk + 1, :]
                + lax.dot_general(k_decs[h][r], vns[h].astype(BF16), tn, preferred_element_type=F32) for h in heads]
        if carry_state:
            states = news
        else:
            for h in heads:
                sout_ref[n, h] = news[h]
    for h in heads:
        v_all = jnp.concatenate(v_new[h], axis=0) if nchunk > 1 else v_new[h][0]
        o_all = jnp.concatenate(o_state[h], axis=0) if nchunk > 1 else o_state[h][0]
        o_s[:, hsl[h]] = o_all + _mm(a_qks[h], v_all.astype(BF16))
        if carry_state:
            s_s[h] = states[h]

    if carry_state:
        sout_ref[0] = s_s[...]
    o = o_s[...]
    o = o * lax.rsqrt(head_sum(o * o) * (1.0 / GDN_DV) + 1e-6)
    y_ref[...] = o * gain_ref[...] * jax.nn.silu(gate_ref[...])


def gated_delta(qkv_act, proj, s0, a_log, dt_bias, norm_g, chunk, carry_state, tiles_per_seq):
    m = qkv_act.shape[0]
    rws = GDN_ROWS
    idx = np.arange(rws)
    same = (idx[:, None] // chunk) == (idx[None, :] // chunk)
    tril = jnp.asarray(same & (idx[:, None] >= idx[None, :]), BF16)
    cones = jnp.asarray(same, BF16)
    lane = np.arange(GDN_WIDTH)
    hones = jnp.asarray(lane[:, None] // GDN_DK == lane[None, :] // GDN_DK, BF16)
    lane2 = np.arange(4 * 128) // 128
    smr = np.arange(128)
    ea = jnp.asarray(smr[:, None] == SM_GA + lane2[None, :], BF16)
    eb = jnp.asarray(smr[:, None] == SM_GB + lane2[None, :], BF16)
    alog_x = jnp.repeat(a_log.astype(F32), 128).reshape(1, 512)
    dt_x = jnp.repeat(dt_bias.astype(F32), 128).reshape(1, 512)
    gain_x = jnp.tile(norm_g.astype(F32), GDN_HEADS).reshape(1, GDN_WIDTH)
    if carry_state:
        s_blk, s_idx = (1, GDN_HEADS, GDN_DK, GDN_DV), (lambda i: (i // tiles_per_seq, 0, 0, 0))
    else:
        s_blk, s_idx = (rws // chunk, GDN_HEADS, GDN_DK, GDN_DV), (lambda i: (i, 0, 0, 0))
    const = lambda shape: pl.BlockSpec(shape, lambda i: (0,) * len(shape))
    return pl.pallas_call(
        functools.partial(_gdn_kernel, chunk=chunk, carry_state=carry_state, tiles_per_seq=tiles_per_seq),
        grid=(m // rws,),
        in_specs=[pl.BlockSpec((rws, GDN_CONV_CH), lambda i: (i, 0)),
                  pl.BlockSpec((rws, GDN_WIDTH), lambda i: (i, COL_GGATE // GDN_WIDTH)),
                  pl.BlockSpec((rws, 128), lambda i: (i, COL_SMALL // 128)),
                  const((1, 512)), const((1, 512)), const((1, GDN_WIDTH)),
                  const((128, 512)), const((128, 512)), const((rws, rws)), const((rws, rws)),
                  const((GDN_WIDTH, GDN_WIDTH)),
                  pl.BlockSpec(s_blk, s_idx)],
        out_specs=[pl.BlockSpec((rws, GDN_WIDTH), lambda i: (i, 0)), pl.BlockSpec(s_blk, s_idx)],
        out_shape=[jax.ShapeDtypeStruct((m, GDN_WIDTH), F32), jax.ShapeDtypeStruct(s0.shape, F32)],
        scratch_shapes=[pltpu.VMEM((GDN_HEADS, GDN_DK, GDN_DV), F32), pltpu.VMEM((rws, GDN_WIDTH), F32)],
        compiler_params=pltpu.CompilerParams(dimension_semantics=("arbitrary",), vmem_limit_bytes=VMEM_LIMIT_BYTES),
        name="gated_delta",
    )(qkv_act, proj, proj, alog_x, dt_x, gain_x, ea, eb, tril, cones, hones, s0)


COL_Q = 0
COL_KV = 512
COL_AB = 1024
COL_AC = 1280
COL_GQKV = 1536
COL_AH = 2304
COL_GGATE = 2560
COL_WIN = 2816
COL_SMALL = 3072
N_PROJ = 3200
SM_GA, SM_GB, SM_NG = 0, 4, 8


def permute_w_in(w_in):
    offs = np.concatenate([[0], np.cumsum(IN_SIZES)])
    grp = lambda k: w_in[..., offs[k]:offs[k + 1]]
    pad = jnp.zeros(w_in.shape[:-1] + (N_PROJ - COL_SMALL - 32,), w_in.dtype)
    order = [grp(9), grp(10), grp(0), grp(1), grp(3), grp(4), grp(5), grp(2), grp(6), grp(11),
             grp(7), grp(8), grp(12), pad]
    return jnp.concatenate(order, axis=-1)


NEG = -1e30
KEY_CHUNK = 128
SEL_CHUNK = 256


def _block_diag4(a, b):
    z = jnp.zeros_like(a)
    rows = [jnp.concatenate([a, z, z, z], -1), jnp.concatenate([z, a, z, z], -1),
            jnp.concatenate([z, z, b, z], -1), jnp.concatenate([z, z, z, b], -1)]
    return jnp.concatenate(rows, -2)


def compress_params(cmp_pe, cmp_w1, cmp_b1, cmp_w2):
    w1 = _block_diag4(cmp_w1[0], cmp_w1[1]).astype(BF16)
    w2 = _block_diag4(cmp_w2[0], cmp_w2[1]).astype(BF16)
    pe = jnp.concatenate([cmp_pe[0], cmp_pe[0], cmp_pe[1], cmp_pe[1]], -1)
    b1 = jnp.concatenate([cmp_b1[0], cmp_b1[0], cmp_b1[1], cmp_b1[1]], -1).reshape(1, 256)
    return w1, pe, b1, w2


def _compress_rows(row_loader, ncp, w1_ref, pe_ref, b1_ref, w2_ref):
    acc_lo = jnp.zeros((ncp, 256), F32)
    acc_hi = jnp.zeros((ncp, 256), F32)
    for l in range(CMP_STRIDE):
        x = row_loader(l)
        acc_lo += jnp.dot((x + pe_ref[l:l + 1, :]).astype(BF16), w1_ref[l], preferred_element_type=F32)
        acc_hi += jnp.dot((x + pe_ref[l + CMP_STRIDE:l + CMP_STRIDE + 1, :]).astype(BF16),
                          w1_ref[l + CMP_STRIDE], preferred_element_type=F32)
    hid = jax.nn.gelu(acc_lo + pltpu.roll(acc_hi, ncp - 1, axis=0) + b1_ref[...])
    return jnp.dot(hid.astype(BF16), w2_ref[...], preferred_element_type=F32)


def _cmp_attend(q_rows, kc, vc, pos, ovl, n_heads, tq, ncp):
    s = lax.dot_general(q_rows, kc, (((1,), (1,)), ((), ())), preferred_element_type=F32)
    cend = lax.broadcasted_iota(jnp.int32, (tq, ncp), 1) * CMP_STRIDE + (CMP_BLOCK - 1)
    s3 = jnp.where((cend <= pos)[None], s.reshape(n_heads, tq, ncp), -jnp.inf)
    m = jnp.max(s3, axis=-1, keepdims=True)
    m = jnp.where(m == -jnp.inf, 0.0, m)
    p = jnp.exp(s3 - m)
    pn = p / jnp.maximum(jnp.sum(p, axis=-1, keepdims=True), 1e-30)
    o_cmp = jnp.dot(pn.reshape(n_heads * tq, ncp).astype(BF16), vc, preferred_element_type=F32)
    psum = pn[0]
    for g in range(1, n_heads):
        psum = psum + pn[g]
    p_hi = psum.astype(BF16)
    p_lo = (psum - p_hi.astype(F32)).astype(BF16)
    imp = (jnp.dot(p_hi, ovl, preferred_element_type=F32) + jnp.dot(p_lo, ovl, preferred_element_type=F32))
    return o_cmp, imp


def _select_topn(imp, pos):
    r, ns = imp.shape
    blk = lax.broadcasted_iota(jnp.int32, (r, ns), 1)
    blk_f = blk.astype(F32)
    forced = (blk == 0) | (blk == lax.shift_right_logical(pos, 6))
    valid = blk * SEL_BLOCK <= pos
    v = jnp.where(forced, jnp.inf, jnp.where(valid, imp, -jnp.inf))
    sel = jnp.zeros((r, ns), F32)
    for _ in range(min(SEL_TOPN, ns)):
        mx = jnp.max(v, axis=-1, keepdims=True)
        idx = jnp.min(jnp.where(v == mx, blk_f, float(ns)), axis=-1, keepdims=True)
        hit = blk_f == idx
        sel = jnp.where(hit, 1.0, sel)
        v = jnp.where(hit, -jnp.inf, v)
    return sel


def _nsa_prompt_kernel(q_ref, cmp_ref, slc_ref, win_ref, sm_ref, w1_ref, pe_ref, b1_ref, w2_ref, ovl_ref, exp_ref,
                       y_ref, kc_s, vc_s, ks_s, vs_s, kw_s, vw_s, q_s, sel_s, ocmp_s, owin_s, m_s, acc_s, *, t_len):
    tq = KEY_CHUNK
    i = pl.program_id(1)
    ncp = t_len // CMP_STRIDE
    ns = t_len // SEL_BLOCK
    ng = NSA_GROUP
    rows = ng * tq

    @pl.when(i == 0)
    def _prepare_sequence():
        kvc = _compress_rows(lambda l: cmp_ref[0, l], ncp, w1_ref, pe_ref, b1_ref, w2_ref)
        kc_s[...] = kvc[:, 0:128].astype(BF16)
        vc_s[...] = kvc[:, 128:256].astype(BF16)
        ones = jnp.ones((512, 64), BF16)

        def cast_rows(r, carry):
            sl = pl.ds(pl.multiple_of(r * 512, 512), 512)
            for h in range(NSA_KV_HEADS):
                ks_s[h, sl, :] = slc_ref[sl, 64 * h:64 * h + 64].astype(BF16)
                vs_s[h, sl, 0:64] = slc_ref[sl, 128 + 64 * h:192 + 64 * h].astype(BF16)
                vs_s[h, sl, 64:128] = ones
                kw_s[h, sl, :] = win_ref[sl, 64 * h:64 * h + 64].astype(BF16)
                vw_s[h, sl, 0:64] = win_ref[sl, 128 + 64 * h:192 + 64 * h].astype(BF16)
                vw_s[h, sl, 64:128] = ones
            return carry

        lax.fori_loop(0, t_len // 512, cast_rows, 0)

    pos = i * tq + lax.broadcasted_iota(jnp.int32, (tq, 1), 0)
    gates = jax.nn.sigmoid(sm_ref[:, SM_NG:SM_NG + 3 * NSA_HEADS])
    nt = (((1,), (1,)), ((), ()))

    wk = min(WINDOW + KEY_CHUNK, t_len)
    w0 = pl.multiple_of(jnp.minimum(jnp.maximum(i - WINDOW // KEY_CHUNK, 0), (t_len - wk) // KEY_CHUNK) * KEY_CHUNK,
                        KEY_CHUNK)
    wpos = w0 + lax.broadcasted_iota(jnp.int32, (tq, wk), 1)
    wbias = jnp.where(wpos <= pos, jnp.where(wpos > pos - WINDOW, 0.0, NEG), NEG)

    hs = range(NSA_KV_HEADS)
    q_rows = []
    for h in hs:
        qh = q_ref[:, 256 * h:256 * h + 256] * (NSA_HD ** -0.5)
        q_rows.append(jnp.concatenate([qh[:, 64 * g:64 * g + 64] for g in range(ng)], axis=0).astype(BF16))
        q_s[h] = q_rows[h]
    cmp = [_cmp_attend(q_rows[h], kc_s[:, 64 * h:64 * h + 64], vc_s[:, 64 * h:64 * h + 64], pos, ovl_ref[...],
                       ng, tq, ncp) for h in hs]
    sel = _select_topn(jnp.concatenate([c[1] for c in cmp], axis=0), jnp.concatenate([pos] * NSA_KV_HEADS, axis=0))
    sw = [lax.dot_general(q_rows[h], kw_s[h, pl.ds(w0, wk), :], nt, preferred_element_type=F32) for h in hs]
    sw = [(s.reshape(ng, tq, wk) + wbias[None]).reshape(rows, wk) for s in sw]
    pw = [jnp.exp(s - jnp.max(s, axis=-1, keepdims=True)).astype(BF16) for s in sw]
    aw = [jnp.dot(pw[h], vw_s[h, pl.ds(w0, wk), :], preferred_element_type=F32) for h in hs]
    for h in hs:
        ocmp_s[h] = cmp[h][0]
        sel_s[h] = sel[h * tq:(h + 1) * tq].astype(BF16)
        owin_s[h] = aw[h][:, 0:64] / aw[h][:, 64:128]
        m_s[h] = jnp.full((rows, 128), NEG, F32)
        acc_s[h] = jnp.zeros((rows, 128), F32)

    lane = lax.broadcasted_iota(jnp.int32, (tq, SEL_CHUNK), 1)

    def sel_body(j, carry):
        off = pl.multiple_of(j * SEL_CHUNK, SEL_CHUNK)
        hs = range(NSA_KV_HEADS)
        pairs = [(h, slice(g * tq, (g + 1) * tq)) for h in hs for g in range(ng)]
        ks = [ks_s[h, pl.ds(off, SEL_CHUNK), :] for h in hs]
        vs = [vs_s[h, pl.ds(off, SEL_CHUNK), :] for h in hs]
        causal = off + lane <= pos
        biases = [jnp.where(causal, (jnp.dot(sel_s[h], exp_ref[:, pl.ds(off, SEL_CHUNK)],
                                            preferred_element_type=F32) - 1.0) * (-NEG), NEG) for h in hs]
        ss = [lax.dot_general(q_s[h, r, :], ks[h], nt, preferred_element_type=F32) + biases[h] for h, r in pairs]
        m_prevs = [m_s[h, r, :] for h, r in pairs]
        m_cols = [jnp.maximum(mp[:, 0:1], jnp.max(s, axis=-1, keepdims=True)) for mp, s in zip(m_prevs, ss)]
        ps = [jnp.exp(s - mc).astype(BF16) for s, mc in zip(ss, m_cols)]
        pvs = [jnp.dot(p, vs[h], preferred_element_type=F32) for p, (h, r) in zip(ps, pairs)]
        for (h, r), mp, mc, pv in zip(pairs, m_prevs, m_cols, pvs):
            acc_s[h, r, :] = jnp.exp(mp - mc) * acc_s[h, r, :] + pv
            m_s[h, r, :] = jnp.broadcast_to(mc, (tq, 128))
        return carry

    lax.fori_loop(0, (i * tq) // SEL_CHUNK + 1, sel_body, 0)

    for h in range(NSA_KV_HEADS):
        acc = acc_s[h]
        o_slc = acc[:, 0:64] / acc[:, 64:128]
        o_cmp = ocmp_s[h]
        o_win = owin_s[h]
        for g in range(ng):
            hh = ng * h + g
            r = slice(g * tq, (g + 1) * tq)
            y_ref[:, 64 * hh:64 * hh + 64] = (gates[:, hh:hh + 1] * o_cmp[r]
                                              + gates[:, NSA_HEADS + hh:NSA_HEADS + hh + 1] * o_slc[r]
                                              + gates[:, 2 * NSA_HEADS + hh:2 * NSA_HEADS + hh + 1] * o_win[r])


def nsa_prompt(proj, n_seq, t_len, cparams):
    tq = KEY_CHUNK
    nt = t_len // tq
    ncp = t_len // CMP_STRIDE
    ns = t_len // SEL_BLOCK
    w1, pe, b1, w2 = cparams
    cstart = np.arange(ncp)[:, None] * CMP_STRIDE
    sstart = np.arange(ns)[None, :] * SEL_BLOCK
    ovl = jnp.asarray((cstart < sstart + SEL_BLOCK) & (cstart + CMP_BLOCK > sstart), BF16)
    expand = jnp.asarray(np.arange(t_len)[None, :] // SEL_BLOCK == np.arange(ns)[:, None], BF16)
    cmp_rows = proj[:, COL_KV:COL_KV + 256].reshape(n_seq, ncp, CMP_STRIDE, 256).transpose(0, 2, 1, 3)
    once = pl.Buffered(1)
    const = lambda shape: pl.BlockSpec(shape, lambda b, i: (0,) * len(shape), pipeline_mode=once)
    return pl.pallas_call(
        functools.partial(_nsa_prompt_kernel, t_len=t_len),
        grid=(n_seq, nt),
        in_specs=[pl.BlockSpec((tq, 512), lambda b, i: (b * nt + i, COL_Q // 512)),
                  pl.BlockSpec((1, CMP_STRIDE, ncp, 256), lambda b, i: (b, 0, 0, 0), pipeline_mode=once),
                  pl.BlockSpec((t_len, 256), lambda b, i: (b, (COL_KV + 256) // 256), pipeline_mode=once),
                  pl.BlockSpec((t_len, 256), lambda b, i: (b, COL_WIN // 256), pipeline_mode=once),
                  pl.BlockSpec((tq, 128), lambda b, i: (b * nt + i, COL_SMALL // 128)),
                  const((CMP_BLOCK, 256, 256)), const((CMP_BLOCK, 256)), const((1, 256)), const((256, 256)),
                  const((ncp, ns)), const((ns, t_len))],
        out_specs=pl.BlockSpec((tq, NSA_WIDTH), lambda b, i: (b * nt + i, 0)),
        out_shape=jax.ShapeDtypeStruct((n_seq * t_len, NSA_WIDTH), F32),
        scratch_shapes=[pltpu.VMEM((ncp, 128), BF16), pltpu.VMEM((ncp, 128), BF16),
                        pltpu.VMEM((NSA_KV_HEADS, t_len, 64), BF16), pltpu.VMEM((NSA_KV_HEADS, t_len, 128), BF16),
                        pltpu.VMEM((NSA_KV_HEADS, t_len, 64), BF16), pltpu.VMEM((NSA_KV_HEADS, t_len, 128), BF16),
                        pltpu.VMEM((NSA_KV_HEADS, NSA_GROUP * tq, 64), BF16), pltpu.VMEM((NSA_KV_HEADS, tq, ns), BF16),
                        pltpu.VMEM((NSA_KV_HEADS, NSA_GROUP * tq, 64), F32),
                        pltpu.VMEM((NSA_KV_HEADS, NSA_GROUP * tq, 64), F32),
                        pltpu.VMEM((NSA_KV_HEADS, NSA_GROUP * tq, 128), F32),
                        pltpu.VMEM((NSA_KV_HEADS, NSA_GROUP * tq, 128), F32)],
        compiler_params=pltpu.CompilerParams(dimension_semantics=("arbitrary", "arbitrary"),
                                             vmem_limit_bytes=VMEM_LIMIT_BYTES),
        name="nsa_prompt",
    )(proj, cmp_rows, proj, proj, proj, w1, pe, b1, w2, ovl, expand)


PAGE_ROWS = 128
SEL_LANES = 128


def _softmax_segments(segs):
    m = segs[0].max(axis=-1, keepdims=True)
    for s in segs[1:]:
        m = jnp.maximum(m, s.max(axis=-1, keepdims=True))
    ps = [jnp.exp(s - m) for s in segs]
    den = ps[0].sum(axis=-1, keepdims=True)
    for p in ps[1:]:
        den = den + p.sum(axis=-1, keepdims=True)
    return ps, den


def _nsa_sample_kernel(pt_ref, *refs, n_pages, past_len, dec_t):
    del pt_ref
    pages = refs[:n_pages]
    (q_ref, kvn_ref, wn_ref, sm_ref, wc_ref, w1_ref, pe_ref, b1_ref, w2_ref, ovl_ref, exp_ref,
     y_ref, wst_ref, rk_s, rv_s, newpg_s, neww_s) = refs[n_pages:]
    ncp = past_len // CMP_STRIDE
    ng, nh = NSA_GROUP, NSA_KV_HEADS
    rows_h = ng * dec_t
    rows = nh * rows_h
    nt = (((1,), (1,)), ((), ()))

    for p in range(n_pages):
        rk_s[p * PAGE_ROWS:(p + 1) * PAGE_ROWS, :] = pages[p][:, 0:128]
        rv_s[p * PAGE_ROWS:(p + 1) * PAGE_ROWS, :] = pages[p][:, 128:256]
    kvc = _compress_rows(
        lambda l: jnp.concatenate([rk_s[pl.ds(l, ncp, stride=CMP_STRIDE), :],
                                   rv_s[pl.ds(l, ncp, stride=CMP_STRIDE), :]], axis=1),
        ncp, w1_ref, pe_ref, b1_ref, w2_ref)
    kc = kvc[:, 0:128].astype(BF16)
    vc = kvc[:, 128:256].astype(BF16)

    newpg_s[...] = jnp.zeros((PAGE_ROWS, 256), F32)
    newpg_s[0:dec_t, :] = kvn_ref[0][:, 256:512]
    neww_s[...] = jnp.zeros((PAGE_ROWS, 256), F32)
    neww_s[0:dec_t, :] = wn_ref[0]

    qf = q_ref[0] * (NSA_HD ** -0.5)
    zero = jnp.zeros((dec_t, 64), F32)
    qrows = []
    for h in range(nh):
        for g in range(ng):
            piece = qf[:, 64 * (ng * h + g):64 * (ng * h + g) + 64]
            qrows.append(jnp.concatenate([piece, zero] if h == 0 else [zero, piece], axis=1))
    q_bd = jnp.concatenate(qrows, axis=0).astype(BF16)
    t_row = lax.broadcasted_iota(jnp.int32, (rows, 1), 0) & (dec_t - 1)
    pos = past_len + t_row
    head0 = lax.broadcasted_iota(jnp.int32, (rows, 128), 0) < rows_h
    lane_lo = lax.broadcasted_iota(jnp.int32, (rows, 128), 1) < 64
    own = head0 == lane_lo

    def own_half(x):
        x = jnp.where(own, x, 0.0)
        return x[:, 0:64] + x[:, 64:128]

    s = lax.dot_general(q_bd, kc, nt, preferred_element_type=F32)
    cend = lax.broadcasted_iota(jnp.int32, (rows, ncp), 1) * CMP_STRIDE + (CMP_BLOCK - 1)
    s = jnp.where(cend <= pos, s, -jnp.inf)
    m = jnp.max(s, axis=-1, keepdims=True)
    m = jnp.where(m == -jnp.inf, 0.0, m)
    p = jnp.exp(s - m)
    pn = p / jnp.maximum(jnp.sum(p, axis=-1, keepdims=True), 1e-30)
    o_cmp = own_half(_mm(pn.astype(BF16), vc))

    ovl = ovl_ref[...]
    blk = lax.broadcasted_iota(jnp.int32, (dec_t, SEL_LANES), 1)
    blk_f = blk.astype(F32)
    pos_t = past_len + lax.broadcasted_iota(jnp.int32, (dec_t, 1), 0)
    sel_rows = []
    for h in range(nh):
        psum = pn[h * rows_h:h * rows_h + dec_t]
        for g in range(1, ng):
            psum = psum + pn[h * rows_h + g * dec_t:h * rows_h + (g + 1) * dec_t]
        p_hi = psum.astype(BF16)
        imp = _mm(p_hi, ovl) + _mm((psum - p_hi.astype(F32)).astype(BF16), ovl)
        forced = (blk == 0) | (blk == lax.shift_right_logical(pos_t, 6))
        v = jnp.where(forced, jnp.inf, jnp.where(blk * SEL_BLOCK <= pos_t, imp, -jnp.inf))
        sel = jnp.zeros((dec_t, SEL_LANES), F32)
        for _ in range(SEL_TOPN):
            mx = jnp.max(v, axis=-1, keepdims=True)
            idx = jnp.min(jnp.where(v == mx, blk_f, float(SEL_LANES)), axis=-1, keepdims=True)
            hit = blk_f == idx
            sel = jnp.where(hit, 1.0, sel)
            v = jnp.where(hit, -jnp.inf, v)
        sel_rows += [sel] * ng
    sel_b = jnp.concatenate(sel_rows, axis=0).astype(BF16)

    lane = lax.broadcasted_iota(jnp.int32, (rows, PAGE_ROWS), 1)
    segs = []
    for pg in range(n_pages + 1):
        kpage = (pages[pg][:, 256:384] if pg < n_pages else newpg_s[:, 0:128]).astype(BF16)
        sc = lax.dot_general(q_bd, kpage, nt, preferred_element_type=F32)
        chosen = _mm(sel_b, exp_ref[:, pg * PAGE_ROWS:(pg + 1) * PAGE_ROWS])
        ok = chosen > 0.5
        if pg == n_pages:
            ok = ok & (past_len + lane <= pos)
        segs.append(jnp.where(ok, sc, NEG))
    ps, den = _softmax_segments(segs)
    acc = jnp.zeros((rows, 128), F32)
    for pg in range(n_pages + 1):
        vpage = (pages[pg][:, 384:512] if pg < n_pages else newpg_s[:, 128:256]).astype(BF16)
        acc = acc + _mm(ps[pg].astype(BF16), vpage)
    o_slc = own_half(acc) / den

    wlen = wc_ref.shape[1]
    wc = wc_ref[0]
    jw = lax.broadcasted_iota(jnp.int32, (rows, wlen), 1)
    s_old = lax.dot_general(q_bd, wc[:, 0:128].astype(BF16), nt, preferred_element_type=F32)
    s_old = jnp.where(past_len - wlen + jw > pos - WINDOW, s_old, NEG)
    s_new = lax.dot_general(q_bd, neww_s[:, 0:128].astype(BF16), nt, preferred_element_type=F32)
    s_new = jnp.where(past_len + lane <= pos, s_new, NEG)
    ps, den = _softmax_segments([s_old, s_new])
    acc = _mm(ps[0].astype(BF16), wc[:, 128:256].astype(BF16)) + _mm(ps[1].astype(BF16), neww_s[:, 128:256].astype(BF16))
    o_win = own_half(acc) / den

    gates = jax.nn.sigmoid(sm_ref[0][:, SM_NG:SM_NG + 3 * NSA_HEADS])
    for hh in range(NSA_HEADS):
        r = slice(hh * dec_t, (hh + 1) * dec_t)
        y_ref[0, :, 64 * hh:64 * hh + 64] = (gates[:, hh:hh + 1] * o_cmp[r]
                                             + gates[:, NSA_HEADS + hh:NSA_HEADS + hh + 1] * o_slc[r]
                                             + gates[:, 2 * NSA_HEADS + hh:2 * NSA_HEADS + hh + 1] * o_win[r])

    wst_ref[0, 0:wlen - dec_t, :] = wc[dec_t:wlen, :]
    wst_ref[0, wlen - dec_t:wlen, :] = wn_ref[0]


def nsa_sample(q, kv_new, win_new, sm, cache_kv, layer_idx, page_table, win_cache, cparams):
    bsz, dec_t, _ = q.shape
    n_pages = page_table.shape[1]
    past_len = n_pages * PAGE_ROWS
    wlen = win_cache.shape[1]
    assert dec_t < CMP_STRIDE and dec_t % SUBLANES == 0 and dec_t & (dec_t - 1) == 0
    assert wlen == WINDOW and wlen <= past_len
    ncp = past_len // CMP_STRIDE
    ns = -(-(past_len + dec_t) // SEL_BLOCK)
    assert ns <= SEL_LANES
    w1, pe, b1, w2 = cparams
    cstart = np.arange(ncp)[:, None] * CMP_STRIDE
    sstart = np.arange(SEL_LANES)[None, :] * SEL_BLOCK
    ovl = jnp.asarray((cstart < sstart + SEL_BLOCK) & (cstart + CMP_BLOCK > sstart) & (np.arange(SEL_LANES)[None, :] < ns), BF16)
    keys = np.arange((n_pages + 1) * PAGE_ROWS)
    expand = jnp.asarray(keys[None, :] // SEL_BLOCK == np.arange(SEL_LANES)[:, None], BF16)
    const = lambda shape: pl.BlockSpec(shape, lambda b, pt: (0,) * len(shape))
    seq = lambda r, c: pl.BlockSpec((1, r, c), lambda b, pt: (b, 0, 0))
    page_specs = [pl.BlockSpec((None, None, PAGE_ROWS, 512), functools.partial(
        lambda b, pt, p: (layer_idx, pt[b, p], 0, 0), p=p)) for p in range(n_pages)]
    grid_spec = pltpu.PrefetchScalarGridSpec(
        num_scalar_prefetch=1,
        grid=(bsz,),
        in_specs=page_specs + [seq(dec_t, 512), seq(dec_t, 512), seq(dec_t, 256), seq(dec_t, 128), seq(wlen, 256),
                               const((CMP_BLOCK, 256, 256)), const((CMP_BLOCK, 256)), const((1, 256)),
                               const((256, 256)), const((ncp, SEL_LANES)),
                               const((SEL_LANES, (n_pages + 1) * PAGE_ROWS))],
        out_specs=[seq(dec_t, NSA_WIDTH), seq(wlen, 256)],
        scratch_shapes=[pltpu.VMEM((past_len, 128), F32), pltpu.VMEM((past_len, 128), F32),
                        pltpu.VMEM((PAGE_ROWS, 256), F32), pltpu.VMEM((PAGE_ROWS, 256), F32)])
    return pl.pallas_call(
        functools.partial(_nsa_sample_kernel, n_pages=n_pages, past_len=past_len, dec_t=dec_t),
        grid_spec=grid_spec,
        out_shape=[jax.ShapeDtypeStruct((bsz, dec_t, NSA_WIDTH), F32), jax.ShapeDtypeStruct((bsz, wlen, 256), F32)],
        compiler_params=pltpu.CompilerParams(dimension_semantics=("arbitrary",), vmem_limit_bytes=VMEM_LIMIT_BYTES),
        name="nsa_sample",
    )(page_table, *([cache_kv] * n_pages), q, kv_new, win_new, sm, win_cache, w1, pe, b1, w2, ovl, expand)


def l2_normalize(x):
    return x * lax.rsqrt(jnp.sum(jnp.square(x), axis=-1, keepdims=True) + 1e-6)


def gated_rms_norm(o, gain, gate):
    of = o.astype(F32)
    of = of * lax.rsqrt(jnp.mean(jnp.square(of), axis=-1, keepdims=True) + 1e-6)
    return (of * gain.astype(F32) * jax.nn.silu(gate.astype(F32))).astype(gate.dtype)


def masked_softmax(s, mask):
    s = jnp.where(mask, s.astype(F32), -jnp.inf)
    m = jnp.max(s, axis=-1, keepdims=True)
    m = jnp.where(jnp.isfinite(m), m, 0.0)
    p = jnp.exp(s - m)
    return p / jnp.maximum(jnp.sum(p, axis=-1, keepdims=True), 1e-30)


def causal_dwconv(x, buf, w):
    k = w.shape[0]
    t = x.shape[1]
    xp = jnp.concatenate([buf.astype(x.dtype), x], axis=1)
    y = sum(xp[:, i:i + t] * w[i] for i in range(k))
    return y, xp[:, t:]


def split_in(proj):
    return jnp.split(proj, np.cumsum(IN_SIZES)[:-1].tolist(), axis=-1)


def gated_delta_rule(q, k, v, a_in, b_in, s0, a_log, dt_bias):
    b, t, h, dk = q.shape
    dv = v.shape[-1]
    q = l2_normalize(q.astype(F32)) * (dk ** -0.5)
    k = l2_normalize(k.astype(F32))
    v = v.astype(F32)
    g = -jnp.exp(a_log.astype(F32)) * jax.nn.softplus(a_in.astype(F32) + dt_bias.astype(F32))
    beta = jax.nn.sigmoid(b_in.astype(F32))
    c = GDN_CHUNK if t % GDN_CHUNK == 0 else t
    n = t // c

    def to_chunks(z):
        return jnp.moveaxis(z.reshape((b, n, c) + z.shape[2:]), 3, 2)

    qc, kc, vc, gc, bc = (to_chunks(z) for z in (q, k, v, g, beta))
    gc = jnp.cumsum(gc, axis=-1)
    incl = jnp.tril(jnp.ones((c, c), dtype=bool))
    strict = jnp.tril(jnp.ones((c, c), dtype=bool), -1)
    diff = gc[..., :, None] - gc[..., None, :]
    decay = jnp.where(incl, jnp.exp(jnp.where(incl, diff, 0.0)), 0.0)
    kb = kc * bc[..., None]
    m = jnp.eye(c, dtype=F32) + jnp.where(strict, jnp.einsum('bnhik,bnhjk->bnhij', kb, kc) * decay, 0.0)
    rhs = jnp.concatenate([vc * bc[..., None], kb * jnp.exp(gc)[..., None]], axis=-1)
    sol = lax.linalg.triangular_solve(m, rhs, left_side=True, lower=True, unit_diagonal=True)
    u, w = sol[..., :dv], sol[..., dv:]
    a_qk = jnp.einsum('bnhik,bnhjk->bnhij', qc, kc) * decay
    q_dec = qc * jnp.exp(gc)[..., None]
    k_dec = kc * jnp.exp(gc[..., -1:] - gc)[..., None]
    g_last = jnp.exp(gc[..., -1])

    def step(state, xs):
        u_n, w_n, q_n, k_n, a_n, gl_n = xs
        v_new = u_n - jnp.einsum('bhik,bhkv->bhiv', w_n, state)
        o_n = jnp.einsum('bhik,bhkv->bhiv', q_n, state) + jnp.einsum('bhij,bhjv->bhiv', a_n, v_new)
        state = state * gl_n[..., None, None] + jnp.einsum('bhik,bhiv->bhkv', k_n, v_new)
        return state, o_n

    xs = tuple(jnp.moveaxis(z, 1, 0) for z in (u, w, q_dec, k_dec, a_qk, g_last))
    s_final, o = lax.scan(step, s0.astype(F32), xs)
    o = jnp.moveaxis(jnp.moveaxis(o, 0, 1), 2, 3).reshape(b, t, h, dv)
    return o, s_final


def compress_blocks(rows, pe, w1, b1, w2):
    b, t_pad, kvh, hd = rows.shape
    r = rows.reshape(b, t_pad // CMP_STRIDE, CMP_STRIDE, kvh, hd)
    blocks = jnp.concatenate([r[:, :-1], r[:, 1:]], axis=2) + pe[:, None, :]
    hid = jax.nn.gelu(jnp.einsum('bclhd,lde->bche', blocks, w1) + b1)
    return jnp.einsum('bche,ed->bchd', hid, w2)


def select_attend(qg, idx, q_pos, ks, vs):
    b, kvh = ks.shape[:2]
    tq = qg.shape[1]
    n = idx.shape[-1]
    b_ix = jnp.arange(b)[:, None, None, None]
    h_ix = jnp.arange(kvh)[None, :, None, None]
    kg = ks[b_ix, h_ix, idx].reshape(b, kvh, tq, n * SEL_BLOCK, NSA_HD)
    vg = vs[b_ix, h_ix, idx].reshape(b, kvh, tq, n * SEL_BLOCK, NSA_HD)
    k_pos = (idx[..., None] * SEL_BLOCK + jnp.arange(SEL_BLOCK)).reshape(b, kvh, 1, tq, n * SEL_BLOCK)
    s = jnp.einsum('bqhgd,bhqkd->bhgqk', qg, kg) * (NSA_HD ** -0.5)
    p = masked_softmax(s, k_pos <= q_pos[:, None])
    return jnp.einsum('bhgqk,bhqkd->bqhgd', p.astype(vg.dtype), vg)


def nsa_compressed_selected(qg, kv, q_pos, cmp_pe, cmp_w1, cmp_b1, cmp_w2):
    b, t = kv.shape[:2]
    tq = qg.shape[1]
    t_pad = -(-t // SEL_BLOCK) * SEL_BLOCK
    kv = jnp.pad(kv, ((0, 0), (0, t_pad - t), (0, 0), (0, 0), (0, 0)))
    kc = compress_blocks(kv[:, :, 0], cmp_pe[0], cmp_w1[0], cmp_b1[0], cmp_w2[0])
    vc = compress_blocks(kv[:, :, 1], cmp_pe[1], cmp_w1[1], cmp_b1[1], cmp_w2[1])
    nc = kc.shape[1]
    cmp_start = jnp.arange(nc) * CMP_STRIDE
    s = jnp.einsum('bqhgd,bchd->bhgqc', qg, kc) * (NSA_HD ** -0.5)
    p = masked_softmax(s, cmp_start[None, :] + (CMP_BLOCK - 1) <= q_pos[:, None])
    o_cmp = jnp.einsum('bhgqc,bchd->bqhgd', p.astype(vc.dtype), vc)
    ns = t_pad // SEL_BLOCK
    sel_start = jnp.arange(ns) * SEL_BLOCK
    overlap = ((cmp_start[:, None] < sel_start[None, :] + SEL_BLOCK)
               & (cmp_start[:, None] + CMP_BLOCK > sel_start[None, :])).astype(F32)
    imp = jnp.einsum('bhgqc,cn->bhqn', p, overlap)
    blk = jnp.arange(ns)[None, :]
    forced = (blk == 0) | (blk == q_pos[:, None] // SEL_BLOCK)
    valid = sel_start[None, :] <= q_pos[:, None]
    imp = jnp.where(forced, jnp.inf, jnp.where(valid, imp, -jnp.inf))
    n_top = min(SEL_TOPN, ns)
    _, idx = lax.top_k(imp, n_top)
    ks = jnp.moveaxis(kv[:, :, 2].reshape(b, ns, SEL_BLOCK, NSA_KV_HEADS, NSA_HD), 3, 1)
    vs = jnp.moveaxis(kv[:, :, 3].reshape(b, ns, SEL_BLOCK, NSA_KV_HEADS, NSA_HD), 3, 1)
    if tq % Q_BLOCK == 0:
        nb = tq // Q_BLOCK
        qb = jnp.moveaxis(qg.reshape(b, nb, Q_BLOCK, NSA_KV_HEADS, NSA_GROUP, NSA_HD), 1, 0)
        ib = jnp.moveaxis(idx.reshape(b, NSA_KV_HEADS, nb, Q_BLOCK, n_top), 2, 0)
        pb = q_pos.reshape(nb, Q_BLOCK)
        ob = lax.map(lambda a: select_attend(a[0], a[1], a[2], ks, vs), (qb, ib, pb))
        o_slc = jnp.moveaxis(ob, 0, 1).reshape(b, tq, NSA_KV_HEADS, NSA_GROUP, NSA_HD)
    else:
        o_slc = select_attend(qg, idx, q_pos, ks, vs)
    return o_cmp, o_slc


def window_banded(qg, kvw):
    b, t = kvw.shape[:2]
    nb = t // Q_BLOCK
    nw = WINDOW // Q_BLOCK
    kp = jnp.pad(kvw, ((0, 0), (WINDOW, 0), (0, 0), (0, 0), (0, 0))).reshape(b, nb + nw, Q_BLOCK, 2, NSA_KV_HEADS, NSA_HD)
    band = jnp.concatenate([kp[:, i:i + nb] for i in range(nw + 1)], axis=2)
    qb = qg.reshape(b, nb, Q_BLOCK, NSA_KV_HEADS, NSA_GROUP, NSA_HD)
    start = jnp.arange(nb)[:, None] * Q_BLOCK
    qpos = start + jnp.arange(Q_BLOCK)
    kpos = start - WINDOW + jnp.arange((nw + 1) * Q_BLOCK)
    qp, kp_ = qpos[:, :, None], kpos[:, None, :]
    mask = (kp_ <= qp) & (kp_ > qp - WINDOW) & (kp_ >= 0)
    s = jnp.einsum('bnqhgd,bnkhd->bnhgqk', qb, band[:, :, :, 0]) * (NSA_HD ** -0.5)
    p = masked_softmax(s, mask[None, :, None, None])
    o = jnp.einsum('bnhgqk,bnkhd->bnqhgd', p.astype(band.dtype), band[:, :, :, 1])
    return o.reshape(b, t, NSA_KV_HEADS, NSA_GROUP, NSA_HD)


def window_dense(qg, kvw_all, q_pos, k_pos):
    s = jnp.einsum('bqhgd,bkhd->bhgqk', qg, kvw_all[:, :, 0]) * (NSA_HD ** -0.5)
    mask = (k_pos[None, :] <= q_pos[:, None]) & (k_pos[None, :] > q_pos[:, None] - WINDOW)
    p = masked_softmax(s, mask)
    return jnp.einsum('bhgqk,bkhd->bqhgd', p.astype(kvw_all.dtype), kvw_all[:, :, 1])


def layer(x, q_pos, past, w_in, conv_a_w, gdn_conv_w, gdn_a_log, gdn_dt_bias, gdn_norm_g,
          cmp_pe, cmp_w1, cmp_b1, cmp_w2, w_out, ln1_g, ln1_b, w_up, ffn_conv_w, w_down, ln2_g, ln2_b):
    raise NotImplementedError


TM = 512


def _to_tb(x, b, t):
    return x.reshape(b, t, -1).transpose(1, 0, 2).reshape(t * b, -1)


def _to_bt(x, b, t):
    return x.reshape(t, b, -1).transpose(1, 0, 2).reshape(b * t, -1)


def _blocked_buf(state):
    b, k1, c = state.shape
    return state.transpose(1, 0, 2).reshape(1, k1 * b, c)


def _unblocked(st, b):
    return st.reshape(-1, b, st.shape[-1]).transpose(1, 0, 2)


def layer_prompt(x, n_seq, t_len, lw):
    tiles = t_len // TM
    carry = ("carry", tiles)
    zeros = lambda c: jnp.zeros((n_seq, SUBLANES, c), F32)
    proj = dense(x, lw["w_in"], TM, N_PROJ, "in_proj")
    qkv, gconv = gdn_conv_silu(proj, lw["gdn_conv_w"], zeros(GDN_CONV_CH), TM, carry)
    y_b, s_new = gated_delta(qkv, proj, jnp.zeros((n_seq, GDN_HEADS, GDN_DK, GDN_DV), F32), lw["gdn_a_log"],
                             lw["gdn_dt_bias"], lw["gdn_norm_g"], GDN_CHUNK, True, t_len // GDN_ROWS)
    y_c = nsa_prompt(proj, n_seq, t_len, lw["cmp"])
    x1, conva = mix_out_ln(proj, y_b, y_c, x, lw["w_out"], lw["conv_a_w"], lw["ln1_g"], lw["ln1_b"],
                           zeros(A_WIDTH), TM, carry)
    h, ffnc = ffn_up_act(x1, lw["w_up"], lw["ffn_conv_w"], zeros(D_FF), TM, D_FF, carry)
    x2 = dense_res_ln(h, lw["w_down"], x1, lw["ln2_g"], lw["ln2_b"], TM, "ffn_down_ln")
    p3 = proj.reshape(n_seq, t_len, N_PROJ)
    kv_new = p3[:, :, COL_KV:COL_KV + 4 * NSA_KV_DIM].reshape(n_seq, t_len, 4, NSA_KV_HEADS, NSA_HD)
    wkeep = min(WINDOW, t_len)
    win_state = p3[:, t_len - wkeep:, COL_WIN:COL_WIN + 2 * NSA_KV_DIM].reshape(n_seq, wkeep, 2, NSA_KV_HEADS, NSA_HD)
    tail = lambda st, k: st[:, SUBLANES - (k - 1):, :]
    return x2, (kv_new, win_state, tail(conva, A_CONV), tail(gconv, GDN_CONV), s_new, tail(ffnc, FFN_CONV))


def layer_sample(x, bsz, dec_t, lw, layer_idx, cache_kv, page_table, win_cache, st_conv_a, st_gdn_conv, st_gdn,
                 st_ffn_conv):
    m = dec_t * bsz
    blocked = ("blocked", bsz)
    proj = dense(x, lw["w_in"], TM, N_PROJ, "in_proj")
    proj_bt = _to_bt(proj, bsz, dec_t)
    qkv, gconv = gdn_conv_silu(proj, lw["gdn_conv_w"], _blocked_buf(st_gdn_conv), m, blocked)
    y_b, s_new = gated_delta(_to_bt(qkv, bsz, dec_t), proj_bt, st_gdn, lw["gdn_a_log"], lw["gdn_dt_bias"],
                             lw["gdn_norm_g"], dec_t, False, 1)
    p3 = proj_bt.reshape(bsz, dec_t, N_PROJ)
    kv_new = p3[:, :, COL_KV:COL_KV + 4 * NSA_KV_DIM]
    y_c, win_state = nsa_sample(p3[:, :, COL_Q:COL_Q + NSA_WIDTH], kv_new,
                                p3[:, :, COL_WIN:COL_WIN + 2 * NSA_KV_DIM], p3[:, :, COL_SMALL:COL_SMALL + 128],
                                cache_kv, layer_idx, page_table, win_cache, lw["cmp"])
    x1, conva = mix_out_ln(proj, _to_tb(y_b, bsz, dec_t), _to_tb(y_c, bsz, dec_t), x, lw["w_out"], lw["conv_a_w"],
                           lw["ln1_g"], lw["ln1_b"], _blocked_buf(st_conv_a), m, blocked)
    h, ffnc = ffn_up_act(x1, lw["w_up"], lw["ffn_conv_w"], _blocked_buf(st_ffn_conv), m, D_FF // 2, blocked)
    x2 = dense_res_ln(h, lw["w_down"], x1, lw["ln2_g"], lw["ln2_b"], TM, "ffn_down_ln")
    wlen = win_cache.shape[1]
    return x2, (kv_new.reshape(bsz, dec_t, 4, NSA_KV_HEADS, NSA_HD),
                win_state.reshape(bsz, wlen, 2, NSA_KV_HEADS, NSA_HD),
                _unblocked(conva, bsz), _unblocked(gconv, bsz), s_new, _unblocked(ffnc, bsz))


def stack_layers(states, i):
    return jnp.stack([s[i] for s in states], axis=0)


def kernel(x_prompt, x_sample, cache_nsa_kv, cache_nsa_win, state_conv_a, state_gdn_conv, state_gdn, state_ffn_conv, page_table, ln_emb_g, ln_emb_b, w_in, conv_a_w, gdn_conv_w, gdn_a_log, gdn_dt_bias, gdn_norm_g, cmp_pe, cmp_w1, cmp_b1, cmp_w2, w_out, ln1_g, ln1_b, w_up, ffn_conv_w, w_down, ln2_g, ln2_b):
    n_seq, t_len = x_prompt.shape[:2]
    dec_b, dec_t = x_sample.shape[:2]
    depth = w_in.shape[0]
    xp = layer_norm_rows(x_prompt.reshape(-1, D_MODEL), ln_emb_g, ln_emb_b)
    xs = layer_norm_rows(_to_tb(x_sample.reshape(-1, D_MODEL), dec_b, dec_t), ln_emb_g, ln_emb_b)
    w_in_b, w_out_b, w_up_b, w_down_b = (w.astype(BF16) for w in (permute_w_in(w_in), w_out, w_up, w_down))
    cache_kv = cache_nsa_kv.reshape(cache_nsa_kv.shape[:3] + (4 * NSA_KV_DIM,))
    win_cache = cache_nsa_win.reshape(cache_nsa_win.shape[:3] + (2 * NSA_KV_DIM,))
    st_p, st_s = [], []
    for l in range(depth):
        lw = dict(w_in=w_in_b[l], conv_a_w=conv_a_w[l], gdn_conv_w=gdn_conv_w[l], gdn_a_log=gdn_a_log[l],
                  gdn_dt_bias=gdn_dt_bias[l], gdn_norm_g=gdn_norm_g[l],
                  cmp=compress_params(cmp_pe[l], cmp_w1[l], cmp_b1[l], cmp_w2[l]),
                  w_out=w_out_b[l], ln1_g=ln1_g[l], ln1_b=ln1_b[l], w_up=w_up_b[l], ffn_conv_w=ffn_conv_w[l],
                  w_down=w_down_b[l], ln2_g=ln2_g[l], ln2_b=ln2_b[l])
        xp, sp = layer_prompt(xp, n_seq, t_len, lw)
        xs, ss = layer_sample(xs, dec_b, dec_t, lw, l, cache_kv, page_table, win_cache[l], state_conv_a[l],
                              state_gdn_conv[l], state_gdn[l], state_ffn_conv[l])
        st_p.append(sp)
        st_s.append(ss)
    xp = xp.reshape(n_seq, t_len, D_MODEL)
    xs = _to_bt(xs, dec_b, dec_t).reshape(dec_b, dec_t, D_MODEL)
    return (xp, xs,
            stack_layers(st_p, 0), stack_layers(st_s, 0),
            stack_layers(st_p, 1), stack_layers(st_s, 1),
            stack_layers(st_p, 2), stack_layers(st_s, 2),
            stack_layers(st_p, 3), stack_layers(st_s, 3),
            stack_layers(st_p, 4), stack_layers(st_s, 4),
            stack_layers(st_p, 5), stack_layers(st_s, 5))
```

```python
import functools
import math

import jax
import jax.numpy as jnp
import numpy as np
from jax import lax
from jax.experimental import pallas as pl
from jax.experimental.pallas import tpu as pltpu

F32 = jnp.float32
BF16 = jnp.bfloat16

D_MODEL = 1024
DEPTH = 4
HEAD_DIM = 64
A_WIDTH = 256
A_CONV = 3
GDN_WIDTH = 256
GDN_HEADS = 4
GDN_DK = 64
GDN_DV = 64
GDN_QK = 256
GDN_CONV = 4
GDN_CONV_CH = 768
GDN_CHUNK = 64
NSA_WIDTH = 512
NSA_HEADS = 8
NSA_KV_HEADS = 2
NSA_GROUP = 4
NSA_HD = 64
NSA_KV_DIM = 128
CMP_STRIDE = 16
CMP_BLOCK = 32
SEL_BLOCK = 64
SEL_TOPN = 8
WINDOW = 512
Q_BLOCK = 128
D_FF = 2816
FFN_CONV = 3
ALPHA = (2.0 * DEPTH) ** 0.25
LN_EPS = 1e-5
IN_SIZES = (A_WIDTH, A_WIDTH, A_WIDTH, GDN_QK, GDN_QK, GDN_WIDTH, GDN_WIDTH, GDN_HEADS, GDN_HEADS,
            NSA_WIDTH, 4 * NSA_KV_DIM, 2 * NSA_KV_DIM, 3 * NSA_HEADS)
N_IN = sum(IN_SIZES)

VMEM_LIMIT_BYTES = 56 * 1024 * 1024


def _ln_rows(z, g, b):
    mu = jnp.mean(z, axis=-1, keepdims=True)
    zc = z - mu
    var = jnp.mean(zc * zc, axis=-1, keepdims=True)
    return zc * lax.rsqrt(var + LN_EPS) * g + b


def _ln_kernel(x_ref, g_ref, b_ref, o_ref):
    o_ref[...] = _ln_rows(x_ref[...], g_ref[...], b_ref[...])


def layer_norm_rows(x, g, b, tm=512):
    m, d = x.shape
    return pl.pallas_call(
        _ln_kernel,
        grid=(m // tm,),
        in_specs=[pl.BlockSpec((tm, d), lambda i: (i, 0)),
                  pl.BlockSpec((1, d), lambda i: (0, 0)),
                  pl.BlockSpec((1, d), lambda i: (0, 0))],
        out_specs=pl.BlockSpec((tm, d), lambda i: (i, 0)),
        out_shape=jax.ShapeDtypeStruct((m, d), F32),
        name="ln_rows",
    )(x, g.reshape(1, d), b.reshape(1, d))


def _dense_kernel(x_ref, w_ref, o_ref):
    o_ref[...] = jnp.dot(x_ref[...].astype(BF16), w_ref[...], preferred_element_type=F32)


def dense(x, w, tm, tn, name):
    m, k = x.shape
    n = w.shape[1]
    return pl.pallas_call(
        _dense_kernel,
        grid=(n // tn, m // tm),
        in_specs=[pl.BlockSpec((tm, k), lambda j, i: (i, 0)),
                  pl.BlockSpec((k, tn), lambda j, i: (0, j))],
        out_specs=pl.BlockSpec((tm, tn), lambda j, i: (i, j)),
        out_shape=jax.ShapeDtypeStruct((m, n), F32),
        compiler_params=pltpu.CompilerParams(vmem_limit_bytes=VMEM_LIMIT_BYTES),
        name=name,
    )(x, w)


def _dense_res_ln_kernel(y_ref, w_ref, x_ref, g_ref, b_ref, o_ref):
    acc = jnp.dot(y_ref[...].astype(BF16), w_ref[...], preferred_element_type=F32)
    o_ref[...] = _ln_rows(ALPHA * x_ref[...] + acc, g_ref[...], b_ref[...])


def dense_res_ln(y, w, x, g, b, tm, name):
    m, k = y.shape
    d = w.shape[1]
    return pl.pallas_call(
        _dense_res_ln_kernel,
        grid=(m // tm,),
        in_specs=[pl.BlockSpec((tm, k), lambda i: (i, 0)),
                  pl.BlockSpec((k, d), lambda i: (0, 0)),
                  pl.BlockSpec((tm, d), lambda i: (i, 0)),
                  pl.BlockSpec((1, d), lambda i: (0, 0)),
                  pl.BlockSpec((1, d), lambda i: (0, 0))],
        out_specs=pl.BlockSpec((tm, d), lambda i: (i, 0)),
        out_shape=jax.ShapeDtypeStruct((m, d), F32),
        compiler_params=pltpu.CompilerParams(vmem_limit_bytes=VMEM_LIMIT_BYTES),
        name=name,
    )(y, w, x, g.reshape(1, d), b.reshape(1, d))


SUBLANES = 8


def _conv_rows_carry(x, tail, w_ref, ksize):
    row8 = lax.broadcasted_iota(jnp.int32, (SUBLANES, x.shape[1]), 0)
    y = x * w_ref[ksize - 1:ksize, :]
    for k in range(1, ksize):
        rolled = pltpu.roll(x, k, axis=0)
        first = jnp.where(row8 < k, pltpu.roll(tail, k, axis=0), rolled[0:SUBLANES])
        y = y + jnp.concatenate([first, rolled[SUBLANES:]], axis=0) * w_ref[ksize - 1 - k:ksize - k, :]
    return y


def _conv_rows_blocked(x, buf, w_ref, ksize, step):
    r = x.shape[0]
    y = x * w_ref[ksize - 1:ksize, :]
    for k in range(1, ksize):
        prev = jnp.concatenate([buf[(ksize - 1 - k) * step:(ksize - 1) * step], x[0:r - k * step]], axis=0)
        y = y + prev * w_ref[ksize - 1 - k:ksize - k, :]
    return y


def _conv_tile(x, buf_ref, st_ref, tail_s, w_ref, ksize, mode):
    kind, param = mode
    if kind == "carry":
        tm = x.shape[0]

        @pl.when(pl.program_id(0) % param == 0)
        def _sequence_start():
            tail_s[...] = buf_ref[0]

        y = _conv_rows_carry(x, tail_s[...], w_ref, ksize)
        tail_s[...] = x[tm - SUBLANES:tm]
        st_ref[0] = x[tm - SUBLANES:tm]
        return y
    y = _conv_rows_blocked(x, buf_ref[0], w_ref, ksize, param)
    st_ref[0] = x[x.shape[0] - (ksize - 1) * param:]
    return y


def _conv_specs(mode, ksize, c, tm):
    kind, param = mode
    if kind == "carry":
        rows = SUBLANES
        idx = lambda i: (i // param, 0, 0)
    else:
        rows = (ksize - 1) * param
        idx = lambda i: (0, 0, 0)
    return pl.BlockSpec((1, rows, c), idx), rows


def _mix_out_ln_kernel(ab_ref, ac_ref, ah_ref, yb_ref, yc_ref, x_ref, w_ref, cw_ref, g_ref, b_ref, buf_ref,
                       o_ref, st_ref, tail_s, *, mode):
    u = ac_ref[...] * ah_ref[...]
    z = _conv_tile(u, buf_ref, st_ref, tail_s, cw_ref, A_CONV, mode)
    y = jnp.concatenate([ab_ref[...] * z, yb_ref[...], yc_ref[...]], axis=1).astype(BF16)
    acc = jnp.dot(y, w_ref[...], preferred_element_type=F32)
    o_ref[...] = _ln_rows(ALPHA * x_ref[...] + acc, g_ref[...], b_ref[...])


def mix_out_ln(proj, y_b, y_c, x, w_out, conv_w, g, b, buf, tm, mode):
    m = x.shape[0]
    buf_spec, st_rows = _conv_specs(mode, A_CONV, A_WIDTH, tm)
    n_st = buf.shape[0]
    row = lambda c, w: pl.BlockSpec((tm, w), lambda i: (i, c // w))
    const = lambda shape: pl.BlockSpec(shape, lambda i: (0,) * len(shape))
    return pl.pallas_call(
        functools.partial(_mix_out_ln_kernel, mode=mode),
        grid=(m // tm,),
        in_specs=[row(COL_AB, A_WIDTH), row(COL_AC, A_WIDTH), row(COL_AH, A_WIDTH),
                  row(0, GDN_WIDTH), row(0, NSA_WIDTH), row(0, D_MODEL),
                  const((D_MODEL, D_MODEL)), const((A_CONV, A_WIDTH)), const((1, D_MODEL)), const((1, D_MODEL)),
                  buf_spec],
        out_specs=[row(0, D_MODEL), pl.BlockSpec((1, st_rows, A_WIDTH), buf_spec.index_map)],
        out_shape=[jax.ShapeDtypeStruct((m, D_MODEL), F32), jax.ShapeDtypeStruct((n_st, st_rows, A_WIDTH), F32)],
        scratch_shapes=[pltpu.VMEM((SUBLANES, A_WIDTH), F32)],
        compiler_params=pltpu.CompilerParams(dimension_semantics=("arbitrary",), vmem_limit_bytes=VMEM_LIMIT_BYTES),
        name="mix_out_ln",
    )(proj, proj, proj, y_b, y_c, x, w_out, conv_w, g.reshape(1, -1), b.reshape(1, -1), buf)


def _ffn_up_kernel(x_ref, wg_ref, wv_ref, cw_ref, buf_ref, h_ref, st_ref, tail_s, *, mode):
    xb = x_ref[...].astype(BF16)
    gate = jnp.dot(xb, wg_ref[...], preferred_element_type=F32)
    val = jnp.dot(xb, wv_ref[...], preferred_element_type=F32)
    gate = _conv_tile(gate, buf_ref, st_ref, tail_s, cw_ref, FFN_CONV, mode)
    h_ref[...] = (jax.nn.silu(gate) * val).astype(BF16)


def ffn_up_act(x, w_up, conv_w, buf, tm, tn, mode):
    m = x.shape[0]
    kind, param = mode
    nj = D_FF // tn
    if kind == "carry":
        assert nj == 1
        st_rows, st_idx = SUBLANES, (lambda i, j: (i // param, 0, j))
    else:
        assert tm == m
        st_rows, st_idx = (FFN_CONV - 1) * param, (lambda i, j: (0, 0, j))
    n_st = buf.shape[0]
    once = pl.Buffered(1) if nj == 1 else None
    return pl.pallas_call(
        functools.partial(_ffn_up_kernel, mode=mode),
        grid=(m // tm, nj),
        in_specs=[pl.BlockSpec((tm, D_MODEL), lambda i, j: (i, 0)),
                  pl.BlockSpec((D_MODEL, tn), lambda i, j: (0, j), pipeline_mode=once),
                  pl.BlockSpec((D_MODEL, tn), lambda i, j: (0, nj + j), pipeline_mode=once),
                  pl.BlockSpec((FFN_CONV, tn), lambda i, j: (0, j)),
                  pl.BlockSpec((1, st_rows, tn), st_idx)],
        out_specs=[pl.BlockSpec((tm, tn), lambda i, j: (i, j)),
                   pl.BlockSpec((1, st_rows, tn), st_idx)],
        out_shape=[jax.ShapeDtypeStruct((m, D_FF), BF16), jax.ShapeDtypeStruct((n_st, st_rows, D_FF), F32)],
        scratch_shapes=[pltpu.VMEM((SUBLANES, tn), F32)],
        compiler_params=pltpu.CompilerParams(dimension_semantics=("arbitrary", "arbitrary"),
                                             vmem_limit_bytes=VMEM_LIMIT_BYTES),
        name="ffn_up_act",
    )(x, w_up, w_up, conv_w, buf)


def _conv_silu_kernel(x_ref, cw_ref, buf_ref, o_ref, st_ref, tail_s, *, mode):
    o_ref[...] = jax.nn.silu(_conv_tile(x_ref[...], buf_ref, st_ref, tail_s, cw_ref, GDN_CONV, mode))


def gdn_conv_silu(proj, conv_w, buf, tm, mode):
    m = proj.shape[0]
    buf_spec, st_rows = _conv_specs(mode, GDN_CONV, GDN_CONV_CH, tm)
    n_st = buf.shape[0]
    return pl.pallas_call(
        functools.partial(_conv_silu_kernel, mode=mode),
        grid=(m // tm,),
        in_specs=[pl.BlockSpec((tm, GDN_CONV_CH), lambda i: (i, COL_GQKV // GDN_CONV_CH)),
                  pl.BlockSpec((GDN_CONV, GDN_CONV_CH), lambda i: (0, 0)),
                  buf_spec],
        out_specs=[pl.BlockSpec((tm, GDN_CONV_CH), lambda i: (i, 0)),
                   pl.BlockSpec((1, st_rows, GDN_CONV_CH), buf_spec.index_map)],
        out_shape=[jax.ShapeDtypeStruct((m, GDN_CONV_CH), F32),
                   jax.ShapeDtypeStruct((n_st, st_rows, GDN_CONV_CH), F32)],
        scratch_shapes=[pltpu.VMEM((SUBLANES, GDN_CONV_CH), F32)],
        compiler_params=pltpu.CompilerParams(dimension_semantics=("arbitrary",), vmem_limit_bytes=VMEM_LIMIT_BYTES),
        name="gdn_conv_silu",
    )(proj, conv_w, buf)


GDN_ROWS = 128


def _mm(a, b):
    return jnp.dot(a, b, preferred_element_type=F32)


def _split3(x):
    hi = x.astype(BF16)
    r1 = x - hi.astype(F32)
    mid = r1.astype(BF16)
    return hi, mid, (r1 - mid.astype(F32)).astype(BF16)


def _mm_exact_lhs(c, x):
    hi, mid, lo = _split3(x)
    return _mm(c, hi) + _mm(c, mid) + _mm(c, lo)


def _mm_exact_rhs(x, c):
    hi, mid, lo = _split3(x)
    return _mm(hi, c) + _mm(mid, c) + _mm(lo, c)


def _mm3(a, b):
    ah = a.astype(BF16)
    al = (a - ah.astype(F32)).astype(BF16)
    bh = b.astype(BF16)
    bl = (b - bh.astype(F32)).astype(BF16)
    return _mm(ah, bh) + _mm(ah, bl) + _mm(al, bh)


def _gdn_kernel(qkv_ref, gate_ref, sm_ref, alog_ref, dt_ref, gain_ref, ea_ref, eb_ref, tril_ref, cones_ref,
                hones_ref, s0_ref, y_ref, sout_ref, s_s, o_s, *, chunk, carry_state, tiles_per_seq):
    rws = GDN_ROWS
    nchunk = rws // chunk
    shift = chunk.bit_length() - 1
    nt = (((1,), (1,)), ((), ()))
    tn = (((0,), (0,)), ((), ()))

    if carry_state:
        @pl.when(pl.program_id(0) % tiles_per_seq == 0)
        def _sequence_start():
            s_s[...] = s0_ref[0]

    hones = hones_ref[...]

    def head_sum(x):
        xh = x.astype(BF16)
        return _mm(xh, hones) + _mm((x - xh.astype(F32)).astype(BF16), hones)

    qkv = qkv_ref[...]
    q, k, v = qkv[:, 0:GDN_QK], qkv[:, GDN_QK:2 * GDN_QK], qkv[:, 2 * GDN_QK:]
    q = q * lax.rsqrt(head_sum(q * q) + 1e-6) * (GDN_DK ** -0.5)
    k = k * lax.rsqrt(head_sum(k * k) + 1e-6)
    sm = sm_ref[...]
    g = -jnp.exp(alog_ref[...]) * jax.nn.softplus(_mm_exact_rhs(sm, ea_ref[...]) + dt_ref[...])
    beta = jax.nn.sigmoid(_mm_exact_rhs(sm, eb_ref[...]))
    gc = _mm_exact_lhs(tril_ref[...], g)
    gcl = _mm_exact_lhs(cones_ref[...], g)

    row = lax.broadcasted_iota(jnp.int32, (rws, rws), 0)
    col = lax.broadcasted_iota(jnp.int32, (rws, rws), 1)
    same = lax.shift_right_logical(row, shift) == lax.shift_right_logical(col, shift)
    eye = jnp.where(row == col, 1.0, 0.0)

    heads = range(GDN_HEADS)
    hsl = [slice(GDN_DK * h, GDN_DK * (h + 1)) for h in heads]
    incl = same & (row >= col)
    strict = same & (row > col)
    xs = [gc[:, 128 * h:128 * (h + 1)] for h in heads]
    gcols = [x[:, 0:GDN_DK] for x in xs]
    gends = [gcl[:, 128 * h:128 * h + GDN_DK] for h in heads]
    betas = [beta[:, 128 * h:128 * h + GDN_DK] for h in heads]
    decays = [jnp.where(incl, jnp.exp(jnp.where(incl, x - x.T, 0.0)), 0.0) for x in xs]
    kbs = [k[:, hsl[h]] * betas[h] for h in heads]
    khbs = [k[:, hsl[h]].astype(BF16) for h in heads]
    a_s = [jnp.where(strict, lax.dot_general(kbs[h].astype(BF16), khbs[h], nt, preferred_element_type=F32) * decays[h], 0.0)
           for h in heads]
    minvs = [eye - a for a in a_s]
    apows = [_mm3(a, a) for a in a_s]
    for step in range(shift - 1):
        minvs = [_mm3(minvs[h], eye + apows[h]) for h in heads]
        if step < shift - 2:
            apows = [_mm3(ap, ap) for ap in apows]
    egcs = [jnp.exp(gc_h) for gc_h in gcols]
    us = [_mm3(minvs[h], v[:, hsl[h]] * betas[h]) for h in heads]
    wbs = [_mm3(minvs[h], kbs[h] * egcs[h]).astype(BF16) for h in heads]
    a_qks = [(lax.dot_general(q[:, hsl[h]].astype(BF16), khbs[h], nt, preferred_element_type=F32) * decays[h]).astype(BF16)
             for h in heads]
    q_decs = [(q[:, hsl[h]] * egcs[h]).astype(BF16) for h in heads]
    k_decs = [(k[:, hsl[h]] * jnp.exp(gends[h] - gcols[h])).astype(BF16) for h in heads]
    g_lasts = [jnp.exp(g_h) for g_h in gends]
    states = [s_s[h] for h in heads] if carry_state else None
    v_new = [[] for _ in heads]
    o_state = [[] for _ in heads]
    for n in range(nchunk):
        r = slice(n * chunk, (n + 1) * chunk)
        olds = states if carry_state else [s0_ref[n, h] for h in heads]
        sbs = [s_old.astype(BF16) for s_old in olds]
        vns = [us[h][r] - _mm(wbs[h][r], sbs[h]) for h in heads]
        for h in heads:
            o_state[h].append(_mm(q_decs[h][r], sbs[h]))
            v_new[h].append(vns[h])
        news = [olds[h] * g_lasts[h][n * chunk:n * chunk + 1, :]
                + lax.dot_general(k_decs[h][r], vns[h].astype(BF16), tn, preferred_element_type=F32) for h in heads]
        if carry_state:
            states = news
        else:
            for h in heads:
                sout_ref[n, h] = news[h]
    for h in heads:
        v_all = jnp.concatenate(v_new[h], axis=0) if nchunk > 1 else v_new[h][0]
        o_all = jnp.concatenate(o_state[h], axis=0) if nchunk > 1 else o_state[h][0]
        o_s[:, hsl[h]] = o_all + _mm(a_qks[h], v_all.astype(BF16))
        if carry_state:
            s_s[h] = states[h]

    if carry_state:
        sout_ref[0] = s_s[...]
    o = o_s[...]
    o = o * lax.rsqrt(head_sum(o * o) * (1.0 / GDN_DV) + 1e-6)
    y_ref[...] = o * gain_ref[...] * jax.nn.silu(gate_ref[...])


def gated_delta(qkv_act, proj, s0, a_log, dt_bias, norm_g, chunk, carry_state, tiles_per_seq):
    m = qkv_act.shape[0]
    rws = GDN_ROWS
    idx = np.arange(rws)
    same = (idx[:, None] // chunk) == (idx[None, :] // chunk)
    tril = jnp.asarray(same & (idx[:, None] >= idx[None, :]), BF16)
    cones = jnp.asarray(same, BF16)
    lane = np.arange(GDN_WIDTH)
    hones = jnp.asarray(lane[:, None] // GDN_DK == lane[None, :] // GDN_DK, BF16)
    lane2 = np.arange(4 * 128) // 128
    smr = np.arange(128)
    ea = jnp.asarray(smr[:, None] == SM_GA + lane2[None, :], BF16)
    eb = jnp.asarray(smr[:, None] == SM_GB + lane2[None, :], BF16)
    alog_x = jnp.repeat(a_log.astype(F32), 128).reshape(1, 512)
    dt_x = jnp.repeat(dt_bias.astype(F32), 128).reshape(1, 512)
    gain_x = jnp.tile(norm_g.astype(F32), GDN_HEADS).reshape(1, GDN_WIDTH)
    if carry_state:
        s_blk, s_idx = (1, GDN_HEADS, GDN_DK, GDN_DV), (lambda i: (i // tiles_per_seq, 0, 0, 0))
    else:
        s_blk, s_idx = (rws // chunk, GDN_HEADS, GDN_DK, GDN_DV), (lambda i: (i, 0, 0, 0))
    const = lambda shape: pl.BlockSpec(shape, lambda i: (0,) * len(shape))
    return pl.pallas_call(
        functools.partial(_gdn_kernel, chunk=chunk, carry_state=carry_state, tiles_per_seq=tiles_per_seq),
        grid=(m // rws,),
        in_specs=[pl.BlockSpec((rws, GDN_CONV_CH), lambda i: (i, 0)),
                  pl.BlockSpec((rws, GDN_WIDTH), lambda i: (i, COL_GGATE // GDN_WIDTH)),
                  pl.BlockSpec((rws, 128), lambda i: (i, COL_SMALL // 128)),
                  const((1, 512)), const((1, 512)), const((1, GDN_WIDTH)),
                  const((128, 512)), const((128, 512)), const((rws, rws)), const((rws, rws)),
                  const((GDN_WIDTH, GDN_WIDTH)),
                  pl.BlockSpec(s_blk, s_idx)],
        out_specs=[pl.BlockSpec((rws, GDN_WIDTH), lambda i: (i, 0)), pl.BlockSpec(s_blk, s_idx)],
        out_shape=[jax.ShapeDtypeStruct((m, GDN_WIDTH), F32), jax.ShapeDtypeStruct(s0.shape, F32)],
        scratch_shapes=[pltpu.VMEM((GDN_HEADS, GDN_DK, GDN_DV), F32), pltpu.VMEM((rws, GDN_WIDTH), F32)],
        compiler_params=pltpu.CompilerParams(dimension_semantics=("arbitrary",), vmem_limit_bytes=VMEM_LIMIT_BYTES),
        name="gated_delta",
    )(qkv_act, proj, proj, alog_x, dt_x, gain_x, ea, eb, tril, cones, hones, s0)


COL_Q = 0
COL_KV = 512
COL_AB = 1024
COL_AC = 1280
COL_GQKV = 1536
COL_AH = 2304
COL_GGATE = 2560
COL_WIN = 2816
COL_SMALL = 3072
N_PROJ = 3200
SM_GA, SM_GB, SM_NG = 0, 4, 8


def permute_w_in(w_in):
    offs = np.concatenate([[0], np.cumsum(IN_SIZES)])
    grp = lambda k: w_in[..., offs[k]:offs[k + 1]]
    pad = jnp.zeros(w_in.shape[:-1] + (N_PROJ - COL_SMALL - 32,), w_in.dtype)
    order = [grp(9), grp(10), grp(0), grp(1), grp(3), grp(4), grp(5), grp(2), grp(6), grp(11),
             grp(7), grp(8), grp(12), pad]
    return jnp.concatenate(order, axis=-1)


NEG = -1e30
KEY_CHUNK = 128
SEL_CHUNK = 256


def _block_diag4(a, b):
    z = jnp.zeros_like(a)
    rows = [jnp.concatenate([a, z, z, z], -1), jnp.concatenate([z, a, z, z], -1),
            jnp.concatenate([z, z, b, z], -1), jnp.concatenate([z, z, z, b], -1)]
    return jnp.concatenate(rows, -2)


def compress_params(cmp_pe, cmp_w1, cmp_b1, cmp_w2):
    w1 = _block_diag4(cmp_w1[0], cmp_w1[1]).astype(BF16)
    w2 = _block_diag4(cmp_w2[0], cmp_w2[1]).astype(BF16)
    pe = jnp.concatenate([cmp_pe[0], cmp_pe[0], cmp_pe[1], cmp_pe[1]], -1)
    b1 = jnp.concatenate([cmp_b1[0], cmp_b1[0], cmp_b1[1], cmp_b1[1]], -1).reshape(1, 256)
    return w1, pe, b1, w2


def _compress_rows(row_loader, ncp, w1_ref, pe_ref, b1_ref, w2_ref):
    acc_lo = jnp.zeros((ncp, 256), F32)
    acc_hi = jnp.zeros((ncp, 256), F32)
    for l in range(CMP_STRIDE):
        x = row_loader(l)
        acc_lo += jnp.dot((x + pe_ref[l:l + 1, :]).astype(BF16), w1_ref[l], preferred_element_type=F32)
        acc_hi += jnp.dot((x + pe_ref[l + CMP_STRIDE:l + CMP_STRIDE + 1, :]).astype(BF16),
                          w1_ref[l + CMP_STRIDE], preferred_element_type=F32)
    hid = jax.nn.gelu(acc_lo + pltpu.roll(acc_hi, ncp - 1, axis=0) + b1_ref[...])
    return jnp.dot(hid.astype(BF16), w2_ref[...], preferred_element_type=F32)


def _cmp_attend(q_rows, kc, vc, pos, ovl, n_heads, tq, ncp):
    s = lax.dot_general(q_rows, kc, (((1,), (1,)), ((), ())), preferred_element_type=F32)
    cend = lax.broadcasted_iota(jnp.int32, (tq, ncp), 1) * CMP_STRIDE + (CMP_BLOCK - 1)
    s3 = jnp.where((cend <= pos)[None], s.reshape(n_heads, tq, ncp), -jnp.inf)
    m = jnp.max(s3, axis=-1, keepdims=True)
    m = jnp.where(m == -jnp.inf, 0.0, m)
    p = jnp.exp(s3 - m)
    pn = p / jnp.maximum(jnp.sum(p, axis=-1, keepdims=True), 1e-30)
    o_cmp = jnp.dot(pn.reshape(n_heads * tq, ncp).astype(BF16), vc, preferred_element_type=F32)
    psum = pn[0]
    for g in range(1, n_heads):
        psum = psum + pn[g]
    p_hi = psum.astype(BF16)
    p_lo = (psum - p_hi.astype(F32)).astype(BF16)
    imp = (jnp.dot(p_hi, ovl, preferred_element_type=F32) + jnp.dot(p_lo, ovl, preferred_element_type=F32))
    return o_cmp, imp


def _select_topn(imp, pos):
    r, ns = imp.shape
    blk = lax.broadcasted_iota(jnp.int32, (r, ns), 1)
    blk_f = blk.astype(F32)
    forced = (blk == 0) | (blk == lax.shift_right_logical(pos, 6))
    valid = blk * SEL_BLOCK <= pos
    v = jnp.where(forced, jnp.inf, jnp.where(valid, imp, -jnp.inf))
    sel = jnp.zeros((r, ns), F32)
    for _ in range(min(SEL_TOPN, ns)):
        mx = jnp.max(v, axis=-1, keepdims=True)
        idx = jnp.min(jnp.where(v == mx, blk_f, float(ns)), axis=-1, keepdims=True)
        hit = blk_f == idx
        sel = jnp.where(hit, 1.0, sel)
        v = jnp.where(hit, -jnp.inf, v)
    return sel


def _nsa_prompt_kernel(q_ref, cmp_ref, slc_ref, win_ref, sm_ref, w1_ref, pe_ref, b1_ref, w2_ref, ovl_ref, exp_ref,
                       y_ref, kc_s, vc_s, ks_s, vs_s, kw_s, vw_s, q_s, sel_s, ocmp_s, owin_s, m_s, acc_s, *, t_len):
    tq = KEY_CHUNK
    i = pl.program_id(1)
    ncp = t_len // CMP_STRIDE
    ns = t_len // SEL_BLOCK
    ng = NSA_GROUP
    rows = ng * tq

    @pl.when(i == 0)
    def _prepare_sequence():
        kvc = _compress_rows(lambda l: cmp_ref[0, l], ncp, w1_ref, pe_ref, b1_ref, w2_ref)
        kc_s[...] = kvc[:, 0:128].astype(BF16)
        vc_s[...] = kvc[:, 128:256].astype(BF16)
        ones = jnp.ones((512, 64), BF16)

        def cast_rows(r, carry):
            sl = pl.ds(pl.multiple_of(r * 512, 512), 512)
            for h in range(NSA_KV_HEADS):
                ks_s[h, sl, :] = slc_ref[sl, 64 * h:64 * h + 64].astype(BF16)
                vs_s[h, sl, 0:64] = slc_ref[sl, 128 + 64 * h:192 + 64 * h].astype(BF16)
                vs_s[h, sl, 64:128] = ones
                kw_s[h, sl, :] = win_ref[sl, 64 * h:64 * h + 64].astype(BF16)
                vw_s[h, sl, 0:64] = win_ref[sl, 128 + 64 * h:192 + 64 * h].astype(BF16)
                vw_s[h, sl, 64:128] = ones
            return carry

        lax.fori_loop(0, t_len // 512, cast_rows, 0)

    pos = i * tq + lax.broadcasted_iota(jnp.int32, (tq, 1), 0)
    gates = jax.nn.sigmoid(sm_ref[:, SM_NG:SM_NG + 3 * NSA_HEADS])
    nt = (((1,), (1,)), ((), ()))

    wk = min(WINDOW + KEY_CHUNK, t_len)
    w0 = pl.multiple_of(jnp.minimum(jnp.maximum(i - WINDOW // KEY_CHUNK, 0), (t_len - wk) // KEY_CHUNK) * KEY_CHUNK,
                        KEY_CHUNK)
    wpos = w0 + lax.broadcasted_iota(jnp.int32, (tq, wk), 1)
    wbias = jnp.where(wpos <= pos, jnp.where(wpos > pos - WINDOW, 0.0, NEG), NEG)

    hs = range(NSA_KV_HEADS)
    q_rows = []
    for h in hs:
        qh = q_ref[:, 256 * h:256 * h + 256] * (NSA_HD ** -0.5)
        q_rows.append(jnp.concatenate([qh[:, 64 * g:64 * g + 64] for g in range(ng)], axis=0).astype(BF16))
        q_s[h] = q_rows[h]
    cmp = [_cmp_attend(q_rows[h], kc_s[:, 64 * h:64 * h + 64], vc_s[:, 64 * h:64 * h + 64], pos, ovl_ref[...],
                       ng, tq, ncp) for h in hs]
    sel = _select_topn(jnp.concatenate([c[1] for c in cmp], axis=0), jnp.concatenate([pos] * NSA_KV_HEADS, axis=0))
    sw = [lax.dot_general(q_rows[h], kw_s[h, pl.ds(w0, wk), :], nt, preferred_element_type=F32) for h in hs]
    sw = [(s.reshape(ng, tq, wk) + wbias[None]).reshape(rows, wk) for s in sw]
    pw = [jnp.exp(s - jnp.max(s, axis=-1, keepdims=True)).astype(BF16) for s in sw]
    aw = [jnp.dot(pw[h], vw_s[h, pl.ds(w0, wk), :], preferred_element_type=F32) for h in hs]
    for h in hs:
        ocmp_s[h] = cmp[h][0]
        sel_s[h] = sel[h * tq:(h + 1) * tq].astype(BF16)
        owin_s[h] = aw[h][:, 0:64] / aw[h][:, 64:128]
        m_s[h] = jnp.full((rows, 128), NEG, F32)
        acc_s[h] = jnp.zeros((rows, 128), F32)

    lane = lax.broadcasted_iota(jnp.int32, (tq, SEL_CHUNK), 1)

    def sel_body(j, carry):
        off = pl.multiple_of(j * SEL_CHUNK, SEL_CHUNK)
        hs = range(NSA_KV_HEADS)
        pairs = [(h, slice(g * tq, (g + 1) * tq)) for h in hs for g in range(ng)]
        ks = [ks_s[h, pl.ds(off, SEL_CHUNK), :] for h in hs]
        vs = [vs_s[h, pl.ds(off, SEL_CHUNK), :] for h in hs]
        causal = off + lane <= pos
        biases = [jnp.where(causal, (jnp.dot(sel_s[h], exp_ref[:, pl.ds(off, SEL_CHUNK)],
                                            preferred_element_type=F32) - 1.0) * (-NEG), NEG) for h in hs]
        ss = [lax.dot_general(q_s[h, r, :], ks[h], nt, preferred_element_type=F32) + biases[h] for h, r in pairs]
        m_prevs = [m_s[h, r, :] for h, r in pairs]
        m_cols = [jnp.maximum(mp[:, 0:1], jnp.max(s, axis=-1, keepdims=True)) for mp, s in zip(m_prevs, ss)]
        ps = [jnp.exp(s - mc).astype(BF16) for s, mc in zip(ss, m_cols)]
        pvs = [jnp.dot(p, vs[h], preferred_element_type=F32) for p, (h, r) in zip(ps, pairs)]
        for (h, r), mp, mc, pv in zip(pairs, m_prevs, m_cols, pvs):
            acc_s[h, r, :] = jnp.exp(mp - mc) * acc_s[h, r, :] + pv
            m_s[h, r, :] = jnp.broadcast_to(mc, (tq, 128))
        return carry

    lax.fori_loop(0, (i * tq) // SEL_CHUNK + 1, sel_body, 0)

    for h in range(NSA_KV_HEADS):
        acc = acc_s[h]
        o_slc = acc[:, 0:64] / acc[:, 64:128]
        o_cmp = ocmp_s[h]
        o_win = owin_s[h]
        for g in range(ng):
            hh = ng * h + g
            r = slice(g * tq, (g + 1) * tq)
            y_ref[:, 64 * hh:64 * hh + 64] = (gates[:, hh:hh + 1] * o_cmp[r]
                                              + gates[:, NSA_HEADS + hh:NSA_HEADS + hh + 1] * o_slc[r]
                                              + gates[:, 2 * NSA_HEADS + hh:2 * NSA_HEADS + hh + 1] * o_win[r])


def nsa_prompt(proj, n_seq, t_len, cparams):
    tq = KEY_CHUNK
    nt = t_len // tq
    ncp = t_len // CMP_STRIDE
    ns = t_len // SEL_BLOCK
    w1, pe, b1, w2 = cparams
    cstart = np.arange(ncp)[:, None] * CMP_STRIDE
    sstart = np.arange(ns)[None, :] * SEL_BLOCK
    ovl = jnp.asarray((cstart < sstart + SEL_BLOCK) & (cstart + CMP_BLOCK > sstart), BF16)
    expand = jnp.asarray(np.arange(t_len)[None, :] // SEL_BLOCK == np.arange(ns)[:, None], BF16)
    cmp_rows = proj[:, COL_KV:COL_KV + 256].reshape(n_seq, ncp, CMP_STRIDE, 256).transpose(0, 2, 1, 3)
    once = pl.Buffered(1)
    const = lambda shape: pl.BlockSpec(shape, lambda b, i: (0,) * len(shape), pipeline_mode=once)
    return pl.pallas_call(
        functools.partial(_nsa_prompt_kernel, t_len=t_len),
        grid=(n_seq, nt),
        in_specs=[pl.BlockSpec((tq, 512), lambda b, i: (b * nt + i, COL_Q // 512)),
                  pl.BlockSpec((1, CMP_STRIDE, ncp, 256), lambda b, i: (b, 0, 0, 0), pipeline_mode=once),
                  pl.BlockSpec((t_len, 256), lambda b, i: (b, (COL_KV + 256) // 256), pipeline_mode=once),
                  pl.BlockSpec((t_len, 256), lambda b, i: (b, COL_WIN // 256), pipeline_mode=once),
                  pl.BlockSpec((tq, 128), lambda b, i: (b * nt + i, COL_SMALL // 128)),
                  const((CMP_BLOCK, 256, 256)), const((CMP_BLOCK, 256)), const((1, 256)), const((256, 256)),
                  const((ncp, ns)), const((ns, t_len))],
        out_specs=pl.BlockSpec((tq, NSA_WIDTH), lambda b, i: (b * nt + i, 0)),
        out_shape=jax.ShapeDtypeStruct((n_seq * t_len, NSA_WIDTH), F32),
        scratch_shapes=[pltpu.VMEM((ncp, 128), BF16), pltpu.VMEM((ncp, 128), BF16),
                        pltpu.VMEM((NSA_KV_HEADS, t_len, 64), BF16), pltpu.VMEM((NSA_KV_HEADS, t_len, 128), BF16),
                        pltpu.VMEM((NSA_KV_HEADS, t_len, 64), BF16), pltpu.VMEM((NSA_KV_HEADS, t_len, 128), BF16),
                        pltpu.VMEM((NSA_KV_HEADS, NSA_GROUP * tq, 64), BF16), pltpu.VMEM((NSA_KV_HEADS, tq, ns), BF16),
                        pltpu.VMEM((NSA_KV_HEADS, NSA_GROUP * tq, 64), F32),
                        pltpu.VMEM((NSA_KV_HEADS, NSA_GROUP * tq, 64), F32),
                        pltpu.VMEM((NSA_KV_HEADS, NSA_GROUP * tq, 128), F32),
                        pltpu.VMEM((NSA_KV_HEADS, NSA_GROUP * tq, 128), F32)],
        compiler_params=pltpu.CompilerParams(dimension_semantics=("arbitrary", "arbitrary"),
                                             vmem_limit_bytes=VMEM_LIMIT_BYTES),
        name="nsa_prompt",
    )(proj, cmp_rows, proj, proj, proj, w1, pe, b1, w2, ovl, expand)


PAGE_ROWS = 128
SEL_LANES = 128


def _softmax_segments(segs):
    m = segs[0].max(axis=-1, keepdims=True)
    for s in segs[1:]:
        m = jnp.maximum(m, s.max(axis=-1, keepdims=True))
    ps = [jnp.exp(s - m) for s in segs]
    den = ps[0].sum(axis=-1, keepdims=True)
    for p in ps[1:]:
        den = den + p.sum(axis=-1, keepdims=True)
    return ps, den


def _nsa_sample_kernel(pt_ref, *refs, n_pages, past_len, dec_t):
    del pt_ref
    pages = refs[:n_pages]
    (q_ref, kvn_ref, wn_ref, sm_ref, wc_ref, w1_ref, pe_ref, b1_ref, w2_ref, ovl_ref, exp_ref,
     y_ref, wst_ref, rk_s, rv_s, newpg_s, neww_s) = refs[n_pages:]
    ncp = past_len // CMP_STRIDE
    ng, nh = NSA_GROUP, NSA_KV_HEADS
    rows_h = ng * dec_t
    rows = nh * rows_h
    nt = (((1,), (1,)), ((), ()))

    for p in range(n_pages):
        rk_s[p * PAGE_ROWS:(p + 1) * PAGE_ROWS, :] = pages[p][0:128, :].T
        rv_s[p * PAGE_ROWS:(p + 1) * PAGE_ROWS, :] = pages[p][128:256, :].T
    kvc = _compress_rows(
        lambda l: jnp.concatenate([rk_s[pl.ds(l, ncp, stride=CMP_STRIDE), :],
                                   rv_s[pl.ds(l, ncp, stride=CMP_STRIDE), :]], axis=1),
        ncp, w1_ref, pe_ref, b1_ref, w2_ref)
    kc = kvc[:, 0:128].astype(BF16)
    vc = kvc[:, 128:256].astype(BF16)

    newpg_s[...] = jnp.zeros((PAGE_ROWS, 256), F32)
    newpg_s[0:dec_t, :] = kvn_ref[0][:, 256:512]
    neww_s[...] = jnp.zeros((PAGE_ROWS, 256), F32)
    neww_s[0:dec_t, :] = wn_ref[0]

    qf = q_ref[0] * (NSA_HD ** -0.5)
    zero = jnp.zeros((dec_t, 64), F32)
    qrows = []
    for h in range(nh):
        for g in range(ng):
            piece = qf[:, 64 * (ng * h + g):64 * (ng * h + g) + 64]
            qrows.append(jnp.concatenate([piece, zero] if h == 0 else [zero, piece], axis=1))
    q_bd = jnp.concatenate(qrows, axis=0).astype(BF16)
    t_row = lax.broadcasted_iota(jnp.int32, (rows, 1), 0) & (dec_t - 1)
    pos = past_len + t_row
    head0 = lax.broadcasted_iota(jnp.int32, (rows, 128), 0) < rows_h
    lane_lo = lax.broadcasted_iota(jnp.int32, (rows, 128), 1) < 64
    own = head0 == lane_lo

    def own_half(x):
        x = jnp.where(own, x, 0.0)
        return x[:, 0:64] + x[:, 64:128]

    s = lax.dot_general(q_bd, kc, nt, preferred_element_type=F32)
    cend = lax.broadcasted_iota(jnp.int32, (rows, ncp), 1) * CMP_STRIDE + (CMP_BLOCK - 1)
    s = jnp.where(cend <= pos, s, -jnp.inf)
    m = jnp.max(s, axis=-1, keepdims=True)
    m = jnp.where(m == -jnp.inf, 0.0, m)
    p = jnp.exp(s - m)
    pn = p / jnp.maximum(jnp.sum(p, axis=-1, keepdims=True), 1e-30)
    o_cmp = own_half(_mm(pn.astype(BF16), vc))

    ovl = ovl_ref[...]
    psums = []
    for h in range(nh):
        psum = pn[h * rows_h:h * rows_h + dec_t]
        for g in range(1, ng):
            psum = psum + pn[h * rows_h + g * dec_t:h * rows_h + (g + 1) * dec_t]
        psums.append(psum)
    psum = jnp.concatenate(psums, axis=0)
    p_hi = psum.astype(BF16)
    imp = _mm(p_hi, ovl) + _mm((psum - p_hi.astype(F32)).astype(BF16), ovl)
    pos_ht = past_len + (lax.broadcasted_iota(jnp.int32, (nh * dec_t, 1), 0) & (dec_t - 1))
    sel = _select_topn(imp, pos_ht)
    sel_b = jnp.concatenate([sel[h * dec_t:(h + 1) * dec_t] for h in range(nh) for _ in range(ng)],
                            axis=0).astype(BF16)

    lane = lax.broadcasted_iota(jnp.int32, (rows, PAGE_ROWS), 1)
    segs = []
    for pg in range(n_pages + 1):
        if pg < n_pages:
            sc = _mm(q_bd, pages[pg][256:384, :].astype(BF16))
        else:
            sc = lax.dot_general(q_bd, newpg_s[:, 0:128].astype(BF16), nt, preferred_element_type=F32)
        chosen = _mm(sel_b, exp_ref[:, pg * PAGE_ROWS:(pg + 1) * PAGE_ROWS])
        ok = chosen > 0.5
        if pg == n_pages:
            ok = ok & (past_len + lane <= pos)
        segs.append(jnp.where(ok, sc, NEG))
    ps, den = _softmax_segments(segs)
    acc = jnp.zeros((rows, 128), F32)
    for pg in range(n_pages + 1):
        if pg < n_pages:
            acc = acc + lax.dot_general(ps[pg].astype(BF16), pages[pg][384:512, :].astype(BF16), nt,
                                        preferred_element_type=F32)
        else:
            acc = acc + _mm(ps[pg].astype(BF16), newpg_s[:, 128:256].astype(BF16))
    o_slc = own_half(acc) / den

    wlen = wc_ref.shape[2]
    wc = wc_ref[0]
    jw = lax.broadcasted_iota(jnp.int32, (rows, wlen), 1)
    s_old = _mm(q_bd, wc[0:128, :].astype(BF16))
    s_old = jnp.where(past_len - wlen + jw > pos - WINDOW, s_old, NEG)
    s_new = lax.dot_general(q_bd, neww_s[:, 0:128].astype(BF16), nt, preferred_element_type=F32)
    s_new = jnp.where(past_len + lane <= pos, s_new, NEG)
    ps, den = _softmax_segments([s_old, s_new])
    acc = (lax.dot_general(ps[0].astype(BF16), wc[128:256, :].astype(BF16), nt, preferred_element_type=F32)
           + _mm(ps[1].astype(BF16), neww_s[:, 128:256].astype(BF16)))
    o_win = own_half(acc) / den

    gates = jax.nn.sigmoid(sm_ref[0][:, SM_NG:SM_NG + 3 * NSA_HEADS])
    for hh in range(NSA_HEADS):
        r = slice(hh * dec_t, (hh + 1) * dec_t)
        y_ref[0, :, 64 * hh:64 * hh + 64] = (gates[:, hh:hh + 1] * o_cmp[r]
                                             + gates[:, NSA_HEADS + hh:NSA_HEADS + hh + 1] * o_slc[r]
                                             + gates[:, 2 * NSA_HEADS + hh:2 * NSA_HEADS + hh + 1] * o_win[r])

    rolled = pltpu.roll(wc, wlen - dec_t, axis=1)
    new_t = jnp.concatenate([neww_s[:, 0:128].T, neww_s[:, 128:256].T], axis=0)
    lane_w = lax.broadcasted_iota(jnp.int32, (256, 128), 1)
    last = jnp.where(lane_w >= 128 - dec_t, pltpu.roll(new_t, 128 - dec_t, axis=1), rolled[:, wlen - 128:wlen])
    wst_ref[0, :, 0:wlen - 128] = rolled[:, 0:wlen - 128]
    wst_ref[0, :, wlen - 128:wlen] = last


def nsa_sample(q, kv_new, win_new, sm, cache_kv, layer_idx, page_table, win_cache, cparams):
    bsz, dec_t, _ = q.shape
    n_pages = page_table.shape[1]
    past_len = n_pages * PAGE_ROWS
    wlen = win_cache.shape[2]
    assert dec_t < CMP_STRIDE and dec_t % SUBLANES == 0 and dec_t & (dec_t - 1) == 0
    assert wlen == WINDOW and wlen <= past_len
    ncp = past_len // CMP_STRIDE
    ns = -(-(past_len + dec_t) // SEL_BLOCK)
    assert ns <= SEL_LANES
    w1, pe, b1, w2 = cparams
    cstart = np.arange(ncp)[:, None] * CMP_STRIDE
    sstart = np.arange(SEL_LANES)[None, :] * SEL_BLOCK
    ovl = jnp.asarray((cstart < sstart + SEL_BLOCK) & (cstart + CMP_BLOCK > sstart) & (np.arange(SEL_LANES)[None, :] < ns), BF16)
    keys = np.arange((n_pages + 1) * PAGE_ROWS)
    expand = jnp.asarray(keys[None, :] // SEL_BLOCK == np.arange(SEL_LANES)[:, None], BF16)
    const = lambda shape: pl.BlockSpec(shape, lambda b, pt: (0,) * len(shape))
    seq = lambda r, c: pl.BlockSpec((1, r, c), lambda b, pt: (b, 0, 0))
    page_specs = [pl.BlockSpec((None, None, 512, PAGE_ROWS), functools.partial(
        lambda b, pt, p: (layer_idx, pt[b, p], 0, 0), p=p)) for p in range(n_pages)]
    grid_spec = pltpu.PrefetchScalarGridSpec(
        num_scalar_prefetch=1,
        grid=(bsz,),
        in_specs=page_specs + [seq(dec_t, 512), seq(dec_t, 512), seq(dec_t, 256), seq(dec_t, 128), seq(256, wlen),
                               const((CMP_BLOCK, 256, 256)), const((CMP_BLOCK, 256)), const((1, 256)),
                               const((256, 256)), const((ncp, SEL_LANES)),
                               const((SEL_LANES, (n_pages + 1) * PAGE_ROWS))],
        out_specs=[seq(dec_t, NSA_WIDTH), seq(256, wlen)],
        scratch_shapes=[pltpu.VMEM((past_len, 128), F32), pltpu.VMEM((past_len, 128), F32),
                        pltpu.VMEM((PAGE_ROWS, 256), F32), pltpu.VMEM((PAGE_ROWS, 256), F32)])
    return pl.pallas_call(
        functools.partial(_nsa_sample_kernel, n_pages=n_pages, past_len=past_len, dec_t=dec_t),
        grid_spec=grid_spec,
        out_shape=[jax.ShapeDtypeStruct((bsz, dec_t, NSA_WIDTH), F32), jax.ShapeDtypeStruct((bsz, 256, wlen), F32)],
        compiler_params=pltpu.CompilerParams(dimension_semantics=("arbitrary",), vmem_limit_bytes=VMEM_LIMIT_BYTES),
        name="nsa_sample",
    )(page_table, *([cache_kv] * n_pages), q, kv_new, win_new, sm, win_cache, w1, pe, b1, w2, ovl, expand)


def l2_normalize(x):
    return x * lax.rsqrt(jnp.sum(jnp.square(x), axis=-1, keepdims=True) + 1e-6)


def gated_rms_norm(o, gain, gate):
    of = o.astype(F32)
    of = of * lax.rsqrt(jnp.mean(jnp.square(of), axis=-1, keepdims=True) + 1e-6)
    return (of * gain.astype(F32) * jax.nn.silu(gate.astype(F32))).astype(gate.dtype)


def masked_softmax(s, mask):
    s = jnp.where(mask, s.astype(F32), -jnp.inf)
    m = jnp.max(s, axis=-1, keepdims=True)
    m = jnp.where(jnp.isfinite(m), m, 0.0)
    p = jnp.exp(s - m)
    return p / jnp.maximum(jnp.sum(p, axis=-1, keepdims=True), 1e-30)


def causal_dwconv(x, buf, w):
    k = w.shape[0]
    t = x.shape[1]
    xp = jnp.concatenate([buf.astype(x.dtype), x], axis=1)
    y = sum(xp[:, i:i + t] * w[i] for i in range(k))
    return y, xp[:, t:]


def split_in(proj):
    return jnp.split(proj, np.cumsum(IN_SIZES)[:-1].tolist(), axis=-1)


def gated_delta_rule(q, k, v, a_in, b_in, s0, a_log, dt_bias):
    b, t, h, dk = q.shape
    dv = v.shape[-1]
    q = l2_normalize(q.astype(F32)) * (dk ** -0.5)
    k = l2_normalize(k.astype(F32))
    v = v.astype(F32)
    g = -jnp.exp(a_log.astype(F32)) * jax.nn.softplus(a_in.astype(F32) + dt_bias.astype(F32))
    beta = jax.nn.sigmoid(b_in.astype(F32))
    c = GDN_CHUNK if t % GDN_CHUNK == 0 else t
    n = t // c

    def to_chunks(z):
        return jnp.moveaxis(z.reshape((b, n, c) + z.shape[2:]), 3, 2)

    qc, kc, vc, gc, bc = (to_chunks(z) for z in (q, k, v, g, beta))
    gc = jnp.cumsum(gc, axis=-1)
    incl = jnp.tril(jnp.ones((c, c), dtype=bool))
    strict = jnp.tril(jnp.ones((c, c), dtype=bool), -1)
    diff = gc[..., :, None] - gc[..., None, :]
    decay = jnp.where(incl, jnp.exp(jnp.where(incl, diff, 0.0)), 0.0)
    kb = kc * bc[..., None]
    m = jnp.eye(c, dtype=F32) + jnp.where(strict, jnp.einsum('bnhik,bnhjk->bnhij', kb, kc) * decay, 0.0)
    rhs = jnp.concatenate([vc * bc[..., None], kb * jnp.exp(gc)[..., None]], axis=-1)
    sol = lax.linalg.triangular_solve(m, rhs, left_side=True, lower=True, unit_diagonal=True)
    u, w = sol[..., :dv], sol[..., dv:]
    a_qk = jnp.einsum('bnhik,bnhjk->bnhij', qc, kc) * decay
    q_dec = qc * jnp.exp(gc)[..., None]
    k_dec = kc * jnp.exp(gc[..., -1:] - gc)[..., None]
    g_last = jnp.exp(gc[..., -1])

    def step(state, xs):
        u_n, w_n, q_n, k_n, a_n, gl_n = xs
        v_new = u_n - jnp.einsum('bhik,bhkv->bhiv', w_n, state)
        o_n = jnp.einsum('bhik,bhkv->bhiv', q_n, state) + jnp.einsum('bhij,bhjv->bhiv', a_n, v_new)
        state = state * gl_n[..., None, None] + jnp.einsum('bhik,bhiv->bhkv', k_n, v_new)
        return state, o_n

    xs = tuple(jnp.moveaxis(z, 1, 0) for z in (u, w, q_dec, k_dec, a_qk, g_last))
    s_final, o = lax.scan(step, s0.astype(F32), xs)
    o = jnp.moveaxis(jnp.moveaxis(o, 0, 1), 2, 3).reshape(b, t, h, dv)
    return o, s_final


def compress_blocks(rows, pe, w1, b1, w2):
    b, t_pad, kvh, hd = rows.shape
    r = rows.reshape(b, t_pad // CMP_STRIDE, CMP_STRIDE, kvh, hd)
    blocks = jnp.concatenate([r[:, :-1], r[:, 1:]], axis=2) + pe[:, None, :]
    hid = jax.nn.gelu(jnp.einsum('bclhd,lde->bche', blocks, w1) + b1)
    return jnp.einsum('bche,ed->bchd', hid, w2)


def select_attend(qg, idx, q_pos, ks, vs):
    b, kvh = ks.shape[:2]
    tq = qg.shape[1]
    n = idx.shape[-1]
    b_ix = jnp.arange(b)[:, None, None, None]
    h_ix = jnp.arange(kvh)[None, :, None, None]
    kg = ks[b_ix, h_ix, idx].reshape(b, kvh, tq, n * SEL_BLOCK, NSA_HD)
    vg = vs[b_ix, h_ix, idx].reshape(b, kvh, tq, n * SEL_BLOCK, NSA_HD)
    k_pos = (idx[..., None] * SEL_BLOCK + jnp.arange(SEL_BLOCK)).reshape(b, kvh, 1, tq, n * SEL_BLOCK)
    s = jnp.einsum('bqhgd,bhqkd->bhgqk', qg, kg) * (NSA_HD ** -0.5)
    p = masked_softmax(s, k_pos <= q_pos[:, None])
    return jnp.einsum('bhgqk,bhqkd->bqhgd', p.astype(vg.dtype), vg)


def nsa_compressed_selected(qg, kv, q_pos, cmp_pe, cmp_w1, cmp_b1, cmp_w2):
    b, t = kv.shape[:2]
    tq = qg.shape[1]
    t_pad = -(-t // SEL_BLOCK) * SEL_BLOCK
    kv = jnp.pad(kv, ((0, 0), (0, t_pad - t), (0, 0), (0, 0), (0, 0)))
    kc = compress_blocks(kv[:, :, 0], cmp_pe[0], cmp_w1[0], cmp_b1[0], cmp_w2[0])
    vc = compress_blocks(kv[:, :, 1], cmp_pe[1], cmp_w1[1], cmp_b1[1], cmp_w2[1])
    nc = kc.shape[1]
    cmp_start = jnp.arange(nc) * CMP_STRIDE
    s = jnp.einsum('bqhgd,bchd->bhgqc', qg, kc) * (NSA_HD ** -0.5)
    p = masked_softmax(s, cmp_start[None, :] + (CMP_BLOCK - 1) <= q_pos[:, None])
    o_cmp = jnp.einsum('bhgqc,bchd->bqhgd', p.astype(vc.dtype), vc)
    ns = t_pad // SEL_BLOCK
    sel_start = jnp.arange(ns) * SEL_BLOCK
    overlap = ((cmp_start[:, None] < sel_start[None, :] + SEL_BLOCK)
               & (cmp_start[:, None] + CMP_BLOCK > sel_start[None, :])).astype(F32)
    imp = jnp.einsum('bhgqc,cn->bhqn', p, overlap)
    blk = jnp.arange(ns)[None, :]
    forced = (blk == 0) | (blk == q_pos[:, None] // SEL_BLOCK)
    valid = sel_start[None, :] <= q_pos[:, None]
    imp = jnp.where(forced, jnp.inf, jnp.where(valid, imp, -jnp.inf))
    n_top = min(SEL_TOPN, ns)
    _, idx = lax.top_k(imp, n_top)
    ks = jnp.moveaxis(kv[:, :, 2].reshape(b, ns, SEL_BLOCK, NSA_KV_HEADS, NSA_HD), 3, 1)
    vs = jnp.moveaxis(kv[:, :, 3].reshape(b, ns, SEL_BLOCK, NSA_KV_HEADS, NSA_HD), 3, 1)
    if tq % Q_BLOCK == 0:
        nb = tq // Q_BLOCK
        qb = jnp.moveaxis(qg.reshape(b, nb, Q_BLOCK, NSA_KV_HEADS, NSA_GROUP, NSA_HD), 1, 0)
        ib = jnp.moveaxis(idx.reshape(b, NSA_KV_HEADS, nb, Q_BLOCK, n_top), 2, 0)
        pb = q_pos.reshape(nb, Q_BLOCK)
        ob = lax.map(lambda a: select_attend(a[0], a[1], a[2], ks, vs), (qb, ib, pb))
        o_slc = jnp.moveaxis(ob, 0, 1).reshape(b, tq, NSA_KV_HEADS, NSA_GROUP, NSA_HD)
    else:
        o_slc = select_attend(qg, idx, q_pos, ks, vs)
    return o_cmp, o_slc


def window_banded(qg, kvw):
    b, t = kvw.shape[:2]
    nb = t // Q_BLOCK
    nw = WINDOW // Q_BLOCK
    kp = jnp.pad(kvw, ((0, 0), (WINDOW, 0), (0, 0), (0, 0), (0, 0))).reshape(b, nb + nw, Q_BLOCK, 2, NSA_KV_HEADS, NSA_HD)
    band = jnp.concatenate([kp[:, i:i + nb] for i in range(nw + 1)], axis=2)
    qb = qg.reshape(b, nb, Q_BLOCK, NSA_KV_HEADS, NSA_GROUP, NSA_HD)
    start = jnp.arange(nb)[:, None] * Q_BLOCK
    qpos = start + jnp.arange(Q_BLOCK)
    kpos = start - WINDOW + jnp.arange((nw + 1) * Q_BLOCK)
    qp, kp_ = qpos[:, :, None], kpos[:, None, :]
    mask = (kp_ <= qp) & (kp_ > qp - WINDOW) & (kp_ >= 0)
    s = jnp.einsum('bnqhgd,bnkhd->bnhgqk', qb, band[:, :, :, 0]) * (NSA_HD ** -0.5)
    p = masked_softmax(s, mask[None, :, None, None])
    o = jnp.einsum('bnhgqk,bnkhd->bnqhgd', p.astype(band.dtype), band[:, :, :, 1])
    return o.reshape(b, t, NSA_KV_HEADS, NSA_GROUP, NSA_HD)


def window_dense(qg, kvw_all, q_pos, k_pos):
    s = jnp.einsum('bqhgd,bkhd->bhgqk', qg, kvw_all[:, :, 0]) * (NSA_HD ** -0.5)
    mask = (k_pos[None, :] <= q_pos[:, None]) & (k_pos[None, :] > q_pos[:, None] - WINDOW)
    p = masked_softmax(s, mask)
    return jnp.einsum('bhgqk,bkhd->bqhgd', p.astype(kvw_all.dtype), kvw_all[:, :, 1])


def layer(x, q_pos, past, w_in, conv_a_w, gdn_conv_w, gdn_a_log, gdn_dt_bias, gdn_norm_g,
          cmp_pe, cmp_w1, cmp_b1, cmp_w2, w_out, ln1_g, ln1_b, w_up, ffn_conv_w, w_down, ln2_g, ln2_b):
    raise NotImplementedError


TM = 512


def _to_tb(x, b, t):
    return x.reshape(b, t, -1).transpose(1, 0, 2).reshape(t * b, -1)


def _to_bt(x, b, t):
    return x.reshape(t, b, -1).transpose(1, 0, 2).reshape(b * t, -1)


def _blocked_buf(state):
    b, k1, c = state.shape
    return state.transpose(1, 0, 2).reshape(1, k1 * b, c)


def _unblocked(st, b):
    return st.reshape(-1, b, st.shape[-1]).transpose(1, 0, 2)


def layer_prompt(x, n_seq, t_len, lw):
    tiles = t_len // TM
    carry = ("carry", tiles)
    zeros = lambda c: jnp.zeros((n_seq, SUBLANES, c), F32)
    proj = dense(x, lw["w_in"], TM, N_PROJ, "in_proj")
    qkv, gconv = gdn_conv_silu(proj, lw["gdn_conv_w"], zeros(GDN_CONV_CH), TM, carry)
    y_b, s_new = gated_delta(qkv, proj, jnp.zeros((n_seq, GDN_HEADS, GDN_DK, GDN_DV), F32), lw["gdn_a_log"],
                             lw["gdn_dt_bias"], lw["gdn_norm_g"], GDN_CHUNK, True, t_len // GDN_ROWS)
    y_c = nsa_prompt(proj, n_seq, t_len, lw["cmp"])
    x1, conva = mix_out_ln(proj, y_b, y_c, x, lw["w_out"], lw["conv_a_w"], lw["ln1_g"], lw["ln1_b"],
                           zeros(A_WIDTH), TM, carry)
    h, ffnc = ffn_up_act(x1, lw["w_up"], lw["ffn_conv_w"], zeros(D_FF), TM, D_FF, carry)
    x2 = dense_res_ln(h, lw["w_down"], x1, lw["ln2_g"], lw["ln2_b"], TM, "ffn_down_ln")
    p3 = proj.reshape(n_seq, t_len, N_PROJ)
    kv_new = p3[:, :, COL_KV:COL_KV + 4 * NSA_KV_DIM].reshape(n_seq, t_len, 4, NSA_KV_HEADS, NSA_HD)
    wkeep = min(WINDOW, t_len)
    win_state = p3[:, t_len - wkeep:, COL_WIN:COL_WIN + 2 * NSA_KV_DIM].reshape(n_seq, wkeep, 2, NSA_KV_HEADS, NSA_HD)
    tail = lambda st, k: st[:, SUBLANES - (k - 1):, :]
    return x2, (kv_new, win_state, tail(conva, A_CONV), tail(gconv, GDN_CONV), s_new, tail(ffnc, FFN_CONV))


def layer_sample(x, bsz, dec_t, lw, layer_idx, cache_kv, page_table, win_cache, st_conv_a, st_gdn_conv, st_gdn,
                 st_ffn_conv):
    m = dec_t * bsz
    blocked = ("blocked", bsz)
    proj = dense(x, lw["w_in"], TM, N_PROJ, "in_proj")
    proj_bt = _to_bt(proj, bsz, dec_t)
    qkv, gconv = gdn_conv_silu(proj, lw["gdn_conv_w"], _blocked_buf(st_gdn_conv), m, blocked)
    y_b, s_new = gated_delta(_to_bt(qkv, bsz, dec_t), proj_bt, st_gdn, lw["gdn_a_log"], lw["gdn_dt_bias"],
                             lw["gdn_norm_g"], dec_t, False, 1)
    p3 = proj_bt.reshape(bsz, dec_t, N_PROJ)
    kv_new = p3[:, :, COL_KV:COL_KV + 4 * NSA_KV_DIM]
    y_c, win_state = nsa_sample(p3[:, :, COL_Q:COL_Q + NSA_WIDTH], kv_new,
                                p3[:, :, COL_WIN:COL_WIN + 2 * NSA_KV_DIM], p3[:, :, COL_SMALL:COL_SMALL + 128],
                                cache_kv, layer_idx, page_table, win_cache, lw["cmp"])
    x1, conva = mix_out_ln(proj, _to_tb(y_b, bsz, dec_t), _to_tb(y_c, bsz, dec_t), x, lw["w_out"], lw["conv_a_w"],
                           lw["ln1_g"], lw["ln1_b"], _blocked_buf(st_conv_a), m, blocked)
    h, ffnc = ffn_up_act(x1, lw["w_up"], lw["ffn_conv_w"], _blocked_buf(st_ffn_conv), m, D_FF // 2, blocked)
    x2 = dense_res_ln(h, lw["w_down"], x1, lw["ln2_g"], lw["ln2_b"], TM, "ffn_down_ln")
    wlen = win_cache.shape[2]
    return x2, (kv_new.reshape(bsz, dec_t, 4, NSA_KV_HEADS, NSA_HD),
                win_state.reshape(bsz, 2, NSA_KV_HEADS, NSA_HD, wlen).transpose(0, 4, 1, 2, 3),
                _unblocked(conva, bsz), _unblocked(gconv, bsz), s_new, _unblocked(ffnc, bsz))


def stack_layers(states, i):
    return jnp.stack([s[i] for s in states], axis=0)


def kernel(x_prompt, x_sample, cache_nsa_kv, cache_nsa_win, state_conv_a, state_gdn_conv, state_gdn, state_ffn_conv, page_table, ln_emb_g, ln_emb_b, w_in, conv_a_w, gdn_conv_w, gdn_a_log, gdn_dt_bias, gdn_norm_g, cmp_pe, cmp_w1, cmp_b1, cmp_w2, w_out, ln1_g, ln1_b, w_up, ffn_conv_w, w_down, ln2_g, ln2_b):
    n_seq, t_len = x_prompt.shape[:2]
    dec_b, dec_t = x_sample.shape[:2]
    depth = w_in.shape[0]
    xp = layer_norm_rows(x_prompt.reshape(-1, D_MODEL), ln_emb_g, ln_emb_b)
    xs = layer_norm_rows(_to_tb(x_sample.reshape(-1, D_MODEL), dec_b, dec_t), ln_emb_g, ln_emb_b)
    w_in_b = permute_w_in(w_in.astype(BF16))
    w_out_b, w_up_b, w_down_b = (w.astype(BF16) for w in (w_out, w_up, w_down))
    cache_kv = cache_nsa_kv.transpose(0, 1, 3, 4, 5, 2).reshape(cache_nsa_kv.shape[:2] + (4 * NSA_KV_DIM, -1))
    win_cache = cache_nsa_win.transpose(0, 1, 3, 4, 5, 2).reshape(cache_nsa_win.shape[:2] + (2 * NSA_KV_DIM, -1))
    st_p, st_s = [], []
    for l in range(depth):
        lw = dict(w_in=w_in_b[l], conv_a_w=conv_a_w[l], gdn_conv_w=gdn_conv_w[l], gdn_a_log=gdn_a_log[l],
                  gdn_dt_bias=gdn_dt_bias[l], gdn_norm_g=gdn_norm_g[l],
                  cmp=compress_params(cmp_pe[l], cmp_w1[l], cmp_b1[l], cmp_w2[l]),
                  w_out=w_out_b[l], ln1_g=ln1_g[l], ln1_b=ln1_b[l], w_up=w_up_b[l], ffn_conv_w=ffn_conv_w[l],
                  w_down=w_down_b[l], ln2_g=ln2_g[l], ln2_b=ln2_b[l])
        xp, sp = layer_prompt(xp, n_seq, t_len, lw)
        xs, ss = layer_sample(xs, dec_b, dec_t, lw, l, cache_kv, page_table, win_cache[l], state_conv_a[l],
                              state_gdn_conv[l], state_gdn[l], state_ffn_conv[l])
        st_p.append(sp)
        st_s.append(ss)
    xp = xp.reshape(n_seq, t_len, D_MODEL)
    xs = _to_bt(xs, dec_b, dec_t).reshape(dec_b, dec_t, D_MODEL)
    return (xp, xs,
            stack_layers(st_p, 0), stack_layers(st_s, 0),
            stack_layers(st_p, 1), stack_layers(st_s, 1),
            stack_layers(st_p, 2), stack_layers(st_s, 2),
            stack_layers(st_p, 3), stack_layers(st_s, 3),
            stack_layers(st_p, 4), stack_layers(st_s, 4),
            stack_layers(st_p, 5), stack_layers(st_s, 5))
```

```python
import functools
import math

import jax
import jax.numpy as jnp
import numpy as np
from jax import lax
from jax.experimental import pallas as pl
from jax.experimental.pallas import tpu as pltpu

F32 = jnp.float32
BF16 = jnp.bfloat16

D_MODEL = 1024
DEPTH = 4
HEAD_DIM = 64
A_WIDTH = 256
A_CONV = 3
GDN_WIDTH = 256
GDN_HEADS = 4
GDN_DK = 64
GDN_DV = 64
GDN_QK = 256
GDN_CONV = 4
GDN_CONV_CH = 768
GDN_CHUNK = 64
NSA_WIDTH = 512
NSA_HEADS = 8
NSA_KV_HEADS = 2
NSA_GROUP = 4
NSA_HD = 64
NSA_KV_DIM = 128
CMP_STRIDE = 16
CMP_BLOCK = 32
SEL_BLOCK = 64
SEL_TOPN = 8
WINDOW = 512
Q_BLOCK = 128
D_FF = 2816
FFN_CONV = 3
ALPHA = (2.0 * DEPTH) ** 0.25
LN_EPS = 1e-5
IN_SIZES = (A_WIDTH, A_WIDTH, A_WIDTH, GDN_QK, GDN_QK, GDN_WIDTH, GDN_WIDTH, GDN_HEADS, GDN_HEADS,
            NSA_WIDTH, 4 * NSA_KV_DIM, 2 * NSA_KV_DIM, 3 * NSA_HEADS)
N_IN = sum(IN_SIZES)

VMEM_LIMIT_BYTES = 56 * 1024 * 1024


def _ln_rows(z, g, b):
    mu = jnp.mean(z, axis=-1, keepdims=True)
    zc = z - mu
    var = jnp.mean(zc * zc, axis=-1, keepdims=True)
    return zc * lax.rsqrt(var + LN_EPS) * g + b


def _ln_kernel(x_ref, g_ref, b_ref, o_ref):
    o_ref[...] = _ln_rows(x_ref[...], g_ref[...], b_ref[...])


def layer_norm_rows(x, g, b, tm=512):
    m, d = x.shape
    return pl.pallas_call(
        _ln_kernel,
        grid=(m // tm,),
        in_specs=[pl.BlockSpec((tm, d), lambda i: (i, 0)),
                  pl.BlockSpec((1, d), lambda i: (0, 0)),
                  pl.BlockSpec((1, d), lambda i: (0, 0))],
        out_specs=pl.BlockSpec((tm, d), lambda i: (i, 0)),
        out_shape=jax.ShapeDtypeStruct((m, d), F32),
        name="ln_rows",
    )(x, g.reshape(1, d), b.reshape(1, d))


def _dense_kernel(x_ref, w_ref, o_ref):
    o_ref[...] = jnp.dot(x_ref[...].astype(BF16), w_ref[...], preferred_element_type=F32)


def dense(x, w, tm, tn, name):
    m, k = x.shape
    n = w.shape[1]
    return pl.pallas_call(
        _dense_kernel,
        grid=(n // tn, m // tm),
        in_specs=[pl.BlockSpec((tm, k), lambda j, i: (i, 0)),
                  pl.BlockSpec((k, tn), lambda j, i: (0, j))],
        out_specs=pl.BlockSpec((tm, tn), lambda j, i: (i, j)),
        out_shape=jax.ShapeDtypeStruct((m, n), F32),
        compiler_params=pltpu.CompilerParams(vmem_limit_bytes=VMEM_LIMIT_BYTES),
        name=name,
    )(x, w)


def _dense_res_ln_kernel(y_ref, w_ref, x_ref, g_ref, b_ref, o_ref):
    acc = jnp.dot(y_ref[...].astype(BF16), w_ref[...], preferred_element_type=F32)
    o_ref[...] = _ln_rows(ALPHA * x_ref[...] + acc, g_ref[...], b_ref[...])


def dense_res_ln(y, w, x, g, b, tm, name):
    m, k = y.shape
    d = w.shape[1]
    return pl.pallas_call(
        _dense_res_ln_kernel,
        grid=(m // tm,),
        in_specs=[pl.BlockSpec((tm, k), lambda i: (i, 0)),
                  pl.BlockSpec((k, d), lambda i: (0, 0)),
                  pl.BlockSpec((tm, d), lambda i: (i, 0)),
                  pl.BlockSpec((1, d), lambda i: (0, 0)),
                  pl.BlockSpec((1, d), lambda i: (0, 0))],
        out_specs=pl.BlockSpec((tm, d), lambda i: (i, 0)),
        out_shape=jax.ShapeDtypeStruct((m, d), F32),
        compiler_params=pltpu.CompilerParams(vmem_limit_bytes=VMEM_LIMIT_BYTES),
        name=name,
    )(y, w, x, g.reshape(1, d), b.reshape(1, d))


SUBLANES = 8


def _conv_rows_carry(x, tail, w_ref, ksize):
    row8 = lax.broadcasted_iota(jnp.int32, (SUBLANES, x.shape[1]), 0)
    y = x * w_ref[ksize - 1:ksize, :]
    for k in range(1, ksize):
        rolled = pltpu.roll(x, k, axis=0)
        first = jnp.where(row8 < k, pltpu.roll(tail, k, axis=0), rolled[0:SUBLANES])
        y = y + jnp.concatenate([first, rolled[SUBLANES:]], axis=0) * w_ref[ksize - 1 - k:ksize - k, :]
    return y


def _conv_rows_blocked(x, buf, w_ref, ksize, step):
    r = x.shape[0]
    y = x * w_ref[ksize - 1:ksize, :]
    for k in range(1, ksize):
        prev = jnp.concatenate([buf[(ksize - 1 - k) * step:(ksize - 1) * step], x[0:r - k * step]], axis=0)
        y = y + prev * w_ref[ksize - 1 - k:ksize - k, :]
    return y


def _conv_tile(x, buf_ref, st_ref, tail_s, w_ref, ksize, mode):
    kind, param = mode
    if kind == "carry":
        tm = x.shape[0]

        @pl.when(pl.program_id(0) % param == 0)
        def _sequence_start():
            tail_s[...] = buf_ref[0]

        y = _conv_rows_carry(x, tail_s[...], w_ref, ksize)
        tail_s[...] = x[tm - SUBLANES:tm]
        st_ref[0] = x[tm - SUBLANES:tm]
        return y
    y = _conv_rows_blocked(x, buf_ref[0], w_ref, ksize, param)
    st_ref[0] = x[x.shape[0] - (ksize - 1) * param:]
    return y


def _conv_specs(mode, ksize, c, tm):
    kind, param = mode
    if kind == "carry":
        rows = SUBLANES
        idx = lambda i: (i // param, 0, 0)
    else:
        rows = (ksize - 1) * param
        idx = lambda i: (0, 0, 0)
    return pl.BlockSpec((1, rows, c), idx), rows


def _mix_out_ln_kernel(ab_ref, ac_ref, ah_ref, yb_ref, yc_ref, x_ref, w_ref, cw_ref, g_ref, b_ref, buf_ref,
                       o_ref, st_ref, tail_s, *, mode):
    u = ac_ref[...] * ah_ref[...]
    z = _conv_tile(u, buf_ref, st_ref, tail_s, cw_ref, A_CONV, mode)
    y = jnp.concatenate([ab_ref[...] * z, yb_ref[...], yc_ref[...]], axis=1).astype(BF16)
    acc = jnp.dot(y, w_ref[...], preferred_element_type=F32)
    o_ref[...] = _ln_rows(ALPHA * x_ref[...] + acc, g_ref[...], b_ref[...])


def mix_out_ln(proj, y_b, y_c, x, w_out, conv_w, g, b, buf, tm, mode):
    m = x.shape[0]
    buf_spec, st_rows = _conv_specs(mode, A_CONV, A_WIDTH, tm)
    n_st = buf.shape[0]
    row = lambda c, w: pl.BlockSpec((tm, w), lambda i: (i, c // w))
    const = lambda shape: pl.BlockSpec(shape, lambda i: (0,) * len(shape))
    return pl.pallas_call(
        functools.partial(_mix_out_ln_kernel, mode=mode),
        grid=(m // tm,),
        in_specs=[row(COL_AB, A_WIDTH), row(COL_AC, A_WIDTH), row(COL_AH, A_WIDTH),
                  row(0, GDN_WIDTH), row(0, NSA_WIDTH), row(0, D_MODEL),
                  const((D_MODEL, D_MODEL)), const((A_CONV, A_WIDTH)), const((1, D_MODEL)), const((1, D_MODEL)),
                  buf_spec],
        out_specs=[row(0, D_MODEL), pl.BlockSpec((1, st_rows, A_WIDTH), buf_spec.index_map)],
        out_shape=[jax.ShapeDtypeStruct((m, D_MODEL), F32), jax.ShapeDtypeStruct((n_st, st_rows, A_WIDTH), F32)],
        scratch_shapes=[pltpu.VMEM((SUBLANES, A_WIDTH), F32)],
        compiler_params=pltpu.CompilerParams(dimension_semantics=("arbitrary",), vmem_limit_bytes=VMEM_LIMIT_BYTES),
        name="mix_out_ln",
    )(proj, proj, proj, y_b, y_c, x, w_out, conv_w, g.reshape(1, -1), b.reshape(1, -1), buf)


def _ffn_up_kernel(x_ref, wg_ref, wv_ref, cw_ref, buf_ref, h_ref, st_ref, tail_s, *, mode):
    xb = x_ref[...].astype(BF16)
    gate = jnp.dot(xb, wg_ref[...], preferred_element_type=F32)
    val = jnp.dot(xb, wv_ref[...], preferred_element_type=F32)
    gate = _conv_tile(gate, buf_ref, st_ref, tail_s, cw_ref, FFN_CONV, mode)
    h_ref[...] = (jax.nn.silu(gate) * val).astype(BF16)


def ffn_up_act(x, w_up, conv_w, buf, tm, tn, mode):
    m = x.shape[0]
    kind, param = mode
    nj = D_FF // tn
    if kind == "carry":
        assert nj == 1
        st_rows, st_idx = SUBLANES, (lambda i, j: (i // param, 0, j))
    else:
        assert tm == m
        st_rows, st_idx = (FFN_CONV - 1) * param, (lambda i, j: (0, 0, j))
    n_st = buf.shape[0]
    once = pl.Buffered(1) if nj == 1 else None
    return pl.pallas_call(
        functools.partial(_ffn_up_kernel, mode=mode),
        grid=(m // tm, nj),
        in_specs=[pl.BlockSpec((tm, D_MODEL), lambda i, j: (i, 0)),
                  pl.BlockSpec((D_MODEL, tn), lambda i, j: (0, j), pipeline_mode=once),
                  pl.BlockSpec((D_MODEL, tn), lambda i, j: (0, nj + j), pipeline_mode=once),
                  pl.BlockSpec((FFN_CONV, tn), lambda i, j: (0, j)),
                  pl.BlockSpec((1, st_rows, tn), st_idx)],
        out_specs=[pl.BlockSpec((tm, tn), lambda i, j: (i, j)),
                   pl.BlockSpec((1, st_rows, tn), st_idx)],
        out_shape=[jax.ShapeDtypeStruct((m, D_FF), BF16), jax.ShapeDtypeStruct((n_st, st_rows, D_FF), F32)],
        scratch_shapes=[pltpu.VMEM((SUBLANES, tn), F32)],
        compiler_params=pltpu.CompilerParams(dimension_semantics=("arbitrary", "arbitrary"),
                                             vmem_limit_bytes=VMEM_LIMIT_BYTES),
        name="ffn_up_act",
    )(x, w_up, w_up, conv_w, buf)


def _conv_silu_kernel(x_ref, cw_ref, buf_ref, o_ref, st_ref, tail_s, *, mode):
    o_ref[...] = jax.nn.silu(_conv_tile(x_ref[...], buf_ref, st_ref, tail_s, cw_ref, GDN_CONV, mode))


def gdn_conv_silu(proj, conv_w, buf, tm, mode):
    m = proj.shape[0]
    buf_spec, st_rows = _conv_specs(mode, GDN_CONV, GDN_CONV_CH, tm)
    n_st = buf.shape[0]
    return pl.pallas_call(
        functools.partial(_conv_silu_kernel, mode=mode),
        grid=(m // tm,),
        in_specs=[pl.BlockSpec((tm, GDN_CONV_CH), lambda i: (i, COL_GQKV // GDN_CONV_CH)),
                  pl.BlockSpec((GDN_CONV, GDN_CONV_CH), lambda i: (0, 0)),
                  buf_spec],
        out_specs=[pl.BlockSpec((tm, GDN_CONV_CH), lambda i: (i, 0)),
                   pl.BlockSpec((1, st_rows, GDN_CONV_CH), buf_spec.index_map)],
        out_shape=[jax.ShapeDtypeStruct((m, GDN_CONV_CH), F32),
                   jax.ShapeDtypeStruct((n_st, st_rows, GDN_CONV_CH), F32)],
        scratch_shapes=[pltpu.VMEM((SUBLANES, GDN_CONV_CH), F32)],
        compiler_params=pltpu.CompilerParams(dimension_semantics=("arbitrary",), vmem_limit_bytes=VMEM_LIMIT_BYTES),
        name="gdn_conv_silu",
    )(proj, conv_w, buf)


GDN_ROWS = 128


def _mm(a, b):
    return jnp.dot(a, b, preferred_element_type=F32)


def _split3(x):
    hi = x.astype(BF16)
    r1 = x - hi.astype(F32)
    mid = r1.astype(BF16)
    return hi, mid, (r1 - mid.astype(F32)).astype(BF16)


def _mm_exact_lhs(c, x):
    hi, mid, lo = _split3(x)
    return _mm(c, hi) + _mm(c, mid) + _mm(c, lo)


def _mm_exact_rhs(x, c):
    hi, mid, lo = _split3(x)
    return _mm(hi, c) + _mm(mid, c) + _mm(lo, c)


def _mm3(a, b):
    ah = a.astype(BF16)
    al = (a - ah.astype(F32)).astype(BF16)
    bh = b.astype(BF16)
    bl = (b - bh.astype(F32)).astype(BF16)
    return _mm(ah, bh) + _mm(ah, bl) + _mm(al, bh)


def _gdn_kernel(qkv_ref, gate_ref, sm_ref, alog_ref, dt_ref, gain_ref, ea_ref, eb_ref, tril_ref, cones_ref,
                hones_ref, s0_ref, y_ref, sout_ref, s_s, o_s, *, chunk, carry_state, tiles_per_seq):
    rws = GDN_ROWS
    nchunk = rws // chunk
    shift = chunk.bit_length() - 1
    nt = (((1,), (1,)), ((), ()))
    tn = (((0,), (0,)), ((), ()))

    if carry_state:
        @pl.when(pl.program_id(0) % tiles_per_seq == 0)
        def _sequence_start():
            s_s[...] = s0_ref[0]

    hones = hones_ref[...]

    def head_sum(x):
        xh = x.astype(BF16)
        return _mm(xh, hones) + _mm((x - xh.astype(F32)).astype(BF16), hones)

    qkv = qkv_ref[...]
    q, k, v = qkv[:, 0:GDN_QK], qkv[:, GDN_QK:2 * GDN_QK], qkv[:, 2 * GDN_QK:]
    q = q * lax.rsqrt(head_sum(q * q) + 1e-6) * (GDN_DK ** -0.5)
    k = k * lax.rsqrt(head_sum(k * k) + 1e-6)
    sm = sm_ref[...]
    g = -jnp.exp(alog_ref[...]) * jax.nn.softplus(_mm_exact_rhs(sm, ea_ref[...]) + dt_ref[...])
    beta = jax.nn.sigmoid(_mm_exact_rhs(sm, eb_ref[...]))
    gc = _mm_exact_lhs(tril_ref[...], g)
    gcl = _mm_exact_lhs(cones_ref[...], g)

    row = lax.broadcasted_iota(jnp.int32, (rws, rws), 0)
    col = lax.broadcasted_iota(jnp.int32, (rws, rws), 1)
    same = lax.shift_right_logical(row, shift) == lax.shift_right_logical(col, shift)
    eye = jnp.where(row == col, 1.0, 0.0)

    heads = range(GDN_HEADS)
    hsl = [slice(GDN_DK * h, GDN_DK * (h + 1)) for h in heads]
    incl = same & (row >= col)
    strict = same & (row > col)
    xs = [gc[:, 128 * h:128 * (h + 1)] for h in heads]
    gcols = [x[:, 0:GDN_DK] for x in xs]
    gends = [gcl[:, 128 * h:128 * h + GDN_DK] for h in heads]
    betas = [beta[:, 128 * h:128 * h + GDN_DK] for h in heads]
    decays = [jnp.where(incl, jnp.exp(jnp.where(incl, x - x.T, 0.0)), 0.0) for x in xs]
    kbs = [k[:, hsl[h]] * betas[h] for h in heads]
    khbs = [k[:, hsl[h]].astype(BF16) for h in heads]
    a_s = [jnp.where(strict, lax.dot_general(kbs[h].astype(BF16), khbs[h], nt, preferred_element_type=F32) * decays[h], 0.0)
           for h in heads]
    minvs = [eye - a for a in a_s]
    apows = [_mm3(a, a) for a in a_s]
    for step in range(shift - 1):
        minvs = [_mm3(minvs[h], eye + apows[h]) for h in heads]
        if step < shift - 2:
            apows = [_mm3(ap, ap) for ap in apows]
    egcs = [jnp.exp(gc_h) for gc_h in gcols]
    us = [_mm3(minvs[h], v[:, hsl[h]] * betas[h]) for h in heads]
    wbs = [_mm3(minvs[h], kbs[h] * egcs[h]).astype(BF16) for h in heads]
    a_qks = [(lax.dot_general(q[:, hsl[h]].astype(BF16), khbs[h], nt, preferred_element_type=F32) * decays[h]).astype(BF16)
             for h in heads]
    q_decs = [(q[:, hsl[h]] * egcs[h]).astype(BF16) for h in heads]
    k_decs = [(k[:, hsl[h]] * jnp.exp(gends[h] - gcols[h])).astype(BF16) for h in heads]
    g_lasts = [jnp.exp(g_h) for g_h in gends]
    states = [s_s[h] for h in heads] if carry_state else None
    v_new = [[] for _ in heads]
    o_state = [[] for _ in heads]
    for n in range(nchunk):
        r = slice(n * chunk, (n + 1) * chunk)
        olds = states if carry_state else [s0_ref[n, h] for h in heads]
        sbs = [s_old.astype(BF16) for s_old in olds]
        vns = [us[h][r] - _mm(wbs[h][r], sbs[h]) for h in heads]
        for h in heads:
            o_state[h].append(_mm(q_decs[h][r], sbs[h]))
            v_new[h].append(vns[h])
        news = [olds[h] * g_lasts[h][n * chunk:n * chunk + 1, :]
                + lax.dot_general(k_decs[h][r], vns[h].astype(BF16), tn, preferred_element_type=F32) for h in heads]
        if carry_state:
            states = news
        else:
            for h in heads:
                sout_ref[n, h] = news[h]
    for h in heads:
        v_all = jnp.concatenate(v_new[h], axis=0) if nchunk > 1 else v_new[h][0]
        o_all = jnp.concatenate(o_state[h], axis=0) if nchunk > 1 else o_state[h][0]
        o_s[:, hsl[h]] = o_all + _mm(a_qks[h], v_all.astype(BF16))
        if carry_state:
            s_s[h] = states[h]

    if carry_state:
        sout_ref[0] = s_s[...]
    o = o_s[...]
    o = o * lax.rsqrt(head_sum(o * o) * (1.0 / GDN_DV) + 1e-6)
    y_ref[...] = o * gain_ref[...] * jax.nn.silu(gate_ref[...])


def gated_delta(qkv_act, proj, s0, a_log, dt_bias, norm_g, chunk, carry_state, tiles_per_seq):
    m = qkv_act.shape[0]
    rws = GDN_ROWS
    idx = np.arange(rws)
    same = (idx[:, None] // chunk) == (idx[None, :] // chunk)
    tril = jnp.asarray(same & (idx[:, None] >= idx[None, :]), BF16)
    cones = jnp.asarray(same, BF16)
    lane = np.arange(GDN_WIDTH)
    hones = jnp.asarray(lane[:, None] // GDN_DK == lane[None, :] // GDN_DK, BF16)
    lane2 = np.arange(4 * 128) // 128
    smr = np.arange(128)
    ea = jnp.asarray(smr[:, None] == SM_GA + lane2[None, :], BF16)
    eb = jnp.asarray(smr[:, None] == SM_GB + lane2[None, :], BF16)
    alog_x = jnp.repeat(a_log.astype(F32), 128).reshape(1, 512)
    dt_x = jnp.repeat(dt_bias.astype(F32), 128).reshape(1, 512)
    gain_x = jnp.tile(norm_g.astype(F32), GDN_HEADS).reshape(1, GDN_WIDTH)
    if carry_state:
        s_blk, s_idx = (1, GDN_HEADS, GDN_DK, GDN_DV), (lambda i: (i // tiles_per_seq, 0, 0, 0))
    else:
        s_blk, s_idx = (rws // chunk, GDN_HEADS, GDN_DK, GDN_DV), (lambda i: (i, 0, 0, 0))
    const = lambda shape: pl.BlockSpec(shape, lambda i: (0,) * len(shape))
    return pl.pallas_call(
        functools.partial(_gdn_kernel, chunk=chunk, carry_state=carry_state, tiles_per_seq=tiles_per_seq),
        grid=(m // rws,),
        in_specs=[pl.BlockSpec((rws, GDN_CONV_CH), lambda i: (i, 0)),
                  pl.BlockSpec((rws, GDN_WIDTH), lambda i: (i, COL_GGATE // GDN_WIDTH)),
                  pl.BlockSpec((rws, 128), lambda i: (i, COL_SMALL // 128)),
                  const((1, 512)), const((1, 512)), const((1, GDN_WIDTH)),
                  const((128, 512)), const((128, 512)), const((rws, rws)), const((rws, rws)),
                  const((GDN_WIDTH, GDN_WIDTH)),
                  pl.BlockSpec(s_blk, s_idx)],
        out_specs=[pl.BlockSpec((rws, GDN_WIDTH), lambda i: (i, 0)), pl.BlockSpec(s_blk, s_idx)],
        out_shape=[jax.ShapeDtypeStruct((m, GDN_WIDTH), F32), jax.ShapeDtypeStruct(s0.shape, F32)],
        scratch_shapes=[pltpu.VMEM((GDN_HEADS, GDN_DK, GDN_DV), F32), pltpu.VMEM((rws, GDN_WIDTH), F32)],
        compiler_params=pltpu.CompilerParams(dimension_semantics=("arbitrary",), vmem_limit_bytes=VMEM_LIMIT_BYTES),
        name="gated_delta",
    )(qkv_act, proj, proj, alog_x, dt_x, gain_x, ea, eb, tril, cones, hones, s0)


COL_Q = 0
COL_KV = 512
COL_AB = 1024
COL_AC = 1280
COL_GQKV = 1536
COL_AH = 2304
COL_GGATE = 2560
COL_WIN = 2816
COL_SMALL = 3072
N_PROJ = 3200
SM_GA, SM_GB, SM_NG = 0, 4, 8


def permute_w_in(w_in):
    offs = np.concatenate([[0], np.cumsum(IN_SIZES)])
    grp = lambda k: w_in[..., offs[k]:offs[k + 1]]
    pad = jnp.zeros(w_in.shape[:-1] + (N_PROJ - COL_SMALL - 32,), w_in.dtype)
    order = [grp(9), grp(10), grp(0), grp(1), grp(3), grp(4), grp(5), grp(2), grp(6), grp(11),
             grp(7), grp(8), grp(12), pad]
    return jnp.concatenate(order, axis=-1)


NEG = -1e30
KEY_CHUNK = 128
SEL_CHUNK = 256


def _block_diag4(a, b):
    z = jnp.zeros_like(a)
    rows = [jnp.concatenate([a, z, z, z], -1), jnp.concatenate([z, a, z, z], -1),
            jnp.concatenate([z, z, b, z], -1), jnp.concatenate([z, z, z, b], -1)]
    return jnp.concatenate(rows, -2)


def compress_params(cmp_pe, cmp_w1, cmp_b1, cmp_w2):
    w1 = _block_diag4(cmp_w1[0], cmp_w1[1]).astype(BF16)
    w2 = _block_diag4(cmp_w2[0], cmp_w2[1]).astype(BF16)
    pe = jnp.concatenate([cmp_pe[0], cmp_pe[0], cmp_pe[1], cmp_pe[1]], -1)
    b1 = jnp.concatenate([cmp_b1[0], cmp_b1[0], cmp_b1[1], cmp_b1[1]], -1).reshape(1, 256)
    return w1, pe, b1, w2


def _compress_rows(row_loader, ncp, w1_ref, pe_ref, b1_ref, w2_ref):
    acc_lo = jnp.zeros((ncp, 256), F32)
    acc_hi = jnp.zeros((ncp, 256), F32)
    for l in range(CMP_STRIDE):
        x = row_loader(l)
        acc_lo += jnp.dot((x + pe_ref[l:l + 1, :]).astype(BF16), w1_ref[l], preferred_element_type=F32)
        acc_hi += jnp.dot((x + pe_ref[l + CMP_STRIDE:l + CMP_STRIDE + 1, :]).astype(BF16),
                          w1_ref[l + CMP_STRIDE], preferred_element_type=F32)
    hid = jax.nn.gelu(acc_lo + pltpu.roll(acc_hi, ncp - 1, axis=0) + b1_ref[...])
    return jnp.dot(hid.astype(BF16), w2_ref[...], preferred_element_type=F32)


def _cmp_attend(q_rows, kc, vc, pos, ovl, n_heads, tq, ncp):
    s = lax.dot_general(q_rows, kc, (((1,), (1,)), ((), ())), preferred_element_type=F32)
    cend = lax.broadcasted_iota(jnp.int32, (tq, ncp), 1) * CMP_STRIDE + (CMP_BLOCK - 1)
    s3 = jnp.where((cend <= pos)[None], s.reshape(n_heads, tq, ncp), -jnp.inf)
    m = jnp.max(s3, axis=-1, keepdims=True)
    m = jnp.where(m == -jnp.inf, 0.0, m)
    p = jnp.exp(s3 - m)
    pn = p / jnp.maximum(jnp.sum(p, axis=-1, keepdims=True), 1e-30)
    o_cmp = jnp.dot(pn.reshape(n_heads * tq, ncp).astype(BF16), vc, preferred_element_type=F32)
    psum = pn[0]
    for g in range(1, n_heads):
        psum = psum + pn[g]
    p_hi = psum.astype(BF16)
    p_lo = (psum - p_hi.astype(F32)).astype(BF16)
    imp = (jnp.dot(p_hi, ovl, preferred_element_type=F32) + jnp.dot(p_lo, ovl, preferred_element_type=F32))
    return o_cmp, imp


def _select_topn(imp, pos):
    r, ns = imp.shape
    blk = lax.broadcasted_iota(jnp.int32, (r, ns), 1)
    blk_f = blk.astype(F32)
    forced = (blk == 0) | (blk == lax.shift_right_logical(pos, 6))
    valid = blk * SEL_BLOCK <= pos
    v = jnp.where(forced, jnp.inf, jnp.where(valid, imp, -jnp.inf))
    sel = jnp.zeros((r, ns), F32)
    for _ in range(min(SEL_TOPN, ns)):
        mx = jnp.max(v, axis=-1, keepdims=True)
        idx = jnp.min(jnp.where(v == mx, blk_f, float(ns)), axis=-1, keepdims=True)
        hit = blk_f == idx
        sel = jnp.where(hit, 1.0, sel)
        v = jnp.where(hit, -jnp.inf, v)
    return sel


def _nsa_prompt_kernel(q_ref, cmp_ref, slc_ref, win_ref, sm_ref, w1_ref, pe_ref, b1_ref, w2_ref, ovl_ref, exp_ref,
                       gexp_ref, y_ref, kc_s, vc_s, ks_s, vs_s, kw_s, vw_s, q_s, sel_s, ocmp_s, owin_s, m_s, acc_s, *, t_len):
    tq = KEY_CHUNK
    i = pl.program_id(1)
    ncp = t_len // CMP_STRIDE
    ns = t_len // SEL_BLOCK
    ng = NSA_GROUP
    rows = ng * tq

    @pl.when(i == 0)
    def _prepare_sequence():
        kvc = _compress_rows(lambda l: cmp_ref[0, l], ncp, w1_ref, pe_ref, b1_ref, w2_ref)
        kc_s[...] = kvc[:, 0:128].astype(BF16)
        vc_s[...] = kvc[:, 128:256].astype(BF16)
        ones = jnp.ones((512, 64), BF16)

        def cast_rows(r, carry):
            sl = pl.ds(pl.multiple_of(r * 512, 512), 512)
            for h in range(NSA_KV_HEADS):
                ks_s[h, sl, :] = slc_ref[sl, 64 * h:64 * h + 64].astype(BF16)
                vs_s[h, sl, 0:64] = slc_ref[sl, 128 + 64 * h:192 + 64 * h].astype(BF16)
                vs_s[h, sl, 64:128] = ones
                kw_s[h, sl, :] = win_ref[sl, 64 * h:64 * h + 64].astype(BF16)
                vw_s[h, sl, 0:64] = win_ref[sl, 128 + 64 * h:192 + 64 * h].astype(BF16)
                vw_s[h, sl, 64:128] = ones
            return carry

        lax.fori_loop(0, t_len // 512, cast_rows, 0)

    pos = i * tq + lax.broadcasted_iota(jnp.int32, (tq, 1), 0)
    gates = jax.nn.sigmoid(sm_ref[...])
    g_hi = gates.astype(BF16)
    gx = _mm(g_hi, gexp_ref[...]) + _mm((gates - g_hi.astype(F32)).astype(BF16), gexp_ref[...])
    nt = (((1,), (1,)), ((), ()))

    wk = min(WINDOW + KEY_CHUNK, t_len)
    w0 = pl.multiple_of(jnp.minimum(jnp.maximum(i - WINDOW // KEY_CHUNK, 0), (t_len - wk) // KEY_CHUNK) * KEY_CHUNK,
                        KEY_CHUNK)
    wpos = w0 + lax.broadcasted_iota(jnp.int32, (tq, wk), 1)
    wbias = jnp.where(wpos <= pos, jnp.where(wpos > pos - WINDOW, 0.0, NEG), NEG)

    hs = range(NSA_KV_HEADS)
    q_rows = []
    for h in hs:
        qh = q_ref[:, 256 * h:256 * h + 256] * (NSA_HD ** -0.5)
        q_rows.append(jnp.concatenate([qh[:, 64 * g:64 * g + 64] for g in range(ng)], axis=0).astype(BF16))
        q_s[h] = q_rows[h]
    cmp = [_cmp_attend(q_rows[h], kc_s[:, 64 * h:64 * h + 64], vc_s[:, 64 * h:64 * h + 64], pos, ovl_ref[...],
                       ng, tq, ncp) for h in hs]
    sw = [lax.dot_general(q_rows[h], kw_s[h, pl.ds(w0, wk), :], nt, preferred_element_type=F32) for h in hs]
    sw = [(s.reshape(ng, tq, wk) + wbias[None]).reshape(rows, wk) for s in sw]
    pw = [jnp.exp(s - jnp.max(s, axis=-1, keepdims=True)).astype(BF16) for s in sw]
    aw = [jnp.dot(pw[h], vw_s[h, pl.ds(w0, wk), :], preferred_element_type=F32) for h in hs]
    sel = _select_topn(jnp.concatenate([c[1] for c in cmp], axis=0), jnp.concatenate([pos] * NSA_KV_HEADS, axis=0))
    for h in hs:
        ocmp_s[h] = cmp[h][0]
        sel_s[h] = sel[h * tq:(h + 1) * tq].astype(BF16)
        owin_s[h] = aw[h][:, 0:64] / aw[h][:, 64:128]
        m_s[h] = jnp.full((rows, 128), NEG, F32)
        acc_s[h] = jnp.zeros((rows, 128), F32)

    lane = lax.broadcasted_iota(jnp.int32, (tq, SEL_CHUNK), 1)

    def sel_body(j, carry):
        off = pl.multiple_of(j * SEL_CHUNK, SEL_CHUNK)
        hs = range(NSA_KV_HEADS)
        pairs = [(h, slice(g * tq, (g + 1) * tq)) for h in hs for g in range(ng)]
        ks = [ks_s[h, pl.ds(off, SEL_CHUNK), :] for h in hs]
        vs = [vs_s[h, pl.ds(off, SEL_CHUNK), :] for h in hs]
        causal = off + lane <= pos
        biases = [jnp.where(causal, (jnp.dot(sel_s[h], exp_ref[:, pl.ds(off, SEL_CHUNK)],
                                            preferred_element_type=F32) - 1.0) * (-NEG), NEG) for h in hs]
        ss = [lax.dot_general(q_s[h, r, :], ks[h], nt, preferred_element_type=F32) + biases[h] for h, r in pairs]
        m_prevs = [m_s[h, r, :] for h, r in pairs]
        m_news = [jnp.maximum(mp, jnp.max(s, axis=-1, keepdims=True)) for mp, s in zip(m_prevs, ss)]
        ps = [jnp.exp(s - jnp.concatenate([mn] * (SEL_CHUNK // 128), axis=1)).astype(BF16)
              for s, mn in zip(ss, m_news)]
        pvs = [jnp.dot(p, vs[h], preferred_element_type=F32) for p, (h, r) in zip(ps, pairs)]
        for (h, r), mp, mn, pv in zip(pairs, m_prevs, m_news, pvs):
            acc_s[h, r, :] = jnp.exp(mp - mn) * acc_s[h, r, :] + pv
            m_s[h, r, :] = mn
        return carry

    lax.fori_loop(0, (i * tq) // SEL_CHUNK + 1, sel_body, 0)

    for h in range(NSA_KV_HEADS):
        acc = acc_s[h]
        o_slc = acc[:, 0:64] / acc[:, 64:128]
        o_cmp = ocmp_s[h]
        o_win = owin_s[h]
        for g in range(ng):
            hh = ng * h + g
            r = slice(g * tq, (g + 1) * tq)
            gate = lambda branch: gx[:, 128 * (NSA_HEADS * branch + hh):128 * (NSA_HEADS * branch + hh) + 64]
            y_ref[:, 64 * hh:64 * hh + 64] = gate(0) * o_cmp[r] + gate(1) * o_slc[r] + gate(2) * o_win[r]


def nsa_prompt(proj, n_seq, t_len, cparams):
    tq = KEY_CHUNK
    nt = t_len // tq
    ncp = t_len // CMP_STRIDE
    ns = t_len // SEL_BLOCK
    w1, pe, b1, w2 = cparams
    cstart = np.arange(ncp)[:, None] * CMP_STRIDE
    sstart = np.arange(ns)[None, :] * SEL_BLOCK
    ovl = jnp.asarray((cstart < sstart + SEL_BLOCK) & (cstart + CMP_BLOCK > sstart), BF16)
    expand = jnp.asarray(np.arange(t_len)[None, :] // SEL_BLOCK == np.arange(ns)[:, None], BF16)
    gexp = jnp.asarray(np.arange(128)[:, None] == SM_NG + np.arange(3 * NSA_HEADS * 128)[None, :] // 128, BF16)
    cmp_rows = proj[:, COL_KV:COL_KV + 256].reshape(n_seq, ncp, CMP_STRIDE, 256).transpose(0, 2, 1, 3)
    once = pl.Buffered(1)
    const = lambda shape: pl.BlockSpec(shape, lambda b, i: (0,) * len(shape), pipeline_mode=once)
    return pl.pallas_call(
        functools.partial(_nsa_prompt_kernel, t_len=t_len),
        grid=(n_seq, nt),
        in_specs=[pl.BlockSpec((tq, 512), lambda b, i: (b * nt + i, COL_Q // 512)),
                  pl.BlockSpec((1, CMP_STRIDE, ncp, 256), lambda b, i: (b, 0, 0, 0), pipeline_mode=once),
                  pl.BlockSpec((t_len, 256), lambda b, i: (b, (COL_KV + 256) // 256), pipeline_mode=once),
                  pl.BlockSpec((t_len, 256), lambda b, i: (b, COL_WIN // 256), pipeline_mode=once),
                  pl.BlockSpec((tq, 128), lambda b, i: (b * nt + i, COL_SMALL // 128)),
                  const((CMP_BLOCK, 256, 256)), const((CMP_BLOCK, 256)), const((1, 256)), const((256, 256)),
                  const((ncp, ns)), const((ns, t_len)), const((128, 3 * NSA_HEADS * 128))],
        out_specs=pl.BlockSpec((tq, NSA_WIDTH), lambda b, i: (b * nt + i, 0)),
        out_shape=jax.ShapeDtypeStruct((n_seq * t_len, NSA_WIDTH), F32),
        scratch_shapes=[pltpu.VMEM((ncp, 128), BF16), pltpu.VMEM((ncp, 128), BF16),
                        pltpu.VMEM((NSA_KV_HEADS, t_len, 64), BF16), pltpu.VMEM((NSA_KV_HEADS, t_len, 128), BF16),
                        pltpu.VMEM((NSA_KV_HEADS, t_len, 64), BF16), pltpu.VMEM((NSA_KV_HEADS, t_len, 128), BF16),
                        pltpu.VMEM((NSA_KV_HEADS, NSA_GROUP * tq, 64), BF16), pltpu.VMEM((NSA_KV_HEADS, tq, ns), BF16),
                        pltpu.VMEM((NSA_KV_HEADS, NSA_GROUP * tq, 64), F32),
                        pltpu.VMEM((NSA_KV_HEADS, NSA_GROUP * tq, 64), F32),
                        pltpu.VMEM((NSA_KV_HEADS, NSA_GROUP * tq, 128), F32),
                        pltpu.VMEM((NSA_KV_HEADS, NSA_GROUP * tq, 128), F32)],
        compiler_params=pltpu.CompilerParams(dimension_semantics=("arbitrary", "arbitrary"),
                                             vmem_limit_bytes=VMEM_LIMIT_BYTES),
        name="nsa_prompt",
    )(proj, cmp_rows, proj, proj, proj, w1, pe, b1, w2, ovl, expand, gexp)


PAGE_ROWS = 128
SEL_LANES = 128


def _softmax_segments(segs):
    m = segs[0].max(axis=-1, keepdims=True)
    for s in segs[1:]:
        m = jnp.maximum(m, s.max(axis=-1, keepdims=True))
    ps = [jnp.exp(s - m) for s in segs]
    den = ps[0].sum(axis=-1, keepdims=True)
    for p in ps[1:]:
        den = den + p.sum(axis=-1, keepdims=True)
    return ps, den


def _nsa_sample_kernel(pt_ref, *refs, n_pages, past_len, dec_t):
    del pt_ref
    pages = refs[:n_pages]
    (q_ref, kvn_ref, wn_ref, sm_ref, wc_ref, w1_ref, pe_ref, b1_ref, w2_ref, ovl_ref, exp_ref,
     y_ref, wst_ref, rk_s, rv_s, newpg_s, neww_s) = refs[n_pages:]
    ncp = past_len // CMP_STRIDE
    ng, nh = NSA_GROUP, NSA_KV_HEADS
    rows_h = ng * dec_t
    rows = nh * rows_h
    nt = (((1,), (1,)), ((), ()))

    for p in range(n_pages):
        rk_s[p * PAGE_ROWS:(p + 1) * PAGE_ROWS, :] = pages[p][0:128, :].T
        rv_s[p * PAGE_ROWS:(p + 1) * PAGE_ROWS, :] = pages[p][128:256, :].T
    kvc = _compress_rows(
        lambda l: jnp.concatenate([rk_s[pl.ds(l, ncp, stride=CMP_STRIDE), :],
                                   rv_s[pl.ds(l, ncp, stride=CMP_STRIDE), :]], axis=1),
        ncp, w1_ref, pe_ref, b1_ref, w2_ref)
    kc = kvc[:, 0:128].astype(BF16)
    vc = kvc[:, 128:256].astype(BF16)

    newpg_s[...] = jnp.zeros((PAGE_ROWS, 256), F32)
    newpg_s[0:dec_t, :] = kvn_ref[0][:, 256:512]
    neww_s[...] = jnp.zeros((PAGE_ROWS, 256), F32)
    neww_s[0:dec_t, :] = wn_ref[0]

    qf = q_ref[0] * (NSA_HD ** -0.5)
    zero = jnp.zeros((dec_t, 64), F32)
    qrows = []
    for h in range(nh):
        for g in range(ng):
            piece = qf[:, 64 * (ng * h + g):64 * (ng * h + g) + 64]
            qrows.append(jnp.concatenate([piece, zero] if h == 0 else [zero, piece], axis=1))
    q_bd = jnp.concatenate(qrows, axis=0).astype(BF16)
    t_row = lax.broadcasted_iota(jnp.int32, (rows, 1), 0) & (dec_t - 1)
    pos = past_len + t_row
    head0 = lax.broadcasted_iota(jnp.int32, (rows, 128), 0) < rows_h
    lane_lo = lax.broadcasted_iota(jnp.int32, (rows, 128), 1) < 64
    own = head0 == lane_lo

    def own_half(x):
        x = jnp.where(own, x, 0.0)
        return x[:, 0:64] + x[:, 64:128]

    s = lax.dot_general(q_bd, kc, nt, preferred_element_type=F32)
    cend = lax.broadcasted_iota(jnp.int32, (rows, ncp), 1) * CMP_STRIDE + (CMP_BLOCK - 1)
    s = jnp.where(cend <= pos, s, -jnp.inf)
    m = jnp.max(s, axis=-1, keepdims=True)
    m = jnp.where(m == -jnp.inf, 0.0, m)
    p = jnp.exp(s - m)
    pn = p / jnp.maximum(jnp.sum(p, axis=-1, keepdims=True), 1e-30)
    o_cmp = own_half(_mm(pn.astype(BF16), vc))

    ovl = ovl_ref[...]
    psums = []
    for h in range(nh):
        psum = pn[h * rows_h:h * rows_h + dec_t]
        for g in range(1, ng):
            psum = psum + pn[h * rows_h + g * dec_t:h * rows_h + (g + 1) * dec_t]
        psums.append(psum)
    psum = jnp.concatenate(psums, axis=0)
    p_hi = psum.astype(BF16)
    imp = _mm(p_hi, ovl) + _mm((psum - p_hi.astype(F32)).astype(BF16), ovl)
    pos_ht = past_len + (lax.broadcasted_iota(jnp.int32, (nh * dec_t, 1), 0) & (dec_t - 1))
    sel = _select_topn(imp, pos_ht)
    sel_b = jnp.concatenate([sel[h * dec_t:(h + 1) * dec_t] for h in range(nh) for _ in range(ng)],
                            axis=0).astype(BF16)

    lane = lax.broadcasted_iota(jnp.int32, (rows, PAGE_ROWS), 1)
    segs = []
    for pg in range(n_pages + 1):
        if pg < n_pages:
            sc = _mm(q_bd, pages[pg][256:384, :].astype(BF16))
        else:
            sc = lax.dot_general(q_bd, newpg_s[:, 0:128].astype(BF16), nt, preferred_element_type=F32)
        chosen = _mm(sel_b, exp_ref[:, pg * PAGE_ROWS:(pg + 1) * PAGE_ROWS])
        ok = chosen > 0.5
        if pg == n_pages:
            ok = ok & (past_len + lane <= pos)
        segs.append(jnp.where(ok, sc, NEG))
    ps, den = _softmax_segments(segs)
    acc = jnp.zeros((rows, 128), F32)
    for pg in range(n_pages + 1):
        if pg < n_pages:
            acc = acc + lax.dot_general(ps[pg].astype(BF16), pages[pg][384:512, :].astype(BF16), nt,
                                        preferred_element_type=F32)
        else:
            acc = acc + _mm(ps[pg].astype(BF16), newpg_s[:, 128:256].astype(BF16))
    o_slc = own_half(acc) / den

    wlen = wc_ref.shape[2]
    wc = wc_ref[0]
    jw = lax.broadcasted_iota(jnp.int32, (rows, wlen), 1)
    s_old = _mm(q_bd, wc[0:128, :].astype(BF16))
    s_old = jnp.where(past_len - wlen + jw > pos - WINDOW, s_old, NEG)
    s_new = lax.dot_general(q_bd, neww_s[:, 0:128].astype(BF16), nt, preferred_element_type=F32)
    s_new = jnp.where(past_len + lane <= pos, s_new, NEG)
    ps, den = _softmax_segments([s_old, s_new])
    acc = (lax.dot_general(ps[0].astype(BF16), wc[128:256, :].astype(BF16), nt, preferred_element_type=F32)
           + _mm(ps[1].astype(BF16), neww_s[:, 128:256].astype(BF16)))
    o_win = own_half(acc) / den

    gates = jax.nn.sigmoid(sm_ref[0][:, SM_NG:SM_NG + 3 * NSA_HEADS])
    for hh in range(NSA_HEADS):
        r = slice(hh * dec_t, (hh + 1) * dec_t)
        y_ref[0, :, 64 * hh:64 * hh + 64] = (gates[:, hh:hh + 1] * o_cmp[r]
                                             + gates[:, NSA_HEADS + hh:NSA_HEADS + hh + 1] * o_slc[r]
                                             + gates[:, 2 * NSA_HEADS + hh:2 * NSA_HEADS + hh + 1] * o_win[r])

    rolled = pltpu.roll(wc, wlen - dec_t, axis=1)
    new_t = jnp.concatenate([neww_s[:, 0:128].T, neww_s[:, 128:256].T], axis=0)
    lane_w = lax.broadcasted_iota(jnp.int32, (256, 128), 1)
    last = jnp.where(lane_w >= 128 - dec_t, pltpu.roll(new_t, 128 - dec_t, axis=1), rolled[:, wlen - 128:wlen])
    wst_ref[0, :, 0:wlen - 128] = rolled[:, 0:wlen - 128]
    wst_ref[0, :, wlen - 128:wlen] = last


def nsa_sample(q, kv_new, win_new, sm, cache_kv, layer_idx, page_table, win_cache, cparams):
    bsz, dec_t, _ = q.shape
    n_pages = page_table.shape[1]
    past_len = n_pages * PAGE_ROWS
    wlen = win_cache.shape[2]
    assert dec_t < CMP_STRIDE and dec_t % SUBLANES == 0 and dec_t & (dec_t - 1) == 0
    assert wlen == WINDOW and wlen <= past_len
    ncp = past_len // CMP_STRIDE
    ns = -(-(past_len + dec_t) // SEL_BLOCK)
    assert ns <= SEL_LANES
    w1, pe, b1, w2 = cparams
    cstart = np.arange(ncp)[:, None] * CMP_STRIDE
    sstart = np.arange(SEL_LANES)[None, :] * SEL_BLOCK
    ovl = jnp.asarray((cstart < sstart + SEL_BLOCK) & (cstart + CMP_BLOCK > sstart) & (np.arange(SEL_LANES)[None, :] < ns), BF16)
    keys = np.arange((n_pages + 1) * PAGE_ROWS)
    expand = jnp.asarray(keys[None, :] // SEL_BLOCK == np.arange(SEL_LANES)[:, None], BF16)
    const = lambda shape: pl.BlockSpec(shape, lambda b, pt: (0,) * len(shape))
    seq = lambda r, c: pl.BlockSpec((1, r, c), lambda b, pt: (b, 0, 0))
    page_specs = [pl.BlockSpec((None, None, 512, PAGE_ROWS), functools.partial(
        lambda b, pt, p: (layer_idx, pt[b, p], 0, 0), p=p)) for p in range(n_pages)]
    grid_spec = pltpu.PrefetchScalarGridSpec(
        num_scalar_prefetch=1,
        grid=(bsz,),
        in_specs=page_specs + [seq(dec_t, 512), seq(dec_t, 512), seq(dec_t, 256), seq(dec_t, 128), seq(256, wlen),
                               const((CMP_BLOCK, 256, 256)), const((CMP_BLOCK, 256)), const((1, 256)),
                               const((256, 256)), const((ncp, SEL_LANES)),
                               const((SEL_LANES, (n_pages + 1) * PAGE_ROWS))],
        out_specs=[seq(dec_t, NSA_WIDTH), seq(256, wlen)],
        scratch_shapes=[pltpu.VMEM((past_len, 128), F32), pltpu.VMEM((past_len, 128), F32),
                        pltpu.VMEM((PAGE_ROWS, 256), F32), pltpu.VMEM((PAGE_ROWS, 256), F32)])
    return pl.pallas_call(
        functools.partial(_nsa_sample_kernel, n_pages=n_pages, past_len=past_len, dec_t=dec_t),
        grid_spec=grid_spec,
        out_shape=[jax.ShapeDtypeStruct((bsz, dec_t, NSA_WIDTH), F32), jax.ShapeDtypeStruct((bsz, 256, wlen), F32)],
        compiler_params=pltpu.CompilerParams(dimension_semantics=("arbitrary",), vmem_limit_bytes=VMEM_LIMIT_BYTES),
        name="nsa_sample",
    )(page_table, *([cache_kv] * n_pages), q, kv_new, win_new, sm, win_cache, w1, pe, b1, w2, ovl, expand)


def l2_normalize(x):
    return x * lax.rsqrt(jnp.sum(jnp.square(x), axis=-1, keepdims=True) + 1e-6)


def gated_rms_norm(o, gain, gate):
    of = o.astype(F32)
    of = of * lax.rsqrt(jnp.mean(jnp.square(of), axis=-1, keepdims=True) + 1e-6)
    return (of * gain.astype(F32) * jax.nn.silu(gate.astype(F32))).astype(gate.dtype)


def masked_softmax(s, mask):
    s = jnp.where(mask, s.astype(F32), -jnp.inf)
    m = jnp.max(s, axis=-1, keepdims=True)
    m = jnp.where(jnp.isfinite(m), m, 0.0)
    p = jnp.exp(s - m)
    return p / jnp.maximum(jnp.sum(p, axis=-1, keepdims=True), 1e-30)


def causal_dwconv(x, buf, w):
    k = w.shape[0]
    t = x.shape[1]
    xp = jnp.concatenate([buf.astype(x.dtype), x], axis=1)
    y = sum(xp[:, i:i + t] * w[i] for i in range(k))
    return y, xp[:, t:]


def split_in(proj):
    return jnp.split(proj, np.cumsum(IN_SIZES)[:-1].tolist(), axis=-1)


def gated_delta_rule(q, k, v, a_in, b_in, s0, a_log, dt_bias):
    b, t, h, dk = q.shape
    dv = v.shape[-1]
    q = l2_normalize(q.astype(F32)) * (dk ** -0.5)
    k = l2_normalize(k.astype(F32))
    v = v.astype(F32)
    g = -jnp.exp(a_log.astype(F32)) * jax.nn.softplus(a_in.astype(F32) + dt_bias.astype(F32))
    beta = jax.nn.sigmoid(b_in.astype(F32))
    c = GDN_CHUNK if t % GDN_CHUNK == 0 else t
    n = t // c

    def to_chunks(z):
        return jnp.moveaxis(z.reshape((b, n, c) + z.shape[2:]), 3, 2)

    qc, kc, vc, gc, bc = (to_chunks(z) for z in (q, k, v, g, beta))
    gc = jnp.cumsum(gc, axis=-1)
    incl = jnp.tril(jnp.ones((c, c), dtype=bool))
    strict = jnp.tril(jnp.ones((c, c), dtype=bool), -1)
    diff = gc[..., :, None] - gc[..., None, :]
    decay = jnp.where(incl, jnp.exp(jnp.where(incl, diff, 0.0)), 0.0)
    kb = kc * bc[..., None]
    m = jnp.eye(c, dtype=F32) + jnp.where(strict, jnp.einsum('bnhik,bnhjk->bnhij', kb, kc) * decay, 0.0)
    rhs = jnp.concatenate([vc * bc[..., None], kb * jnp.exp(gc)[..., None]], axis=-1)
    sol = lax.linalg.triangular_solve(m, rhs, left_side=True, lower=True, unit_diagonal=True)
    u, w = sol[..., :dv], sol[..., dv:]
    a_qk = jnp.einsum('bnhik,bnhjk->bnhij', qc, kc) * decay
    q_dec = qc * jnp.exp(gc)[..., None]
    k_dec = kc * jnp.exp(gc[..., -1:] - gc)[..., None]
    g_last = jnp.exp(gc[..., -1])

    def step(state, xs):
        u_n, w_n, q_n, k_n, a_n, gl_n = xs
        v_new = u_n - jnp.einsum('bhik,bhkv->bhiv', w_n, state)
        o_n = jnp.einsum('bhik,bhkv->bhiv', q_n, state) + jnp.einsum('bhij,bhjv->bhiv', a_n, v_new)
        state = state * gl_n[..., None, None] + jnp.einsum('bhik,bhiv->bhkv', k_n, v_new)
        return state, o_n

    xs = tuple(jnp.moveaxis(z, 1, 0) for z in (u, w, q_dec, k_dec, a_qk, g_last))
    s_final, o = lax.scan(step, s0.astype(F32), xs)
    o = jnp.moveaxis(jnp.moveaxis(o, 0, 1), 2, 3).reshape(b, t, h, dv)
    return o, s_final


def compress_blocks(rows, pe, w1, b1, w2):
    b, t_pad, kvh, hd = rows.shape
    r = rows.reshape(b, t_pad // CMP_STRIDE, CMP_STRIDE, kvh, hd)
    blocks = jnp.concatenate([r[:, :-1], r[:, 1:]], axis=2) + pe[:, None, :]
    hid = jax.nn.gelu(jnp.einsum('bclhd,lde->bche', blocks, w1) + b1)
    return jnp.einsum('bche,ed->bchd', hid, w2)


def select_attend(qg, idx, q_pos, ks, vs):
    b, kvh = ks.shape[:2]
    tq = qg.shape[1]
    n = idx.shape[-1]
    b_ix = jnp.arange(b)[:, None, None, None]
    h_ix = jnp.arange(kvh)[None, :, None, None]
    kg = ks[b_ix, h_ix, idx].reshape(b, kvh, tq, n * SEL_BLOCK, NSA_HD)
    vg = vs[b_ix, h_ix, idx].reshape(b, kvh, tq, n * SEL_BLOCK, NSA_HD)
    k_pos = (idx[..., None] * SEL_BLOCK + jnp.arange(SEL_BLOCK)).reshape(b, kvh, 1, tq, n * SEL_BLOCK)
    s = jnp.einsum('bqhgd,bhqkd->bhgqk', qg, kg) * (NSA_HD ** -0.5)
    p = masked_softmax(s, k_pos <= q_pos[:, None])
    return jnp.einsum('bhgqk,bhqkd->bqhgd', p.astype(vg.dtype), vg)


def nsa_compressed_selected(qg, kv, q_pos, cmp_pe, cmp_w1, cmp_b1, cmp_w2):
    b, t = kv.shape[:2]
    tq = qg.shape[1]
    t_pad = -(-t // SEL_BLOCK) * SEL_BLOCK
    kv = jnp.pad(kv, ((0, 0), (0, t_pad - t), (0, 0), (0, 0), (0, 0)))
    kc = compress_blocks(kv[:, :, 0], cmp_pe[0], cmp_w1[0], cmp_b1[0], cmp_w2[0])
    vc = compress_blocks(kv[:, :, 1], cmp_pe[1], cmp_w1[1], cmp_b1[1], cmp_w2[1])
    nc = kc.shape[1]
    cmp_start = jnp.arange(nc) * CMP_STRIDE
    s = jnp.einsum('bqhgd,bchd->bhgqc', qg, kc) * (NSA_HD ** -0.5)
    p = masked_softmax(s, cmp_start[None, :] + (CMP_BLOCK - 1) <= q_pos[:, None])
    o_cmp = jnp.einsum('bhgqc,bchd->bqhgd', p.astype(vc.dtype), vc)
    ns = t_pad // SEL_BLOCK
    sel_start = jnp.arange(ns) * SEL_BLOCK
    overlap = ((cmp_start[:, None] < sel_start[None, :] + SEL_BLOCK)
               & (cmp_start[:, None] + CMP_BLOCK > sel_start[None, :])).astype(F32)
    imp = jnp.einsum('bhgqc,cn->bhqn', p, overlap)
    blk = jnp.arange(ns)[None, :]
    forced = (blk == 0) | (blk == q_pos[:, None] // SEL_BLOCK)
    valid = sel_start[None, :] <= q_pos[:, None]
    imp = jnp.where(forced, jnp.inf, jnp.where(valid, imp, -jnp.inf))
    n_top = min(SEL_TOPN, ns)
    _, idx = lax.top_k(imp, n_top)
    ks = jnp.moveaxis(kv[:, :, 2].reshape(b, ns, SEL_BLOCK, NSA_KV_HEADS, NSA_HD), 3, 1)
    vs = jnp.moveaxis(kv[:, :, 3].reshape(b, ns, SEL_BLOCK, NSA_KV_HEADS, NSA_HD), 3, 1)
    if tq % Q_BLOCK == 0:
        nb = tq // Q_BLOCK
        qb = jnp.moveaxis(qg.reshape(b, nb, Q_BLOCK, NSA_KV_HEADS, NSA_GROUP, NSA_HD), 1, 0)
        ib = jnp.moveaxis(idx.reshape(b, NSA_KV_HEADS, nb, Q_BLOCK, n_top), 2, 0)
        pb = q_pos.reshape(nb, Q_BLOCK)
        ob = lax.map(lambda a: select_attend(a[0], a[1], a[2], ks, vs), (qb, ib, pb))
        o_slc = jnp.moveaxis(ob, 0, 1).reshape(b, tq, NSA_KV_HEADS, NSA_GROUP, NSA_HD)
    else:
        o_slc = select_attend(qg, idx, q_pos, ks, vs)
    return o_cmp, o_slc


def window_banded(qg, kvw):
    b, t = kvw.shape[:2]
    nb = t // Q_BLOCK
    nw = WINDOW // Q_BLOCK
    kp = jnp.pad(kvw, ((0, 0), (WINDOW, 0), (0, 0), (0, 0), (0, 0))).reshape(b, nb + nw, Q_BLOCK, 2, NSA_KV_HEADS, NSA_HD)
    band = jnp.concatenate([kp[:, i:i + nb] for i in range(nw + 1)], axis=2)
    qb = qg.reshape(b, nb, Q_BLOCK, NSA_KV_HEADS, NSA_GROUP, NSA_HD)
    start = jnp.arange(nb)[:, None] * Q_BLOCK
    qpos = start + jnp.arange(Q_BLOCK)
    kpos = start - WINDOW + jnp.arange((nw + 1) * Q_BLOCK)
    qp, kp_ = qpos[:, :, None], kpos[:, None, :]
    mask = (kp_ <= qp) & (kp_ > qp - WINDOW) & (kp_ >= 0)
    s = jnp.einsum('bnqhgd,bnkhd->bnhgqk', qb, band[:, :, :, 0]) * (NSA_HD ** -0.5)
    p = masked_softmax(s, mask[None, :, None, None])
    o = jnp.einsum('bnhgqk,bnkhd->bnqhgd', p.astype(band.dtype), band[:, :, :, 1])
    return o.reshape(b, t, NSA_KV_HEADS, NSA_GROUP, NSA_HD)


def window_dense(qg, kvw_all, q_pos, k_pos):
    s = jnp.einsum('bqhgd,bkhd->bhgqk', qg, kvw_all[:, :, 0]) * (NSA_HD ** -0.5)
    mask = (k_pos[None, :] <= q_pos[:, None]) & (k_pos[None, :] > q_pos[:, None] - WINDOW)
    p = masked_softmax(s, mask)
    return jnp.einsum('bhgqk,bkhd->bqhgd', p.astype(kvw_all.dtype), kvw_all[:, :, 1])


def layer(x, q_pos, past, w_in, conv_a_w, gdn_conv_w, gdn_a_log, gdn_dt_bias, gdn_norm_g,
          cmp_pe, cmp_w1, cmp_b1, cmp_w2, w_out, ln1_g, ln1_b, w_up, ffn_conv_w, w_down, ln2_g, ln2_b):
    raise NotImplementedError


TM = 512


def _to_tb(x, b, t):
    return x.reshape(b, t, -1).transpose(1, 0, 2).reshape(t * b, -1)


def _to_bt(x, b, t):
    return x.reshape(t, b, -1).transpose(1, 0, 2).reshape(b * t, -1)


def _blocked_buf(state):
    b, k1, c = state.shape
    return state.transpose(1, 0, 2).reshape(1, k1 * b, c)


def _unblocked(st, b):
    return st.reshape(-1, b, st.shape[-1]).transpose(1, 0, 2)


def layer_prompt(x, n_seq, t_len, lw):
    tiles = t_len // TM
    carry = ("carry", tiles)
    zeros = lambda c: jnp.zeros((n_seq, SUBLANES, c), F32)
    proj = dense(x, lw["w_in"], TM, N_PROJ, "in_proj")
    qkv, gconv = gdn_conv_silu(proj, lw["gdn_conv_w"], zeros(GDN_CONV_CH), TM, carry)
    y_b, s_new = gated_delta(qkv, proj, jnp.zeros((n_seq, GDN_HEADS, GDN_DK, GDN_DV), F32), lw["gdn_a_log"],
                             lw["gdn_dt_bias"], lw["gdn_norm_g"], GDN_CHUNK, True, t_len // GDN_ROWS)
    y_c = nsa_prompt(proj, n_seq, t_len, lw["cmp"])
    x1, conva = mix_out_ln(proj, y_b, y_c, x, lw["w_out"], lw["conv_a_w"], lw["ln1_g"], lw["ln1_b"],
                           zeros(A_WIDTH), TM, carry)
    h, ffnc = ffn_up_act(x1, lw["w_up"], lw["ffn_conv_w"], zeros(D_FF), TM, D_FF, carry)
    x2 = dense_res_ln(h, lw["w_down"], x1, lw["ln2_g"], lw["ln2_b"], TM, "ffn_down_ln")
    p3 = proj.reshape(n_seq, t_len, N_PROJ)
    kv_new = p3[:, :, COL_KV:COL_KV + 4 * NSA_KV_DIM].reshape(n_seq, t_len, 4, NSA_KV_HEADS, NSA_HD)
    wkeep = min(WINDOW, t_len)
    win_state = p3[:, t_len - wkeep:, COL_WIN:COL_WIN + 2 * NSA_KV_DIM].reshape(n_seq, wkeep, 2, NSA_KV_HEADS, NSA_HD)
    tail = lambda st, k: st[:, SUBLANES - (k - 1):, :]
    return x2, (kv_new, win_state, tail(conva, A_CONV), tail(gconv, GDN_CONV), s_new, tail(ffnc, FFN_CONV))


def layer_sample(x, bsz, dec_t, lw, layer_idx, cache_kv, page_table, win_cache, st_conv_a, st_gdn_conv, st_gdn,
                 st_ffn_conv):
    m = dec_t * bsz
    blocked = ("blocked", bsz)
    proj = dense(x, lw["w_in"], TM, N_PROJ, "in_proj")
    proj_bt = _to_bt(proj, bsz, dec_t)
    qkv, gconv = gdn_conv_silu(proj, lw["gdn_conv_w"], _blocked_buf(st_gdn_conv), m, blocked)
    y_b, s_new = gated_delta(_to_bt(qkv, bsz, dec_t), proj_bt, st_gdn, lw["gdn_a_log"], lw["gdn_dt_bias"],
                             lw["gdn_norm_g"], dec_t, False, 1)
    p3 = proj_bt.reshape(bsz, dec_t, N_PROJ)
    kv_new = p3[:, :, COL_KV:COL_KV + 4 * NSA_KV_DIM]
    y_c, win_state = nsa_sample(p3[:, :, COL_Q:COL_Q + NSA_WIDTH], kv_new,
                                p3[:, :, COL_WIN:COL_WIN + 2 * NSA_KV_DIM], p3[:, :, COL_SMALL:COL_SMALL + 128],
                                cache_kv, layer_idx, page_table, win_cache, lw["cmp"])
    x1, conva = mix_out_ln(proj, _to_tb(y_b, bsz, dec_t), _to_tb(y_c, bsz, dec_t), x, lw["w_out"], lw["conv_a_w"],
                           lw["ln1_g"], lw["ln1_b"], _blocked_buf(st_conv_a), m, blocked)
    h, ffnc = ffn_up_act(x1, lw["w_up"], lw["ffn_conv_w"], _blocked_buf(st_ffn_conv), m, D_FF // 2, blocked)
    x2 = dense_res_ln(h, lw["w_down"], x1, lw["ln2_g"], lw["ln2_b"], TM, "ffn_down_ln")
    wlen = win_cache.shape[2]
    return x2, (kv_new.reshape(bsz, dec_t, 4, NSA_KV_HEADS, NSA_HD),
                win_state.reshape(bsz, 2, NSA_KV_HEADS, NSA_HD, wlen).transpose(0, 4, 1, 2, 3),
                _unblocked(conva, bsz), _unblocked(gconv, bsz), s_new, _unblocked(ffnc, bsz))


def stack_layers(states, i):
    return jnp.stack([s[i] for s in states], axis=0)


def kernel(x_prompt, x_sample, cache_nsa_kv, cache_nsa_win, state_conv_a, state_gdn_conv, state_gdn, state_ffn_conv, page_table, ln_emb_g, ln_emb_b, w_in, conv_a_w, gdn_conv_w, gdn_a_log, gdn_dt_bias, gdn_norm_g, cmp_pe, cmp_w1, cmp_b1, cmp_w2, w_out, ln1_g, ln1_b, w_up, ffn_conv_w, w_down, ln2_g, ln2_b):
    n_seq, t_len = x_prompt.shape[:2]
    dec_b, dec_t = x_sample.shape[:2]
    depth = w_in.shape[0]
    xp = layer_norm_rows(x_prompt.reshape(-1, D_MODEL), ln_emb_g, ln_emb_b)
    xs = layer_norm_rows(_to_tb(x_sample.reshape(-1, D_MODEL), dec_b, dec_t), ln_emb_g, ln_emb_b)
    w_in_b = permute_w_in(w_in.astype(BF16))
    w_out_b, w_up_b, w_down_b = (w.astype(BF16) for w in (w_out, w_up, w_down))
    cache_kv = cache_nsa_kv.transpose(0, 1, 3, 4, 5, 2).reshape(cache_nsa_kv.shape[:2] + (4 * NSA_KV_DIM, -1))
    win_cache = cache_nsa_win.transpose(0, 1, 3, 4, 5, 2).reshape(cache_nsa_win.shape[:2] + (2 * NSA_KV_DIM, -1))
    st_p, st_s = [], []
    for l in range(depth):
        lw = dict(w_in=w_in_b[l], conv_a_w=conv_a_w[l], gdn_conv_w=gdn_conv_w[l], gdn_a_log=gdn_a_log[l],
                  gdn_dt_bias=gdn_dt_bias[l], gdn_norm_g=gdn_norm_g[l],
                  cmp=compress_params(cmp_pe[l], cmp_w1[l], cmp_b1[l], cmp_w2[l]),
                  w_out=w_out_b[l], ln1_g=ln1_g[l], ln1_b=ln1_b[l], w_up=w_up_b[l], ffn_conv_w=ffn_conv_w[l],
                  w_down=w_down_b[l], ln2_g=ln2_g[l], ln2_b=ln2_b[l])
        xp, sp = layer_prompt(xp, n_seq, t_len, lw)
        xs, ss = layer_sample(xs, dec_b, dec_t, lw, l, cache_kv, page_table, win_cache[l], state_conv_a[l],
                              state_gdn_conv[l], state_gdn[l], state_ffn_conv[l])
        st_p.append(sp)
        st_s.append(ss)
    xp = xp.reshape(n_seq, t_len, D_MODEL)
    xs = _to_bt(xs, dec_b, dec_t).reshape(dec_b, dec_t, D_MODEL)
    return (xp, xs,
            stack_layers(st_p, 0), stack_layers(st_s, 0),
            stack_layers(st_p, 1), stack_layers(st_s, 1),
            stack_layers(st_p, 2), stack_layers(st_s, 2),
            stack_layers(st_p, 3), stack_layers(st_s, 3),
            stack_layers(st_p, 4), stack_layers(st_s, 4),
            stack_layers(st_p, 5), stack_layers(st_s, 5))
```

```python
import functools
import math

import jax
import jax.numpy as jnp
import numpy as np
from jax import lax
from jax.experimental import pallas as pl
from jax.experimental.pallas import tpu as pltpu

F32 = jnp.float32
BF16 = jnp.bfloat16

D_MODEL = 1024
DEPTH = 4
HEAD_DIM = 64
A_WIDTH = 256
A_CONV = 3
GDN_WIDTH = 256
GDN_HEADS = 4
GDN_DK = 64
GDN_DV = 64
GDN_QK = 256
GDN_CONV = 4
GDN_CONV_CH = 768
GDN_CHUNK = 64
NSA_WIDTH = 512
NSA_HEADS = 8
NSA_KV_HEADS = 2
NSA_GROUP = 4
NSA_HD = 64
NSA_KV_DIM = 128
CMP_STRIDE = 16
CMP_BLOCK = 32
SEL_BLOCK = 64
SEL_TOPN = 8
WINDOW = 512
Q_BLOCK = 128
D_FF = 2816
FFN_CONV = 3
ALPHA = (2.0 * DEPTH) ** 0.25
LN_EPS = 1e-5
IN_SIZES = (A_WIDTH, A_WIDTH, A_WIDTH, GDN_QK, GDN_QK, GDN_WIDTH, GDN_WIDTH, GDN_HEADS, GDN_HEADS,
            NSA_WIDTH, 4 * NSA_KV_DIM, 2 * NSA_KV_DIM, 3 * NSA_HEADS)
N_IN = sum(IN_SIZES)

VMEM_LIMIT_BYTES = 56 * 1024 * 1024


def _ln_rows(z, g, b):
    mu = jnp.mean(z, axis=-1, keepdims=True)
    zc = z - mu
    var = jnp.mean(zc * zc, axis=-1, keepdims=True)
    return zc * lax.rsqrt(var + LN_EPS) * g + b


def _ln_kernel(x_ref, g_ref, b_ref, o_ref):
    o_ref[...] = _ln_rows(x_ref[...], g_ref[...], b_ref[...])


def layer_norm_rows(x, g, b, tm=512):
    m, d = x.shape
    return pl.pallas_call(
        _ln_kernel,
        grid=(m // tm,),
        in_specs=[pl.BlockSpec((tm, d), lambda i: (i, 0)),
                  pl.BlockSpec((1, d), lambda i: (0, 0)),
                  pl.BlockSpec((1, d), lambda i: (0, 0))],
        out_specs=pl.BlockSpec((tm, d), lambda i: (i, 0)),
        out_shape=jax.ShapeDtypeStruct((m, d), F32),
        name="ln_rows",
    )(x, g.reshape(1, d), b.reshape(1, d))


def _dense_kernel(x_ref, w_ref, o_ref):
    o_ref[...] = jnp.dot(x_ref[...].astype(BF16), w_ref[...], preferred_element_type=F32)


def dense(x, w, tm, tn, name):
    m, k = x.shape
    n = w.shape[1]
    return pl.pallas_call(
        _dense_kernel,
        grid=(n // tn, m // tm),
        in_specs=[pl.BlockSpec((tm, k), lambda j, i: (i, 0)),
                  pl.BlockSpec((k, tn), lambda j, i: (0, j))],
        out_specs=pl.BlockSpec((tm, tn), lambda j, i: (i, j)),
        out_shape=jax.ShapeDtypeStruct((m, n), F32),
        compiler_params=pltpu.CompilerParams(vmem_limit_bytes=VMEM_LIMIT_BYTES),
        name=name,
    )(x, w)


def _dense_res_ln_kernel(y_ref, w_ref, x_ref, g_ref, b_ref, o_ref):
    acc = jnp.dot(y_ref[...].astype(BF16), w_ref[...], preferred_element_type=F32)
    o_ref[...] = _ln_rows(ALPHA * x_ref[...] + acc, g_ref[...], b_ref[...])


def dense_res_ln(y, w, x, g, b, tm, name):
    m, k = y.shape
    d = w.shape[1]
    return pl.pallas_call(
        _dense_res_ln_kernel,
        grid=(m // tm,),
        in_specs=[pl.BlockSpec((tm, k), lambda i: (i, 0)),
                  pl.BlockSpec((k, d), lambda i: (0, 0)),
                  pl.BlockSpec((tm, d), lambda i: (i, 0)),
                  pl.BlockSpec((1, d), lambda i: (0, 0)),
                  pl.BlockSpec((1, d), lambda i: (0, 0))],
        out_specs=pl.BlockSpec((tm, d), lambda i: (i, 0)),
        out_shape=jax.ShapeDtypeStruct((m, d), F32),
        compiler_params=pltpu.CompilerParams(vmem_limit_bytes=VMEM_LIMIT_BYTES),
        name=name,
    )(y, w, x, g.reshape(1, d), b.reshape(1, d))


SUBLANES = 8


def _conv_rows_carry(x, tail, w_ref, ksize):
    row8 = lax.broadcasted_iota(jnp.int32, (SUBLANES, x.shape[1]), 0)
    y = x * w_ref[ksize - 1:ksize, :]
    for k in range(1, ksize):
        rolled = pltpu.roll(x, k, axis=0)
        first = jnp.where(row8 < k, pltpu.roll(tail, k, axis=0), rolled[0:SUBLANES])
        y = y + jnp.concatenate([first, rolled[SUBLANES:]], axis=0) * w_ref[ksize - 1 - k:ksize - k, :]
    return y


def _conv_rows_blocked(x, buf, w_ref, ksize, step):
    r = x.shape[0]
    y = x * w_ref[ksize - 1:ksize, :]
    for k in range(1, ksize):
        prev = jnp.concatenate([buf[(ksize - 1 - k) * step:(ksize - 1) * step], x[0:r - k * step]], axis=0)
        y = y + prev * w_ref[ksize - 1 - k:ksize - k, :]
    return y


def _conv_start(buf_ref, tail_s, mode):
    kind, param = mode
    if kind == "carry":
        @pl.when(pl.program_id(0) % param == 0)
        def _sequence_start():
            tail_s[...] = buf_ref[0]


def _conv_tile(x, buf_ref, st_ref, tail_s, w_ref, ksize, mode, cols=slice(None)):
    kind, param = mode
    w = w_ref[:, cols]
    if kind == "carry":
        tm = x.shape[0]
        y = _conv_rows_carry(x, tail_s[:, cols], w, ksize)
        tail_s[:, cols] = x[tm - SUBLANES:tm]
        st_ref[0, :, cols] = x[tm - SUBLANES:tm]
        return y
    y = _conv_rows_blocked(x, buf_ref[0, :, cols], w, ksize, param)
    st_ref[0, :, cols] = x[x.shape[0] - (ksize - 1) * param:]
    return y


def _conv_specs(mode, ksize, c, tm):
    kind, param = mode
    if kind == "carry":
        rows = SUBLANES
        idx = lambda i: (i // param, 0, 0)
    else:
        rows = (ksize - 1) * param
        idx = lambda i: (0, 0, 0)
    return pl.BlockSpec((1, rows, c), idx), rows


def _mix_out_ln_kernel(ab_ref, ac_ref, ah_ref, yb_ref, yc_ref, x_ref, w_ref, cw_ref, g_ref, b_ref, buf_ref,
                       o_ref, st_ref, tail_s, *, mode):
    u = ac_ref[...] * ah_ref[...]
    _conv_start(buf_ref, tail_s, mode)
    z = _conv_tile(u, buf_ref, st_ref, tail_s, cw_ref, A_CONV, mode)
    y = jnp.concatenate([ab_ref[...] * z, yb_ref[...], yc_ref[...]], axis=1).astype(BF16)
    acc = jnp.dot(y, w_ref[...], preferred_element_type=F32)
    o_ref[...] = _ln_rows(ALPHA * x_ref[...] + acc, g_ref[...], b_ref[...])


def mix_out_ln(proj, y_b, y_c, x, w_out, conv_w, g, b, buf, tm, mode):
    m = x.shape[0]
    buf_spec, st_rows = _conv_specs(mode, A_CONV, A_WIDTH, tm)
    n_st = buf.shape[0]
    row = lambda c, w: pl.BlockSpec((tm, w), lambda i: (i, c // w))
    const = lambda shape: pl.BlockSpec(shape, lambda i: (0,) * len(shape))
    return pl.pallas_call(
        functools.partial(_mix_out_ln_kernel, mode=mode),
        grid=(m // tm,),
        in_specs=[row(COL_AB, A_WIDTH), row(COL_AC, A_WIDTH), row(COL_AH, A_WIDTH),
                  row(0, GDN_WIDTH), row(0, NSA_WIDTH), row(0, D_MODEL),
                  const((D_MODEL, D_MODEL)), const((A_CONV, A_WIDTH)), const((1, D_MODEL)), const((1, D_MODEL)),
                  buf_spec],
        out_specs=[row(0, D_MODEL), pl.BlockSpec((1, st_rows, A_WIDTH), buf_spec.index_map)],
        out_shape=[jax.ShapeDtypeStruct((m, D_MODEL), F32), jax.ShapeDtypeStruct((n_st, st_rows, A_WIDTH), F32)],
        scratch_shapes=[pltpu.VMEM((SUBLANES, A_WIDTH), F32)],
        compiler_params=pltpu.CompilerParams(dimension_semantics=("arbitrary",), vmem_limit_bytes=VMEM_LIMIT_BYTES),
        name="mix_out_ln",
    )(proj, proj, proj, y_b, y_c, x, w_out, conv_w, g.reshape(1, -1), b.reshape(1, -1), buf)


FFN_COL_CHUNK = 256


def _ffn_up_kernel(x_ref, wg_ref, wv_ref, cw_ref, buf_ref, h_ref, st_ref, tail_s, *, mode):
    xb = x_ref[...].astype(BF16)
    _conv_start(buf_ref, tail_s, mode)
    tn = h_ref.shape[1]
    cb = FFN_COL_CHUNK if tn % FFN_COL_CHUNK == 0 else tn
    for j in range(tn // cb):
        cols = slice(j * cb, (j + 1) * cb)
        gate = jnp.dot(xb, wg_ref[:, cols], preferred_element_type=F32)
        val = jnp.dot(xb, wv_ref[:, cols], preferred_element_type=F32)
        gate = _conv_tile(gate, buf_ref, st_ref, tail_s, cw_ref, FFN_CONV, mode, cols)
        h_ref[:, cols] = (jax.nn.silu(gate) * val).astype(BF16)


def ffn_up_act(x, w_up, conv_w, buf, tm, tn, mode):
    m = x.shape[0]
    kind, param = mode
    nj = D_FF // tn
    if kind == "carry":
        assert nj == 1
        st_rows, st_idx = SUBLANES, (lambda i, j: (i // param, 0, j))
    else:
        assert tm == m
        st_rows, st_idx = (FFN_CONV - 1) * param, (lambda i, j: (0, 0, j))
    n_st = buf.shape[0]
    once = pl.Buffered(1) if nj == 1 else None
    return pl.pallas_call(
        functools.partial(_ffn_up_kernel, mode=mode),
        grid=(m // tm, nj),
        in_specs=[pl.BlockSpec((tm, D_MODEL), lambda i, j: (i, 0)),
                  pl.BlockSpec((D_MODEL, tn), lambda i, j: (0, j), pipeline_mode=once),
                  pl.BlockSpec((D_MODEL, tn), lambda i, j: (0, nj + j), pipeline_mode=once),
                  pl.BlockSpec((FFN_CONV, tn), lambda i, j: (0, j)),
                  pl.BlockSpec((1, st_rows, tn), st_idx)],
        out_specs=[pl.BlockSpec((tm, tn), lambda i, j: (i, j)),
                   pl.BlockSpec((1, st_rows, tn), st_idx)],
        out_shape=[jax.ShapeDtypeStruct((m, D_FF), BF16), jax.ShapeDtypeStruct((n_st, st_rows, D_FF), F32)],
        scratch_shapes=[pltpu.VMEM((SUBLANES, tn), F32)],
        compiler_params=pltpu.CompilerParams(dimension_semantics=("arbitrary", "arbitrary"),
                                             vmem_limit_bytes=VMEM_LIMIT_BYTES),
        name="ffn_up_act",
    )(x, w_up, w_up, conv_w, buf)


def _conv_silu_kernel(x_ref, cw_ref, buf_ref, o_ref, st_ref, tail_s, *, mode):
    _conv_start(buf_ref, tail_s, mode)
    o_ref[...] = jax.nn.silu(_conv_tile(x_ref[...], buf_ref, st_ref, tail_s, cw_ref, GDN_CONV, mode))


def gdn_conv_silu(proj, conv_w, buf, tm, mode):
    m = proj.shape[0]
    buf_spec, st_rows = _conv_specs(mode, GDN_CONV, GDN_CONV_CH, tm)
    n_st = buf.shape[0]
    return pl.pallas_call(
        functools.partial(_conv_silu_kernel, mode=mode),
        grid=(m // tm,),
        in_specs=[pl.BlockSpec((tm, GDN_CONV_CH), lambda i: (i, COL_GQKV // GDN_CONV_CH)),
                  pl.BlockSpec((GDN_CONV, GDN_CONV_CH), lambda i: (0, 0)),
                  buf_spec],
        out_specs=[pl.BlockSpec((tm, GDN_CONV_CH), lambda i: (i, 0)),
                   pl.BlockSpec((1, st_rows, GDN_CONV_CH), buf_spec.index_map)],
        out_shape=[jax.ShapeDtypeStruct((m, GDN_CONV_CH), F32),
                   jax.ShapeDtypeStruct((n_st, st_rows, GDN_CONV_CH), F32)],
        scratch_shapes=[pltpu.VMEM((SUBLANES, GDN_CONV_CH), F32)],
        compiler_params=pltpu.CompilerParams(dimension_semantics=("arbitrary",), vmem_limit_bytes=VMEM_LIMIT_BYTES),
        name="gdn_conv_silu",
    )(proj, conv_w, buf)


GDN_ROWS = 128
GDN_STEP = 256


def _mm(a, b):
    return jnp.dot(a, b, preferred_element_type=F32)


def _split3(x):
    hi = x.astype(BF16)
    r1 = x - hi.astype(F32)
    mid = r1.astype(BF16)
    return hi, mid, (r1 - mid.astype(F32)).astype(BF16)


def _mm_exact_lhs(c, x):
    hi, mid, lo = _split3(x)
    return _mm(c, hi) + _mm(c, mid) + _mm(c, lo)


def _mm_exact_rhs(x, c):
    hi, mid, lo = _split3(x)
    return _mm(hi, c) + _mm(mid, c) + _mm(lo, c)


def _mm3(a, b):
    ah = a.astype(BF16)
    al = (a - ah.astype(F32)).astype(BF16)
    bh = b.astype(BF16)
    bl = (b - bh.astype(F32)).astype(BF16)
    return _mm(ah, bh) + _mm(ah, bl) + _mm(al, bh)


def _gdn_kernel(qkv_ref, gate_ref, sm_ref, alog_ref, dt_ref, gain_ref, ea_ref, eb_ref, tril_ref, cones_ref,
                hones_ref, s0_ref, y_ref, sout_ref, s_s, o_s, *, chunk, carry_state, tiles_per_seq):
    rws = GDN_ROWS
    ngrp = qkv_ref.shape[0] // rws
    nchunk = rws // chunk
    shift = chunk.bit_length() - 1
    nt = (((1,), (1,)), ((), ()))
    tn = (((0,), (0,)), ((), ()))

    if carry_state:
        @pl.when(pl.program_id(0) % tiles_per_seq == 0)
        def _sequence_start():
            s_s[...] = s0_ref[0]

    hones = hones_ref[...]

    def head_sum(x):
        xh = x.astype(BF16)
        return _mm(xh, hones) + _mm((x - xh.astype(F32)).astype(BF16), hones)

    qkv = qkv_ref[...]
    q, k, v = qkv[:, 0:GDN_QK], qkv[:, GDN_QK:2 * GDN_QK], qkv[:, 2 * GDN_QK:]
    q = q * lax.rsqrt(head_sum(q * q) + 1e-6) * (GDN_DK ** -0.5)
    k = k * lax.rsqrt(head_sum(k * k) + 1e-6)
    sm = sm_ref[...]
    g = -jnp.exp(alog_ref[...]) * jax.nn.softplus(_mm_exact_rhs(sm, ea_ref[...]) + dt_ref[...])
    beta = jax.nn.sigmoid(_mm_exact_rhs(sm, eb_ref[...]))
    gc = _mm_exact_lhs(tril_ref[...], g)
    gcl = _mm_exact_lhs(cones_ref[...], g)

    row = lax.broadcasted_iota(jnp.int32, (rws, rws), 0)
    col = lax.broadcasted_iota(jnp.int32, (rws, rws), 1)
    same = lax.shift_right_logical(row, shift) == lax.shift_right_logical(col, shift)
    eye = jnp.where(row == col, 1.0, 0.0)

    heads = range(GDN_HEADS)
    chains = [(slice(grp * rws, (grp + 1) * rws), h) for grp in range(ngrp) for h in heads]
    cid = lambda grp, h: grp * GDN_HEADS + h
    hsl = [slice(GDN_DK * h, GDN_DK * (h + 1)) for h in heads]
    incl = same & (row >= col)
    strict = same & (row > col)
    xs = [gc[rg, 128 * h:128 * (h + 1)] for rg, h in chains]
    gcols = [x[:, 0:GDN_DK] for x in xs]
    gends = [gcl[rg, 128 * h:128 * h + GDN_DK] for rg, h in chains]
    betas = [beta[rg, 128 * h:128 * h + GDN_DK] for rg, h in chains]
    decays = [jnp.where(incl, jnp.exp(jnp.where(incl, x - x.T, 0.0)), 0.0) for x in xs]
    kbs = [k[rg, hsl[h]] * betas[c] for c, (rg, h) in enumerate(chains)]
    khbs = [k[rg, hsl[h]].astype(BF16) for rg, h in chains]
    heads_all = range(len(chains))
    a_s = [jnp.where(strict, lax.dot_general(kbs[c].astype(BF16), khbs[c], nt, preferred_element_type=F32) * decays[c], 0.0)
           for c in heads_all]
    compact = lambda full: jnp.sum(full.reshape(nchunk, chunk, rws), axis=0)
    expand = lambda c: jnp.where(same, jnp.concatenate([c] * nchunk, axis=0), 0.0)
    eye_c = compact(eye)
    a_cs = [compact(a) for a in a_s]
    minv_cs = [eye_c - a_c for a_c in a_cs]
    apow_cs = [_mm3(a_cs[c], a_s[c]) for c in heads_all]
    for step in range(shift - 1):
        apow_fs = [expand(ap) for ap in apow_cs]
        minv_cs = [_mm3(minv_cs[c], eye + apow_fs[c]) for c in heads_all]
        if step < shift - 2:
            apow_cs = [_mm3(apow_cs[c], apow_fs[c]) for c in heads_all]
    minvs = [expand(m_c) for m_c in minv_cs]
    egcs = [jnp.exp(gc_h) for gc_h in gcols]
    us = [_mm3(minvs[c], v[rg, hsl[h]] * betas[c]) for c, (rg, h) in enumerate(chains)]
    wbs = [_mm3(minvs[c], kbs[c] * egcs[c]).astype(BF16) for c in heads_all]
    a_qks = [(lax.dot_general(q[rg, hsl[h]].astype(BF16), khbs[c], nt, preferred_element_type=F32) * decays[c]).astype(BF16)
             for c, (rg, h) in enumerate(chains)]
    q_decs = [(q[rg, hsl[h]] * egcs[c]).astype(BF16) for c, (rg, h) in enumerate(chains)]
    k_decs = [(k[rg, hsl[h]] * jnp.exp(gends[c] - gcols[c])).astype(BF16) for c, (rg, h) in enumerate(chains)]
    g_lasts = [jnp.exp(g_h) for g_h in gends]
    states = [s_s[h] for h in heads] if carry_state else None
    v_new = [[] for _ in heads_all]
    o_state = [[] for _ in heads_all]
    for n in range(ngrp * nchunk):
        grp, nl = divmod(n, nchunk)
        r = slice(nl * chunk, (nl + 1) * chunk)
        olds = states if carry_state else [s0_ref[n, h] for h in heads]
        sbs = [s_old.astype(BF16) for s_old in olds]
        vns = [us[cid(grp, h)][r] - _mm(wbs[cid(grp, h)][r], sbs[h]) for h in heads]
        for h in heads:
            o_state[cid(grp, h)].append(_mm(q_decs[cid(grp, h)][r], sbs[h]))
            v_new[cid(grp, h)].append(vns[h])
        news = [olds[h] * g_lasts[cid(grp, h)][nl * chunk:nl * chunk + 1, :]
                + lax.dot_general(k_decs[cid(grp, h)][r], vns[h].astype(BF16), tn, preferred_element_type=F32)
                for h in heads]
        if carry_state:
            states = news
        else:
            for h in heads:
                sout_ref[n, h] = news[h]
    for c, (rg, h) in enumerate(chains):
        v_all = jnp.concatenate(v_new[c], axis=0) if nchunk > 1 else v_new[c][0]
        o_all = jnp.concatenate(o_state[c], axis=0) if nchunk > 1 else o_state[c][0]
        o_s[rg, hsl[h]] = o_all + _mm(a_qks[c], v_all.astype(BF16))
    if carry_state:
        for h in heads:
            s_s[h] = states[h]

    if carry_state:
        sout_ref[0] = s_s[...]
    o = o_s[...]
    o = o * lax.rsqrt(head_sum(o * o) * (1.0 / GDN_DV) + 1e-6)
    y_ref[...] = o * gain_ref[...] * jax.nn.silu(gate_ref[...])


def gated_delta(qkv_act, proj, s0, a_log, dt_bias, norm_g, chunk, carry_state, tiles_per_seq):
    m = qkv_act.shape[0]
    rws = GDN_STEP
    idx = np.arange(rws)
    same = (idx[:, None] // chunk) == (idx[None, :] // chunk)
    tril = jnp.asarray(same & (idx[:, None] >= idx[None, :]), BF16)
    cones = jnp.asarray(same, BF16)
    lane = np.arange(GDN_WIDTH)
    hones = jnp.asarray(lane[:, None] // GDN_DK == lane[None, :] // GDN_DK, BF16)
    lane2 = np.arange(4 * 128) // 128
    smr = np.arange(128)
    ea = jnp.asarray(smr[:, None] == SM_GA + lane2[None, :], BF16)
    eb = jnp.asarray(smr[:, None] == SM_GB + lane2[None, :], BF16)
    alog_x = jnp.repeat(a_log.astype(F32), 128).reshape(1, 512)
    dt_x = jnp.repeat(dt_bias.astype(F32), 128).reshape(1, 512)
    gain_x = jnp.tile(norm_g.astype(F32), GDN_HEADS).reshape(1, GDN_WIDTH)
    if carry_state:
        s_blk, s_idx = (1, GDN_HEADS, GDN_DK, GDN_DV), (lambda i: (i // tiles_per_seq, 0, 0, 0))
    else:
        s_blk, s_idx = (rws // chunk, GDN_HEADS, GDN_DK, GDN_DV), (lambda i: (i, 0, 0, 0))
    const = lambda shape: pl.BlockSpec(shape, lambda i: (0,) * len(shape))
    return pl.pallas_call(
        functools.partial(_gdn_kernel, chunk=chunk, carry_state=carry_state, tiles_per_seq=tiles_per_seq),
        grid=(m // rws,),
        in_specs=[pl.BlockSpec((rws, GDN_CONV_CH), lambda i: (i, 0)),
                  pl.BlockSpec((rws, GDN_WIDTH), lambda i: (i, COL_GGATE // GDN_WIDTH)),
                  pl.BlockSpec((rws, 128), lambda i: (i, COL_SMALL // 128)),
                  const((1, 512)), const((1, 512)), const((1, GDN_WIDTH)),
                  const((128, 512)), const((128, 512)), const((rws, rws)), const((rws, rws)),
                  const((GDN_WIDTH, GDN_WIDTH)),
                  pl.BlockSpec(s_blk, s_idx)],
        out_specs=[pl.BlockSpec((rws, GDN_WIDTH), lambda i: (i, 0)), pl.BlockSpec(s_blk, s_idx)],
        out_shape=[jax.ShapeDtypeStruct((m, GDN_WIDTH), F32), jax.ShapeDtypeStruct(s0.shape, F32)],
        scratch_shapes=[pltpu.VMEM((GDN_HEADS, GDN_DK, GDN_DV), F32), pltpu.VMEM((rws, GDN_WIDTH), F32)],
        compiler_params=pltpu.CompilerParams(dimension_semantics=("arbitrary",), vmem_limit_bytes=VMEM_LIMIT_BYTES),
        name="gated_delta",
    )(qkv_act, proj, proj, alog_x, dt_x, gain_x, ea, eb, tril, cones, hones, s0)


COL_Q = 0
COL_KV = 512
COL_AB = 1024
COL_AC = 1280
COL_GQKV = 1536
COL_AH = 2304
COL_GGATE = 2560
COL_WIN = 2816
COL_SMALL = 3072
N_PROJ = 3200
SM_GA, SM_GB, SM_NG = 0, 4, 8


def permute_w_in(w_in):
    offs = np.concatenate([[0], np.cumsum(IN_SIZES)])
    grp = lambda k: w_in[..., offs[k]:offs[k + 1]]
    pad = jnp.zeros(w_in.shape[:-1] + (N_PROJ - COL_SMALL - 32,), w_in.dtype)
    order = [grp(9), grp(10), grp(0), grp(1), grp(3), grp(4), grp(5), grp(2), grp(6), grp(11),
             grp(7), grp(8), grp(12), pad]
    return jnp.concatenate(order, axis=-1)


NEG = -1e30
KEY_CHUNK = 128
SEL_CHUNK = 256


def _block_diag4(a, b):
    z = jnp.zeros_like(a)
    rows = [jnp.concatenate([a, z, z, z], -1), jnp.concatenate([z, a, z, z], -1),
            jnp.concatenate([z, z, b, z], -1), jnp.concatenate([z, z, z, b], -1)]
    return jnp.concatenate(rows, -2)


def compress_params(cmp_pe, cmp_w1, cmp_b1, cmp_w2):
    w1 = _block_diag4(cmp_w1[0], cmp_w1[1]).astype(BF16)
    w2 = _block_diag4(cmp_w2[0], cmp_w2[1]).astype(BF16)
    pe = jnp.concatenate([cmp_pe[0], cmp_pe[0], cmp_pe[1], cmp_pe[1]], -1)
    b1 = jnp.concatenate([cmp_b1[0], cmp_b1[0], cmp_b1[1], cmp_b1[1]], -1).reshape(1, 256)
    return w1, pe, b1, w2


def _compress_rows(row_loader, ncp, w1_ref, pe_ref, b1_ref, w2_ref):
    acc_lo = jnp.zeros((ncp, 256), F32)
    acc_hi = jnp.zeros((ncp, 256), F32)
    for l in range(CMP_STRIDE):
        x = row_loader(l)
        acc_lo += jnp.dot((x + pe_ref[l:l + 1, :]).astype(BF16), w1_ref[l], preferred_element_type=F32)
        acc_hi += jnp.dot((x + pe_ref[l + CMP_STRIDE:l + CMP_STRIDE + 1, :]).astype(BF16),
                          w1_ref[l + CMP_STRIDE], preferred_element_type=F32)
    hid = jax.nn.gelu(acc_lo + pltpu.roll(acc_hi, ncp - 1, axis=0) + b1_ref[...])
    return jnp.dot(hid.astype(BF16), w2_ref[...], preferred_element_type=F32)


def _cmp_attend(q_rows, kc, vc, pos, ovl, n_heads, tq, ncp):
    s = lax.dot_general(q_rows, kc, (((1,), (1,)), ((), ())), preferred_element_type=F32)
    cend = lax.broadcasted_iota(jnp.int32, (tq, ncp), 1) * CMP_STRIDE + (CMP_BLOCK - 1)
    s3 = jnp.where((cend <= pos)[None], s.reshape(n_heads, tq, ncp), -jnp.inf)
    m = jnp.max(s3, axis=-1, keepdims=True)
    m = jnp.where(m == -jnp.inf, 0.0, m)
    p = jnp.exp(s3 - m)
    pn = p / jnp.maximum(jnp.sum(p, axis=-1, keepdims=True), 1e-30)
    o_cmp = jnp.dot(pn.reshape(n_heads * tq, ncp).astype(BF16), vc, preferred_element_type=F32)
    psum = pn[0]
    for g in range(1, n_heads):
        psum = psum + pn[g]
    p_hi = psum.astype(BF16)
    p_lo = (psum - p_hi.astype(F32)).astype(BF16)
    imp = (jnp.dot(p_hi, ovl, preferred_element_type=F32) + jnp.dot(p_lo, ovl, preferred_element_type=F32))
    return o_cmp, imp


def _select_topn(imp, pos):
    r, ns = imp.shape
    blk = lax.broadcasted_iota(jnp.int32, (r, ns), 1)
    blk_f = blk.astype(F32)
    forced = (blk == 0) | (blk == lax.shift_right_logical(pos, 6))
    valid = blk * SEL_BLOCK <= pos
    v = jnp.where(forced, jnp.inf, jnp.where(valid, imp, -jnp.inf))
    sel = jnp.zeros((r, ns), F32)
    for _ in range(min(SEL_TOPN, ns)):
        mx = jnp.max(v, axis=-1, keepdims=True)
        idx = jnp.min(jnp.where(v == mx, blk_f, float(ns)), axis=-1, keepdims=True)
        hit = blk_f == idx
        sel = jnp.where(hit, 1.0, sel)
        v = jnp.where(hit, -jnp.inf, v)
    return sel


def _nsa_prompt_kernel(q_ref, cmp_ref, slc_ref, win_ref, sm_ref, w1_ref, pe_ref, b1_ref, w2_ref, ovl_ref, exp_ref,
                       gexp_ref, y_ref, kc_s, vc_s, ks_s, vs_s, kw_s, vw_s, q_s, sel_s, ocmp_s, owin_s, m_s, acc_s, *, t_len):
    tq = KEY_CHUNK
    i = pl.program_id(1)
    ncp = t_len // CMP_STRIDE
    ns = t_len // SEL_BLOCK
    ng = NSA_GROUP
    rows = ng * tq

    @pl.when(i == 0)
    def _prepare_sequence():
        kvc = _compress_rows(lambda l: cmp_ref[0, l], ncp, w1_ref, pe_ref, b1_ref, w2_ref)
        kc_s[...] = kvc[:, 0:128].astype(BF16)
        vc_s[...] = kvc[:, 128:256].astype(BF16)
        ones = jnp.ones((512, 64), BF16)

        def cast_rows(r, carry):
            sl = pl.ds(pl.multiple_of(r * 512, 512), 512)
            for h in range(NSA_KV_HEADS):
                ks_s[h, sl, :] = slc_ref[sl, 64 * h:64 * h + 64].astype(BF16)
                vs_s[h, sl, 0:64] = slc_ref[sl, 128 + 64 * h:192 + 64 * h].astype(BF16)
                vs_s[h, sl, 64:128] = ones
                kw_s[h, sl, :] = win_ref[sl, 64 * h:64 * h + 64].astype(BF16)
                vw_s[h, sl, 0:64] = win_ref[sl, 128 + 64 * h:192 + 64 * h].astype(BF16)
                vw_s[h, sl, 64:128] = ones
            return carry

        lax.fori_loop(0, t_len // 512, cast_rows, 0)

    pos = i * tq + lax.broadcasted_iota(jnp.int32, (tq, 1), 0)
    gates = jax.nn.sigmoid(sm_ref[...])
    g_hi = gates.astype(BF16)
    gx = _mm(g_hi, gexp_ref[...]) + _mm((gates - g_hi.astype(F32)).astype(BF16), gexp_ref[...])
    nt = (((1,), (1,)), ((), ()))

    wk = min(WINDOW + KEY_CHUNK, t_len)
    w0 = pl.multiple_of(jnp.minimum(jnp.maximum(i - WINDOW // KEY_CHUNK, 0), (t_len - wk) // KEY_CHUNK) * KEY_CHUNK,
                        KEY_CHUNK)
    wpos = w0 + lax.broadcasted_iota(jnp.int32, (tq, wk), 1)
    wbias = jnp.where(wpos <= pos, jnp.where(wpos > pos - WINDOW, 0.0, NEG), NEG)

    hs = range(NSA_KV_HEADS)
    q_rows = []
    for h in hs:
        qh = q_ref[:, 256 * h:256 * h + 256] * (NSA_HD ** -0.5)
        q_rows.append(jnp.concatenate([qh[:, 64 * g:64 * g + 64] for g in range(ng)], axis=0).astype(BF16))
        q_s[h] = q_rows[h]
    cmp = [_cmp_attend(q_rows[h], kc_s[:, 64 * h:64 * h + 64], vc_s[:, 64 * h:64 * h + 64], pos, ovl_ref[...],
                       ng, tq, ncp) for h in hs]
    sw = [lax.dot_general(q_rows[h], kw_s[h, pl.ds(w0, wk), :], nt, preferred_element_type=F32) for h in hs]
    sw = [(s.reshape(ng, tq, wk) + wbias[None]).reshape(rows, wk) for s in sw]
    pw = [jnp.exp(s - jnp.max(s, axis=-1, keepdims=True)).astype(BF16) for s in sw]
    aw = [jnp.dot(pw[h], vw_s[h, pl.ds(w0, wk), :], preferred_element_type=F32) for h in hs]
    sel = _select_topn(jnp.concatenate([c[1] for c in cmp], axis=0), jnp.concatenate([pos] * NSA_KV_HEADS, axis=0))
    for h in hs:
        ocmp_s[h] = cmp[h][0]
        sel_s[h] = sel[h * tq:(h + 1) * tq].astype(BF16)
        owin_s[h] = aw[h][:, 0:64] / aw[h][:, 64:128]
        m_s[h] = jnp.full((rows, 128), NEG, F32)
        acc_s[h] = jnp.zeros((rows, 128), F32)

    lane = lax.broadcasted_iota(jnp.int32, (tq, SEL_CHUNK), 1)

    def sel_body(j, carry):
        off = pl.multiple_of(j * SEL_CHUNK, SEL_CHUNK)
        hs = range(NSA_KV_HEADS)
        pairs = [(h, slice(g * tq, (g + 1) * tq)) for h in hs for g in range(ng)]
        ks = [ks_s[h, pl.ds(off, SEL_CHUNK), :] for h in hs]
        vs = [vs_s[h, pl.ds(off, SEL_CHUNK), :] for h in hs]
        causal = off + lane <= pos
        biases = [jnp.where(causal, (jnp.dot(sel_s[h], exp_ref[:, pl.ds(off, SEL_CHUNK)],
                                            preferred_element_type=F32) - 1.0) * (-NEG), NEG) for h in hs]
        ss = [lax.dot_general(q_s[h, r, :], ks[h], nt, preferred_element_type=F32) + biases[h] for h, r in pairs]
        m_prevs = [m_s[h, r, :] for h, r in pairs]
        m_news = [jnp.maximum(mp, jnp.max(s, axis=-1, keepdims=True)) for mp, s in zip(m_prevs, ss)]
        ps = [jnp.exp(s - jnp.concatenate([mn] * (SEL_CHUNK // 128), axis=1)).astype(BF16)
              for s, mn in zip(ss, m_news)]
        pvs = [jnp.dot(p, vs[h], preferred_element_type=F32) for p, (h, r) in zip(ps, pairs)]
        for (h, r), mp, mn, pv in zip(pairs, m_prevs, m_news, pvs):
            acc_s[h, r, :] = jnp.exp(mp - mn) * acc_s[h, r, :] + pv
            m_s[h, r, :] = mn
        return carry

    lax.fori_loop(0, (i * tq) // SEL_CHUNK + 1, sel_body, 0)

    for h in range(NSA_KV_HEADS):
        acc = acc_s[h]
        o_slc = acc[:, 0:64] / acc[:, 64:128]
        o_cmp = ocmp_s[h]
        o_win = owin_s[h]
        for g in range(ng):
            hh = ng * h + g
            r = slice(g * tq, (g + 1) * tq)
            gate = lambda branch: gx[:, 128 * (NSA_HEADS * branch + hh):128 * (NSA_HEADS * branch + hh) + 64]
            y_ref[:, 64 * hh:64 * hh + 64] = gate(0) * o_cmp[r] + gate(1) * o_slc[r] + gate(2) * o_win[r]


def nsa_prompt(proj, n_seq, t_len, cparams):
    tq = KEY_CHUNK
    nt = t_len // tq
    ncp = t_len // CMP_STRIDE
    ns = t_len // SEL_BLOCK
    w1, pe, b1, w2 = cparams
    cstart = np.arange(ncp)[:, None] * CMP_STRIDE
    sstart = np.arange(ns)[None, :] * SEL_BLOCK
    ovl = jnp.asarray((cstart < sstart + SEL_BLOCK) & (cstart + CMP_BLOCK > sstart), BF16)
    expand = jnp.asarray(np.arange(t_len)[None, :] // SEL_BLOCK == np.arange(ns)[:, None], BF16)
    gexp = jnp.asarray(np.arange(128)[:, None] == SM_NG + np.arange(3 * NSA_HEADS * 128)[None, :] // 128, BF16)
    cmp_rows = proj[:, COL_KV:COL_KV + 256].reshape(n_seq, ncp, CMP_STRIDE, 256).transpose(0, 2, 1, 3)
    once = pl.Buffered(1)
    const = lambda shape: pl.BlockSpec(shape, lambda b, i: (0,) * len(shape), pipeline_mode=once)
    return pl.pallas_call(
        functools.partial(_nsa_prompt_kernel, t_len=t_len),
        grid=(n_seq, nt),
        in_specs=[pl.BlockSpec((tq, 512), lambda b, i: (b * nt + i, COL_Q // 512)),
                  pl.BlockSpec((1, CMP_STRIDE, ncp, 256), lambda b, i: (b, 0, 0, 0), pipeline_mode=once),
                  pl.BlockSpec((t_len, 256), lambda b, i: (b, (COL_KV + 256) // 256), pipeline_mode=once),
                  pl.BlockSpec((t_len, 256), lambda b, i: (b, COL_WIN // 256), pipeline_mode=once),
                  pl.BlockSpec((tq, 128), lambda b, i: (b * nt + i, COL_SMALL // 128)),
                  const((CMP_BLOCK, 256, 256)), const((CMP_BLOCK, 256)), const((1, 256)), const((256, 256)),
                  const((ncp, ns)), const((ns, t_len)), const((128, 3 * NSA_HEADS * 128))],
        out_specs=pl.BlockSpec((tq, NSA_WIDTH), lambda b, i: (b * nt + i, 0)),
        out_shape=jax.ShapeDtypeStruct((n_seq * t_len, NSA_WIDTH), F32),
        scratch_shapes=[pltpu.VMEM((ncp, 128), BF16), pltpu.VMEM((ncp, 128), BF16),
                        pltpu.VMEM((NSA_KV_HEADS, t_len, 64), BF16), pltpu.VMEM((NSA_KV_HEADS, t_len, 128), BF16),
                        pltpu.VMEM((NSA_KV_HEADS, t_len, 64), BF16), pltpu.VMEM((NSA_KV_HEADS, t_len, 128), BF16),
                        pltpu.VMEM((NSA_KV_HEADS, NSA_GROUP * tq, 64), BF16), pltpu.VMEM((NSA_KV_HEADS, tq, ns), BF16),
                        pltpu.VMEM((NSA_KV_HEADS, NSA_GROUP * tq, 64), F32),
                        pltpu.VMEM((NSA_KV_HEADS, NSA_GROUP * tq, 64), F32),
                        pltpu.VMEM((NSA_KV_HEADS, NSA_GROUP * tq, 128), F32),
                        pltpu.VMEM((NSA_KV_HEADS, NSA_GROUP * tq, 128), F32)],
        compiler_params=pltpu.CompilerParams(dimension_semantics=("arbitrary", "arbitrary"),
                                             vmem_limit_bytes=VMEM_LIMIT_BYTES),
        name="nsa_prompt",
    )(proj, cmp_rows, proj, proj, proj, w1, pe, b1, w2, ovl, expand, gexp)


PAGE_ROWS = 128
SEL_LANES = 128


def _softmax_segments(segs):
    m = segs[0].max(axis=-1, keepdims=True)
    for s in segs[1:]:
        m = jnp.maximum(m, s.max(axis=-1, keepdims=True))
    ps = [jnp.exp(s - m) for s in segs]
    den = ps[0].sum(axis=-1, keepdims=True)
    for p in ps[1:]:
        den = den + p.sum(axis=-1, keepdims=True)
    return ps, den


def _nsa_sample_kernel(pt_ref, *refs, n_pages, past_len, dec_t):
    del pt_ref
    pages = refs[:n_pages]
    (q_ref, kvn_ref, wn_ref, sm_ref, wc_ref, w1_ref, pe_ref, b1_ref, w2_ref, ovl_ref, exp_ref,
     y_ref, wst_ref, rk_s, rv_s, newpg_s, neww_s) = refs[n_pages:]
    ncp = past_len // CMP_STRIDE
    ng, nh = NSA_GROUP, NSA_KV_HEADS
    rows_h = ng * dec_t
    rows = nh * rows_h
    nt = (((1,), (1,)), ((), ()))

    for p in range(n_pages):
        rk_s[p * PAGE_ROWS:(p + 1) * PAGE_ROWS, :] = pages[p][0:128, :].T
        rv_s[p * PAGE_ROWS:(p + 1) * PAGE_ROWS, :] = pages[p][128:256, :].T
    kvc = _compress_rows(
        lambda l: jnp.concatenate([rk_s[pl.ds(l, ncp, stride=CMP_STRIDE), :],
                                   rv_s[pl.ds(l, ncp, stride=CMP_STRIDE), :]], axis=1),
        ncp, w1_ref, pe_ref, b1_ref, w2_ref)
    kc = kvc[:, 0:128].astype(BF16)
    vc = kvc[:, 128:256].astype(BF16)

    newpg_s[...] = jnp.zeros((PAGE_ROWS, 256), F32)
    newpg_s[0:dec_t, :] = kvn_ref[0][:, 256:512]
    neww_s[...] = jnp.zeros((PAGE_ROWS, 256), F32)
    neww_s[0:dec_t, :] = wn_ref[0]

    qf = q_ref[0] * (NSA_HD ** -0.5)
    zero = jnp.zeros((dec_t, 64), F32)
    qrows = []
    for h in range(nh):
        for g in range(ng):
            piece = qf[:, 64 * (ng * h + g):64 * (ng * h + g) + 64]
            qrows.append(jnp.concatenate([piece, zero] if h == 0 else [zero, piece], axis=1))
    q_bd = jnp.concatenate(qrows, axis=0).astype(BF16)
    t_row = lax.broadcasted_iota(jnp.int32, (rows, 1), 0) & (dec_t - 1)
    pos = past_len + t_row
    head0 = lax.broadcasted_iota(jnp.int32, (rows, 128), 0) < rows_h
    lane_lo = lax.broadcasted_iota(jnp.int32, (rows, 128), 1) < 64
    own = head0 == lane_lo

    def own_half(x):
        x = jnp.where(own, x, 0.0)
        return x[:, 0:64] + x[:, 64:128]

    s = lax.dot_general(q_bd, kc, nt, preferred_element_type=F32)
    cend = lax.broadcasted_iota(jnp.int32, (rows, ncp), 1) * CMP_STRIDE + (CMP_BLOCK - 1)
    s = jnp.where(cend <= pos, s, -jnp.inf)
    m = jnp.max(s, axis=-1, keepdims=True)
    m = jnp.where(m == -jnp.inf, 0.0, m)
    p = jnp.exp(s - m)
    pn = p / jnp.maximum(jnp.sum(p, axis=-1, keepdims=True), 1e-30)
    o_cmp = own_half(_mm(pn.astype(BF16), vc))

    ovl = ovl_ref[...]
    psums = []
    for h in range(nh):
        psum = pn[h * rows_h:h * rows_h + dec_t]
        for g in range(1, ng):
            psum = psum + pn[h * rows_h + g * dec_t:h * rows_h + (g + 1) * dec_t]
        psums.append(psum)
    psum = jnp.concatenate(psums, axis=0)
    p_hi = psum.astype(BF16)
    imp = _mm(p_hi, ovl) + _mm((psum - p_hi.astype(F32)).astype(BF16), ovl)
    pos_ht = past_len + (lax.broadcasted_iota(jnp.int32, (nh * dec_t, 1), 0) & (dec_t - 1))
    sel = _select_topn(imp, pos_ht)
    sel_b = jnp.concatenate([sel[h * dec_t:(h + 1) * dec_t] for h in range(nh) for _ in range(ng)],
                            axis=0).astype(BF16)

    lane = lax.broadcasted_iota(jnp.int32, (rows, PAGE_ROWS), 1)
    segs = []
    for pg in range(n_pages + 1):
        if pg < n_pages:
            sc = _mm(q_bd, pages[pg][256:384, :].astype(BF16))
        else:
            sc = lax.dot_general(q_bd, newpg_s[:, 0:128].astype(BF16), nt, preferred_element_type=F32)
        chosen = _mm(sel_b, exp_ref[:, pg * PAGE_ROWS:(pg + 1) * PAGE_ROWS])
        ok = chosen > 0.5
        if pg == n_pages:
            ok = ok & (past_len + lane <= pos)
        segs.append(jnp.where(ok, sc, NEG))
    ps, den = _softmax_segments(segs)
    acc = jnp.zeros((rows, 128), F32)
    for pg in range(n_pages + 1):
        if pg < n_pages:
            acc = acc + lax.dot_general(ps[pg].astype(BF16), pages[pg][384:512, :].astype(BF16), nt,
                                        preferred_element_type=F32)
        else:
            acc = acc + _mm(ps[pg].astype(BF16), newpg_s[:, 128:256].astype(BF16))
    o_slc = own_half(acc) / den

    wlen = wc_ref.shape[2]
    wc = wc_ref[0]
    jw = lax.broadcasted_iota(jnp.int32, (rows, wlen), 1)
    s_old = _mm(q_bd, wc[0:128, :].astype(BF16))
    s_old = jnp.where(past_len - wlen + jw > pos - WINDOW, s_old, NEG)
    s_new = lax.dot_general(q_bd, neww_s[:, 0:128].astype(BF16), nt, preferred_element_type=F32)
    s_new = jnp.where(past_len + lane <= pos, s_new, NEG)
    ps, den = _softmax_segments([s_old, s_new])
    acc = (lax.dot_general(ps[0].astype(BF16), wc[128:256, :].astype(BF16), nt, preferred_element_type=F32)
           + _mm(ps[1].astype(BF16), neww_s[:, 128:256].astype(BF16)))
    o_win = own_half(acc) / den

    gates = jax.nn.sigmoid(sm_ref[0][:, SM_NG:SM_NG + 3 * NSA_HEADS])
    for hh in range(NSA_HEADS):
        r = slice(hh * dec_t, (hh + 1) * dec_t)
        y_ref[0, :, 64 * hh:64 * hh + 64] = (gates[:, hh:hh + 1] * o_cmp[r]
                                             + gates[:, NSA_HEADS + hh:NSA_HEADS + hh + 1] * o_slc[r]
                                             + gates[:, 2 * NSA_HEADS + hh:2 * NSA_HEADS + hh + 1] * o_win[r])

    rolled = pltpu.roll(wc, wlen - dec_t, axis=1)
    new_t = jnp.concatenate([neww_s[:, 0:128].T, neww_s[:, 128:256].T], axis=0)
    lane_w = lax.broadcasted_iota(jnp.int32, (256, 128), 1)
    last = jnp.where(lane_w >= 128 - dec_t, pltpu.roll(new_t, 128 - dec_t, axis=1), rolled[:, wlen - 128:wlen])
    wst_ref[0, :, 0:wlen - 128] = rolled[:, 0:wlen - 128]
    wst_ref[0, :, wlen - 128:wlen] = last


def nsa_sample(q, kv_new, win_new, sm, cache_kv, layer_idx, page_table, win_cache, cparams):
    bsz, dec_t, _ = q.shape
    n_pages = page_table.shape[1]
    past_len = n_pages * PAGE_ROWS
    wlen = win_cache.shape[2]
    assert dec_t < CMP_STRIDE and dec_t % SUBLANES == 0 and dec_t & (dec_t - 1) == 0
    assert wlen == WINDOW and wlen <= past_len
    ncp = past_len // CMP_STRIDE
    ns = -(-(past_len + dec_t) // SEL_BLOCK)
    assert ns <= SEL_LANES
    w1, pe, b1, w2 = cparams
    cstart = np.arange(ncp)[:, None] * CMP_STRIDE
    sstart = np.arange(SEL_LANES)[None, :] * SEL_BLOCK
    ovl = jnp.asarray((cstart < sstart + SEL_BLOCK) & (cstart + CMP_BLOCK > sstart) & (np.arange(SEL_LANES)[None, :] < ns), BF16)
    keys = np.arange((n_pages + 1) * PAGE_ROWS)
    expand = jnp.asarray(keys[None, :] // SEL_BLOCK == np.arange(SEL_LANES)[:, None], BF16)
    const = lambda shape: pl.BlockSpec(shape, lambda b, pt: (0,) * len(shape))
    seq = lambda r, c: pl.BlockSpec((1, r, c), lambda b, pt: (b, 0, 0))
    page_specs = [pl.BlockSpec((None, None, 512, PAGE_ROWS), functools.partial(
        lambda b, pt, p: (layer_idx, pt[b, p], 0, 0), p=p)) for p in range(n_pages)]
    grid_spec = pltpu.PrefetchScalarGridSpec(
        num_scalar_prefetch=1,
        grid=(bsz,),
        in_specs=page_specs + [seq(dec_t, 512), seq(dec_t, 512), seq(dec_t, 256), seq(dec_t, 128), seq(256, wlen),
                               const((CMP_BLOCK, 256, 256)), const((CMP_BLOCK, 256)), const((1, 256)),
                               const((256, 256)), const((ncp, SEL_LANES)),
                               const((SEL_LANES, (n_pages + 1) * PAGE_ROWS))],
        out_specs=[seq(dec_t, NSA_WIDTH), seq(256, wlen)],
        scratch_shapes=[pltpu.VMEM((past_len, 128), F32), pltpu.VMEM((past_len, 128), F32),
                        pltpu.VMEM((PAGE_ROWS, 256), F32), pltpu.VMEM((PAGE_ROWS, 256), F32)])
    return pl.pallas_call(
        functools.partial(_nsa_sample_kernel, n_pages=n_pages, past_len=past_len, dec_t=dec_t),
        grid_spec=grid_spec,
        out_shape=[jax.ShapeDtypeStruct((bsz, dec_t, NSA_WIDTH), F32), jax.ShapeDtypeStruct((bsz, 256, wlen), F32)],
        compiler_params=pltpu.CompilerParams(dimension_semantics=("arbitrary",), vmem_limit_bytes=VMEM_LIMIT_BYTES),
        name="nsa_sample",
    )(page_table, *([cache_kv] * n_pages), q, kv_new, win_new, sm, win_cache, w1, pe, b1, w2, ovl, expand)


TM = 512


def _to_tb(x, b, t):
    return x.reshape(b, t, -1).transpose(1, 0, 2).reshape(t * b, -1)


def _to_bt(x, b, t):
    return x.reshape(t, b, -1).transpose(1, 0, 2).reshape(b * t, -1)


def _blocked_buf(state):
    b, k1, c = state.shape
    return state.transpose(1, 0, 2).reshape(1, k1 * b, c)


def _unblocked(st, b):
    return st.reshape(-1, b, st.shape[-1]).transpose(1, 0, 2)


def layer_prompt(x, n_seq, t_len, lw):
    tiles = t_len // TM
    carry = ("carry", tiles)
    zeros = lambda c: jnp.zeros((n_seq, SUBLANES, c), F32)
    proj = dense(x, lw["w_in"], TM, N_PROJ, "in_proj")
    qkv, gconv = gdn_conv_silu(proj, lw["gdn_conv_w"], zeros(GDN_CONV_CH), TM, carry)
    y_b, s_new = gated_delta(qkv, proj, jnp.zeros((n_seq, GDN_HEADS, GDN_DK, GDN_DV), F32), lw["gdn_a_log"],
                             lw["gdn_dt_bias"], lw["gdn_norm_g"], GDN_CHUNK, True, t_len // GDN_STEP)
    y_c = nsa_prompt(proj, n_seq, t_len, lw["cmp"])
    x1, conva = mix_out_ln(proj, y_b, y_c, x, lw["w_out"], lw["conv_a_w"], lw["ln1_g"], lw["ln1_b"],
                           zeros(A_WIDTH), TM, carry)
    h, ffnc = ffn_up_act(x1, lw["w_up"], lw["ffn_conv_w"], zeros(D_FF), TM, D_FF, carry)
    x2 = dense_res_ln(h, lw["w_down"], x1, lw["ln2_g"], lw["ln2_b"], TM, "ffn_down_ln")
    p3 = proj.reshape(n_seq, t_len, N_PROJ)
    kv_new = p3[:, :, COL_KV:COL_KV + 4 * NSA_KV_DIM].reshape(n_seq, t_len, 4, NSA_KV_HEADS, NSA_HD)
    wkeep = min(WINDOW, t_len)
    win_state = p3[:, t_len - wkeep:, COL_WIN:COL_WIN + 2 * NSA_KV_DIM].reshape(n_seq, wkeep, 2, NSA_KV_HEADS, NSA_HD)
    tail = lambda st, k: st[:, SUBLANES - (k - 1):, :]
    return x2, (kv_new, win_state, tail(conva, A_CONV), tail(gconv, GDN_CONV), s_new, tail(ffnc, FFN_CONV))


def layer_sample(x, bsz, dec_t, lw, layer_idx, cache_kv, page_table, win_cache, st_conv_a, st_gdn_conv, st_gdn,
                 st_ffn_conv):
    m = dec_t * bsz
    blocked = ("blocked", bsz)
    proj = dense(x, lw["w_in"], TM, N_PROJ, "in_proj")
    proj_bt = _to_bt(proj, bsz, dec_t)
    qkv, gconv = gdn_conv_silu(proj, lw["gdn_conv_w"], _blocked_buf(st_gdn_conv), m, blocked)
    y_b, s_new = gated_delta(_to_bt(qkv, bsz, dec_t), proj_bt, st_gdn, lw["gdn_a_log"], lw["gdn_dt_bias"],
                             lw["gdn_norm_g"], dec_t, False, 1)
    p3 = proj_bt.reshape(bsz, dec_t, N_PROJ)
    kv_new = p3[:, :, COL_KV:COL_KV + 4 * NSA_KV_DIM]
    y_c, win_state = nsa_sample(p3[:, :, COL_Q:COL_Q + NSA_WIDTH], kv_new,
                                p3[:, :, COL_WIN:COL_WIN + 2 * NSA_KV_DIM], p3[:, :, COL_SMALL:COL_SMALL + 128],
                                cache_kv, layer_idx, page_table, win_cache, lw["cmp"])
    x1, conva = mix_out_ln(proj, _to_tb(y_b, bsz, dec_t), _to_tb(y_c, bsz, dec_t), x, lw["w_out"], lw["conv_a_w"],
                           lw["ln1_g"], lw["ln1_b"], _blocked_buf(st_conv_a), m, blocked)
    h, ffnc = ffn_up_act(x1, lw["w_up"], lw["ffn_conv_w"], _blocked_buf(st_ffn_conv), m, D_FF // 2, blocked)
    x2 = dense_res_ln(h, lw["w_down"], x1, lw["ln2_g"], lw["ln2_b"], TM, "ffn_down_ln")
    wlen = win_cache.shape[2]
    return x2, (kv_new.reshape(bsz, dec_t, 4, NSA_KV_HEADS, NSA_HD),
                win_state.reshape(bsz, 2, NSA_KV_HEADS, NSA_HD, wlen).transpose(0, 4, 1, 2, 3),
                _unblocked(conva, bsz), _unblocked(gconv, bsz), s_new, _unblocked(ffnc, bsz))


def stack_layers(states, i):
    return jnp.stack([s[i] for s in states], axis=0)


def kernel(x_prompt, x_sample, cache_nsa_kv, cache_nsa_win, state_conv_a, state_gdn_conv, state_gdn, state_ffn_conv, page_table, ln_emb_g, ln_emb_b, w_in, conv_a_w, gdn_conv_w, gdn_a_log, gdn_dt_bias, gdn_norm_g, cmp_pe, cmp_w1, cmp_b1, cmp_w2, w_out, ln1_g, ln1_b, w_up, ffn_conv_w, w_down, ln2_g, ln2_b):
    n_seq, t_len = x_prompt.shape[:2]
    dec_b, dec_t = x_sample.shape[:2]
    depth = w_in.shape[0]
    xp = layer_norm_rows(x_prompt.reshape(-1, D_MODEL), ln_emb_g, ln_emb_b)
    xs = layer_norm_rows(_to_tb(x_sample.reshape(-1, D_MODEL), dec_b, dec_t), ln_emb_g, ln_emb_b)
    w_in_b = permute_w_in(w_in.astype(BF16))
    w_out_b, w_up_b, w_down_b = (w.astype(BF16) for w in (w_out, w_up, w_down))
    cache_kv = cache_nsa_kv.transpose(0, 1, 3, 4, 5, 2).reshape(cache_nsa_kv.shape[:2] + (4 * NSA_KV_DIM, -1))
    win_cache = cache_nsa_win.transpose(0, 1, 3, 4, 5, 2).reshape(cache_nsa_win.shape[:2] + (2 * NSA_KV_DIM, -1))
    st_p, st_s = [], []
    for l in range(depth):
        lw = dict(w_in=w_in_b[l], conv_a_w=conv_a_w[l], gdn_conv_w=gdn_conv_w[l], gdn_a_log=gdn_a_log[l],
                  gdn_dt_bias=gdn_dt_bias[l], gdn_norm_g=gdn_norm_g[l],
                  cmp=compress_params(cmp_pe[l], cmp_w1[l], cmp_b1[l], cmp_w2[l]),
                  w_out=w_out_b[l], ln1_g=ln1_g[l], ln1_b=ln1_b[l], w_up=w_up_b[l], ffn_conv_w=ffn_conv_w[l],
                  w_down=w_down_b[l], ln2_g=ln2_g[l], ln2_b=ln2_b[l])
        xp, sp = layer_prompt(xp, n_seq, t_len, lw)
        xs, ss = layer_sample(xs, dec_b, dec_t, lw, l, cache_kv, page_table, win_cache[l], state_conv_a[l],
                              state_gdn_conv[l], state_gdn[l], state_ffn_conv[l])
        st_p.append(sp)
        st_s.append(ss)
    xp = xp.reshape(n_seq, t_len, D_MODEL)
    xs = _to_bt(xs, dec_b, dec_t).reshape(dec_b, dec_t, D_MODEL)
    return (xp, xs,
            stack_layers(st_p, 0), stack_layers(st_s, 0),
            stack_layers(st_p, 1), stack_layers(st_s, 1),
            stack_layers(st_p, 2), stack_layers(st_s, 2),
            stack_layers(st_p, 3), stack_layers(st_s, 3),
            stack_layers(st_p, 4), stack_layers(st_s, 4),
            stack_layers(st_p, 5), stack_layers(st_s, 5))
```

```python
import functools
import math

import jax
import jax.numpy as jnp
import numpy as np
from jax import lax
from jax.experimental import pallas as pl
from jax.experimental.pallas import tpu as pltpu

F32 = jnp.float32
BF16 = jnp.bfloat16

D_MODEL = 1024
DEPTH = 4
HEAD_DIM = 64
A_WIDTH = 256
A_CONV = 3
GDN_WIDTH = 256
GDN_HEADS = 4
GDN_DK = 64
GDN_DV = 64
GDN_QK = 256
GDN_CONV = 4
GDN_CONV_CH = 768
GDN_CHUNK = 64
NSA_WIDTH = 512
NSA_HEADS = 8
NSA_KV_HEADS = 2
NSA_GROUP = 4
NSA_HD = 64
NSA_KV_DIM = 128
CMP_STRIDE = 16
CMP_BLOCK = 32
SEL_BLOCK = 64
SEL_TOPN = 8
WINDOW = 512
Q_BLOCK = 128
D_FF = 2816
FFN_CONV = 3
ALPHA = (2.0 * DEPTH) ** 0.25
LN_EPS = 1e-5
IN_SIZES = (A_WIDTH, A_WIDTH, A_WIDTH, GDN_QK, GDN_QK, GDN_WIDTH, GDN_WIDTH, GDN_HEADS, GDN_HEADS,
            NSA_WIDTH, 4 * NSA_KV_DIM, 2 * NSA_KV_DIM, 3 * NSA_HEADS)
N_IN = sum(IN_SIZES)

VMEM_LIMIT_BYTES = 56 * 1024 * 1024


def _ln_rows(z, g, b):
    mu = jnp.mean(z, axis=-1, keepdims=True)
    zc = z - mu
    var = jnp.mean(zc * zc, axis=-1, keepdims=True)
    return zc * lax.rsqrt(var + LN_EPS) * g + b


def _ln_kernel(x_ref, g_ref, b_ref, o_ref):
    o_ref[...] = _ln_rows(x_ref[...], g_ref[...], b_ref[...])


def layer_norm_rows(x, g, b, tm=512):
    m, d = x.shape
    return pl.pallas_call(
        _ln_kernel,
        grid=(m // tm,),
        in_specs=[pl.BlockSpec((tm, d), lambda i: (i, 0)),
                  pl.BlockSpec((1, d), lambda i: (0, 0)),
                  pl.BlockSpec((1, d), lambda i: (0, 0))],
        out_specs=pl.BlockSpec((tm, d), lambda i: (i, 0)),
        out_shape=jax.ShapeDtypeStruct((m, d), F32),
        name="ln_rows",
    )(x, g.reshape(1, d), b.reshape(1, d))


def _dense_kernel(x_ref, w_ref, o_ref):
    o_ref[...] = jnp.dot(x_ref[...].astype(BF16), w_ref[...], preferred_element_type=F32)


def dense(x, w, tm, tn, name):
    m, k = x.shape
    n = w.shape[1]
    return pl.pallas_call(
        _dense_kernel,
        grid=(n // tn, m // tm),
        in_specs=[pl.BlockSpec((tm, k), lambda j, i: (i, 0)),
                  pl.BlockSpec((k, tn), lambda j, i: (0, j))],
        out_specs=pl.BlockSpec((tm, tn), lambda j, i: (i, j)),
        out_shape=jax.ShapeDtypeStruct((m, n), F32),
        compiler_params=pltpu.CompilerParams(vmem_limit_bytes=VMEM_LIMIT_BYTES),
        name=name,
    )(x, w)


def _dense_res_ln_kernel(y_ref, w_ref, x_ref, g_ref, b_ref, o_ref):
    acc = jnp.dot(y_ref[...].astype(BF16), w_ref[...], preferred_element_type=F32)
    o_ref[...] = _ln_rows(ALPHA * x_ref[...] + acc, g_ref[...], b_ref[...])


def dense_res_ln(y, w, x, g, b, tm, name):
    m, k = y.shape
    d = w.shape[1]
    return pl.pallas_call(
        _dense_res_ln_kernel,
        grid=(m // tm,),
        in_specs=[pl.BlockSpec((tm, k), lambda i: (i, 0)),
                  pl.BlockSpec((k, d), lambda i: (0, 0)),
                  pl.BlockSpec((tm, d), lambda i: (i, 0)),
                  pl.BlockSpec((1, d), lambda i: (0, 0)),
                  pl.BlockSpec((1, d), lambda i: (0, 0))],
        out_specs=pl.BlockSpec((tm, d), lambda i: (i, 0)),
        out_shape=jax.ShapeDtypeStruct((m, d), F32),
        compiler_params=pltpu.CompilerParams(vmem_limit_bytes=VMEM_LIMIT_BYTES),
        name=name,
    )(y, w, x, g.reshape(1, d), b.reshape(1, d))


SUBLANES = 8


def _conv_rows_carry(x, tail, w_ref, ksize):
    row8 = lax.broadcasted_iota(jnp.int32, (SUBLANES, x.shape[1]), 0)
    y = x * w_ref[ksize - 1:ksize, :]
    for k in range(1, ksize):
        rolled = pltpu.roll(x, k, axis=0)
        first = jnp.where(row8 < k, pltpu.roll(tail, k, axis=0), rolled[0:SUBLANES])
        y = y + jnp.concatenate([first, rolled[SUBLANES:]], axis=0) * w_ref[ksize - 1 - k:ksize - k, :]
    return y


def _conv_rows_blocked(x, buf, w_ref, ksize, step):
    r = x.shape[0]
    y = x * w_ref[ksize - 1:ksize, :]
    for k in range(1, ksize):
        prev = jnp.concatenate([buf[(ksize - 1 - k) * step:(ksize - 1) * step], x[0:r - k * step]], axis=0)
        y = y + prev * w_ref[ksize - 1 - k:ksize - k, :]
    return y


def _conv_start(buf_ref, tail_s, mode):
    kind, param = mode
    if kind == "carry":
        @pl.when(pl.program_id(0) % param == 0)
        def _sequence_start():
            tail_s[...] = buf_ref[0]


def _conv_tile(x, buf_ref, st_ref, tail_s, w_ref, ksize, mode, cols=slice(None)):
    kind, param = mode
    w = w_ref[:, cols]
    if kind == "carry":
        tm = x.shape[0]
        y = _conv_rows_carry(x, tail_s[:, cols], w, ksize)
        tail_s[:, cols] = x[tm - SUBLANES:tm]
        st_ref[0, :, cols] = x[tm - SUBLANES:tm]
        return y
    y = _conv_rows_blocked(x, buf_ref[0, :, cols], w, ksize, param)
    st_ref[0, :, cols] = x[x.shape[0] - (ksize - 1) * param:]
    return y


def _conv_specs(mode, ksize, c, tm):
    kind, param = mode
    if kind == "carry":
        rows = SUBLANES
        idx = lambda i: (i // param, 0, 0)
    else:
        rows = (ksize - 1) * param
        idx = lambda i: (0, 0, 0)
    return pl.BlockSpec((1, rows, c), idx), rows


def _mix_out_ln_kernel(ab_ref, ac_ref, ah_ref, yb_ref, yc_ref, x_ref, w_ref, cw_ref, g_ref, b_ref, buf_ref,
                       o_ref, st_ref, tail_s, *, mode):
    u = ac_ref[...] * ah_ref[...]
    _conv_start(buf_ref, tail_s, mode)
    z = _conv_tile(u, buf_ref, st_ref, tail_s, cw_ref, A_CONV, mode)
    y = jnp.concatenate([ab_ref[...] * z, yb_ref[...], yc_ref[...]], axis=1).astype(BF16)
    acc = jnp.dot(y, w_ref[...], preferred_element_type=F32)
    o_ref[...] = _ln_rows(ALPHA * x_ref[...] + acc, g_ref[...], b_ref[...])


def mix_out_ln(proj, y_b, y_c, x, w_out, conv_w, g, b, buf, tm, mode):
    m = x.shape[0]
    buf_spec, st_rows = _conv_specs(mode, A_CONV, A_WIDTH, tm)
    n_st = buf.shape[0]
    row = lambda c, w: pl.BlockSpec((tm, w), lambda i: (i, c // w))
    const = lambda shape: pl.BlockSpec(shape, lambda i: (0,) * len(shape))
    return pl.pallas_call(
        functools.partial(_mix_out_ln_kernel, mode=mode),
        grid=(m // tm,),
        in_specs=[row(COL_AB, A_WIDTH), row(COL_AC, A_WIDTH), row(COL_AH, A_WIDTH),
                  row(0, GDN_WIDTH), row(0, NSA_WIDTH), row(0, D_MODEL),
                  const((D_MODEL, D_MODEL)), const((A_CONV, A_WIDTH)), const((1, D_MODEL)), const((1, D_MODEL)),
                  buf_spec],
        out_specs=[row(0, D_MODEL), pl.BlockSpec((1, st_rows, A_WIDTH), buf_spec.index_map)],
        out_shape=[jax.ShapeDtypeStruct((m, D_MODEL), F32), jax.ShapeDtypeStruct((n_st, st_rows, A_WIDTH), F32)],
        scratch_shapes=[pltpu.VMEM((SUBLANES, A_WIDTH), F32)],
        compiler_params=pltpu.CompilerParams(dimension_semantics=("arbitrary",), vmem_limit_bytes=VMEM_LIMIT_BYTES),
        name="mix_out_ln",
    )(proj, proj, proj, y_b, y_c, x, w_out, conv_w, g.reshape(1, -1), b.reshape(1, -1), buf)


FFN_COL_CHUNK = 256


def _ffn_up_kernel(x_ref, wg_ref, wv_ref, cw_ref, buf_ref, h_ref, st_ref, tail_s, *, mode):
    xb = x_ref[...].astype(BF16)
    _conv_start(buf_ref, tail_s, mode)
    tn = h_ref.shape[1]
    cb = FFN_COL_CHUNK if tn % FFN_COL_CHUNK == 0 else tn
    for j in range(tn // cb):
        cols = slice(j * cb, (j + 1) * cb)
        gate = jnp.dot(xb, wg_ref[:, cols], preferred_element_type=F32)
        val = jnp.dot(xb, wv_ref[:, cols], preferred_element_type=F32)
        gate = _conv_tile(gate, buf_ref, st_ref, tail_s, cw_ref, FFN_CONV, mode, cols)
        h_ref[:, cols] = (jax.nn.silu(gate) * val).astype(BF16)


def ffn_up_act(x, w_up, conv_w, buf, tm, tn, mode):
    m = x.shape[0]
    kind, param = mode
    nj = D_FF // tn
    if kind == "carry":
        assert nj == 1
        st_rows, st_idx = SUBLANES, (lambda i, j: (i // param, 0, j))
    else:
        assert tm == m
        st_rows, st_idx = (FFN_CONV - 1) * param, (lambda i, j: (0, 0, j))
    n_st = buf.shape[0]
    once = pl.Buffered(1) if nj == 1 else None
    return pl.pallas_call(
        functools.partial(_ffn_up_kernel, mode=mode),
        grid=(m // tm, nj),
        in_specs=[pl.BlockSpec((tm, D_MODEL), lambda i, j: (i, 0)),
                  pl.BlockSpec((D_MODEL, tn), lambda i, j: (0, j), pipeline_mode=once),
                  pl.BlockSpec((D_MODEL, tn), lambda i, j: (0, nj + j), pipeline_mode=once),
                  pl.BlockSpec((FFN_CONV, tn), lambda i, j: (0, j)),
                  pl.BlockSpec((1, st_rows, tn), st_idx)],
        out_specs=[pl.BlockSpec((tm, tn), lambda i, j: (i, j)),
                   pl.BlockSpec((1, st_rows, tn), st_idx)],
        out_shape=[jax.ShapeDtypeStruct((m, D_FF), BF16), jax.ShapeDtypeStruct((n_st, st_rows, D_FF), F32)],
        scratch_shapes=[pltpu.VMEM((SUBLANES, tn), F32)],
        compiler_params=pltpu.CompilerParams(dimension_semantics=("arbitrary", "arbitrary"),
                                             vmem_limit_bytes=VMEM_LIMIT_BYTES),
        name="ffn_up_act",
    )(x, w_up, w_up, conv_w, buf)


def _conv_silu_kernel(x_ref, cw_ref, buf_ref, o_ref, st_ref, tail_s, *, mode):
    _conv_start(buf_ref, tail_s, mode)
    o_ref[...] = jax.nn.silu(_conv_tile(x_ref[...], buf_ref, st_ref, tail_s, cw_ref, GDN_CONV, mode))


def gdn_conv_silu(proj, conv_w, buf, tm, mode):
    m = proj.shape[0]
    buf_spec, st_rows = _conv_specs(mode, GDN_CONV, GDN_CONV_CH, tm)
    n_st = buf.shape[0]
    return pl.pallas_call(
        functools.partial(_conv_silu_kernel, mode=mode),
        grid=(m // tm,),
        in_specs=[pl.BlockSpec((tm, GDN_CONV_CH), lambda i: (i, COL_GQKV // GDN_CONV_CH)),
                  pl.BlockSpec((GDN_CONV, GDN_CONV_CH), lambda i: (0, 0)),
                  buf_spec],
        out_specs=[pl.BlockSpec((tm, GDN_CONV_CH), lambda i: (i, 0)),
                   pl.BlockSpec((1, st_rows, GDN_CONV_CH), buf_spec.index_map)],
        out_shape=[jax.ShapeDtypeStruct((m, GDN_CONV_CH), F32),
                   jax.ShapeDtypeStruct((n_st, st_rows, GDN_CONV_CH), F32)],
        scratch_shapes=[pltpu.VMEM((SUBLANES, GDN_CONV_CH), F32)],
        compiler_params=pltpu.CompilerParams(dimension_semantics=("arbitrary",), vmem_limit_bytes=VMEM_LIMIT_BYTES),
        name="gdn_conv_silu",
    )(proj, conv_w, buf)


GDN_ROWS = 128
GDN_STEP = 256


def _mm(a, b):
    return jnp.dot(a, b, preferred_element_type=F32)


def _split3(x):
    hi = x.astype(BF16)
    r1 = x - hi.astype(F32)
    mid = r1.astype(BF16)
    return hi, mid, (r1 - mid.astype(F32)).astype(BF16)


def _mm_exact_lhs(c, x):
    hi, mid, lo = _split3(x)
    return _mm(c, hi) + _mm(c, mid) + _mm(c, lo)


def _mm_exact_rhs(x, c):
    hi, mid, lo = _split3(x)
    return _mm(hi, c) + _mm(mid, c) + _mm(lo, c)


def _mm3(a, b):
    ah = a.astype(BF16)
    al = (a - ah.astype(F32)).astype(BF16)
    bh = b.astype(BF16)
    bl = (b - bh.astype(F32)).astype(BF16)
    return _mm(ah, bh) + _mm(ah, bl) + _mm(al, bh)


def _gdn_kernel(qkv_ref, gate_ref, sm_ref, alog_ref, dt_ref, gain_ref, ea_ref, eb_ref, tril_ref, cones_ref,
                hones_ref, s0_ref, y_ref, sout_ref, s_s, o_s, *, chunk, carry_state, tiles_per_seq):
    rws = GDN_ROWS
    ngrp = qkv_ref.shape[0] // rws
    nchunk = rws // chunk
    shift = chunk.bit_length() - 1
    nt = (((1,), (1,)), ((), ()))
    tn = (((0,), (0,)), ((), ()))

    if carry_state:
        @pl.when(pl.program_id(0) % tiles_per_seq == 0)
        def _sequence_start():
            s_s[...] = s0_ref[0]

    hones = hones_ref[...]

    def head_sum(x):
        xh = x.astype(BF16)
        return _mm(xh, hones) + _mm((x - xh.astype(F32)).astype(BF16), hones)

    qkv = qkv_ref[...]
    q, k, v = qkv[:, 0:GDN_QK], qkv[:, GDN_QK:2 * GDN_QK], qkv[:, 2 * GDN_QK:]
    q = q * lax.rsqrt(head_sum(q * q) + 1e-6) * (GDN_DK ** -0.5)
    k = k * lax.rsqrt(head_sum(k * k) + 1e-6)
    sm = sm_ref[...]
    g = -jnp.exp(alog_ref[...]) * jax.nn.softplus(_mm_exact_rhs(sm, ea_ref[...]) + dt_ref[...])
    beta = jax.nn.sigmoid(_mm_exact_rhs(sm, eb_ref[...]))
    gc = _mm_exact_lhs(tril_ref[...], g)
    gcl = _mm_exact_lhs(cones_ref[...], g)

    row = lax.broadcasted_iota(jnp.int32, (rws, rws), 0)
    col = lax.broadcasted_iota(jnp.int32, (rws, rws), 1)
    same = lax.shift_right_logical(row, shift) == lax.shift_right_logical(col, shift)
    eye = jnp.where(row == col, 1.0, 0.0)

    heads = range(GDN_HEADS)
    chains = [(slice(grp * rws, (grp + 1) * rws), h) for grp in range(ngrp) for h in heads]
    cid = lambda grp, h: grp * GDN_HEADS + h
    hsl = [slice(GDN_DK * h, GDN_DK * (h + 1)) for h in heads]
    incl = same & (row >= col)
    strict = same & (row > col)
    xs = [gc[rg, 128 * h:128 * (h + 1)] for rg, h in chains]
    gcols = [x[:, 0:GDN_DK] for x in xs]
    gends = [gcl[rg, 128 * h:128 * h + GDN_DK] for rg, h in chains]
    betas = [beta[rg, 128 * h:128 * h + GDN_DK] for rg, h in chains]
    decays = [jnp.where(incl, jnp.exp(jnp.where(incl, x - x.T, 0.0)), 0.0) for x in xs]
    kbs = [k[rg, hsl[h]] * betas[c] for c, (rg, h) in enumerate(chains)]
    khbs = [k[rg, hsl[h]].astype(BF16) for rg, h in chains]
    heads_all = range(len(chains))
    a_s = [jnp.where(strict, lax.dot_general(kbs[c].astype(BF16), khbs[c], nt, preferred_element_type=F32) * decays[c], 0.0)
           for c in heads_all]
    compact = lambda full: jnp.sum(full.reshape(nchunk, chunk, rws), axis=0)
    expand = lambda c: jnp.where(same, jnp.concatenate([c] * nchunk, axis=0), 0.0)
    eye_c = compact(eye)
    a_cs = [compact(a) for a in a_s]
    minv_cs = [eye_c - a_c for a_c in a_cs]
    apow_cs = [_mm3(a_cs[c], a_s[c]) for c in heads_all]
    for step in range(shift - 1):
        apow_fs = [expand(ap) for ap in apow_cs]
        minv_cs = [_mm3(minv_cs[c], eye + apow_fs[c]) for c in heads_all]
        if step < shift - 2:
            apow_cs = [_mm3(apow_cs[c], apow_fs[c]) for c in heads_all]
    minvs = [expand(m_c) for m_c in minv_cs]
    egcs = [jnp.exp(gc_h) for gc_h in gcols]
    us = [_mm3(minvs[c], v[rg, hsl[h]] * betas[c]) for c, (rg, h) in enumerate(chains)]
    wbs = [_mm3(minvs[c], kbs[c] * egcs[c]).astype(BF16) for c in heads_all]
    a_qks = [(lax.dot_general(q[rg, hsl[h]].astype(BF16), khbs[c], nt, preferred_element_type=F32) * decays[c]).astype(BF16)
             for c, (rg, h) in enumerate(chains)]
    q_decs = [(q[rg, hsl[h]] * egcs[c]).astype(BF16) for c, (rg, h) in enumerate(chains)]
    k_decs = [(k[rg, hsl[h]] * jnp.exp(gends[c] - gcols[c])).astype(BF16) for c, (rg, h) in enumerate(chains)]
    g_lasts = [jnp.exp(g_h) for g_h in gends]
    states = [s_s[h] for h in heads] if carry_state else None
    v_new = [[] for _ in heads_all]
    o_state = [[] for _ in heads_all]
    for n in range(ngrp * nchunk):
        grp, nl = divmod(n, nchunk)
        r = slice(nl * chunk, (nl + 1) * chunk)
        olds = states if carry_state else [s0_ref[n, h] for h in heads]
        sbs = [s_old.astype(BF16) for s_old in olds]
        vns = [us[cid(grp, h)][r] - _mm(wbs[cid(grp, h)][r], sbs[h]) for h in heads]
        for h in heads:
            o_state[cid(grp, h)].append(_mm(q_decs[cid(grp, h)][r], sbs[h]))
            v_new[cid(grp, h)].append(vns[h])
        news = [olds[h] * g_lasts[cid(grp, h)][nl * chunk:nl * chunk + 1, :]
                + lax.dot_general(k_decs[cid(grp, h)][r], vns[h].astype(BF16), tn, preferred_element_type=F32)
                for h in heads]
        if carry_state:
            states = news
        else:
            for h in heads:
                sout_ref[n, h] = news[h]
    for c, (rg, h) in enumerate(chains):
        v_all = jnp.concatenate(v_new[c], axis=0) if nchunk > 1 else v_new[c][0]
        o_all = jnp.concatenate(o_state[c], axis=0) if nchunk > 1 else o_state[c][0]
        o_s[rg, hsl[h]] = o_all + _mm(a_qks[c], v_all.astype(BF16))
    if carry_state:
        for h in heads:
            s_s[h] = states[h]

    if carry_state:
        sout_ref[0] = s_s[...]
    o = o_s[...]
    o = o * lax.rsqrt(head_sum(o * o) * (1.0 / GDN_DV) + 1e-6)
    y_ref[...] = o * gain_ref[...] * jax.nn.silu(gate_ref[...])


def gated_delta(qkv_act, proj, s0, a_log, dt_bias, norm_g, chunk, carry_state, tiles_per_seq):
    m = qkv_act.shape[0]
    rws = GDN_STEP
    idx = np.arange(rws)
    same = (idx[:, None] // chunk) == (idx[None, :] // chunk)
    tril = jnp.asarray(same & (idx[:, None] >= idx[None, :]), BF16)
    cones = jnp.asarray(same, BF16)
    lane = np.arange(GDN_WIDTH)
    hones = jnp.asarray(lane[:, None] // GDN_DK == lane[None, :] // GDN_DK, BF16)
    lane2 = np.arange(4 * 128) // 128
    smr = np.arange(128)
    ea = jnp.asarray(smr[:, None] == SM_GA + lane2[None, :], BF16)
    eb = jnp.asarray(smr[:, None] == SM_GB + lane2[None, :], BF16)
    alog_x = jnp.repeat(a_log.astype(F32), 128).reshape(1, 512)
    dt_x = jnp.repeat(dt_bias.astype(F32), 128).reshape(1, 512)
    gain_x = jnp.tile(norm_g.astype(F32), GDN_HEADS).reshape(1, GDN_WIDTH)
    if carry_state:
        s_blk, s_idx = (1, GDN_HEADS, GDN_DK, GDN_DV), (lambda i: (i // tiles_per_seq, 0, 0, 0))
    else:
        s_blk, s_idx = (rws // chunk, GDN_HEADS, GDN_DK, GDN_DV), (lambda i: (i, 0, 0, 0))
    const = lambda shape: pl.BlockSpec(shape, lambda i: (0,) * len(shape))
    return pl.pallas_call(
        functools.partial(_gdn_kernel, chunk=chunk, carry_state=carry_state, tiles_per_seq=tiles_per_seq),
        grid=(m // rws,),
        in_specs=[pl.BlockSpec((rws, GDN_CONV_CH), lambda i: (i, 0)),
                  pl.BlockSpec((rws, GDN_WIDTH), lambda i: (i, COL_GGATE // GDN_WIDTH)),
                  pl.BlockSpec((rws, 128), lambda i: (i, COL_SMALL // 128)),
                  const((1, 512)), const((1, 512)), const((1, GDN_WIDTH)),
                  const((128, 512)), const((128, 512)), const((rws, rws)), const((rws, rws)),
                  const((GDN_WIDTH, GDN_WIDTH)),
                  pl.BlockSpec(s_blk, s_idx)],
        out_specs=[pl.BlockSpec((rws, GDN_WIDTH), lambda i: (i, 0)), pl.BlockSpec(s_blk, s_idx)],
        out_shape=[jax.ShapeDtypeStruct((m, GDN_WIDTH), F32), jax.ShapeDtypeStruct(s0.shape, F32)],
        scratch_shapes=[pltpu.VMEM((GDN_HEADS, GDN_DK, GDN_DV), F32), pltpu.VMEM((rws, GDN_WIDTH), F32)],
        compiler_params=pltpu.CompilerParams(dimension_semantics=("arbitrary",), vmem_limit_bytes=VMEM_LIMIT_BYTES),
        name="gated_delta",
    )(qkv_act, proj, proj, alog_x, dt_x, gain_x, ea, eb, tril, cones, hones, s0)


COL_Q = 0
COL_KV = 512
COL_AB = 1024
COL_AC = 1280
COL_GQKV = 1536
COL_AH = 2304
COL_GGATE = 2560
COL_WIN = 2816
COL_SMALL = 3072
N_PROJ = 3200
SM_GA, SM_GB, SM_NG = 0, 4, 8


W_IN_ORDER = (9, 10, 0, 1, 3, 4, 5, 2, 6, 11)
W_IN_SMALL = (7, 8, 12)


def _permute_w_in_kernel(w_ref, o_ref):
    offs = np.concatenate([[0], np.cumsum(IN_SIZES)])
    grp = lambda k: w_ref[:, int(offs[k]):int(offs[k + 1])]
    col = 0
    for k in W_IN_ORDER:
        o_ref[:, col:col + IN_SIZES[k]] = grp(k).astype(BF16)
        col += IN_SIZES[k]
    rows = w_ref.shape[0]
    small = [grp(k) for k in W_IN_SMALL]
    small.append(jnp.zeros((rows, N_PROJ - col - sum(IN_SIZES[k] for k in W_IN_SMALL)), F32))
    o_ref[:, col:N_PROJ] = jnp.concatenate(small, axis=1).astype(BF16)


def permute_w_in(w_in, tk=256):
    depth, d, n_in = w_in.shape
    return pl.pallas_call(
        _permute_w_in_kernel,
        grid=(depth, d // tk),
        in_specs=[pl.BlockSpec((None, tk, n_in), lambda l, i: (l, i, 0))],
        out_specs=pl.BlockSpec((None, tk, N_PROJ), lambda l, i: (l, i, 0)),
        out_shape=jax.ShapeDtypeStruct((depth, d, N_PROJ), BF16),
        name="permute_w_in",
    )(w_in)


NEG = -1e30
KEY_CHUNK = 128
SEL_CHUNK = 256


def _block_diag4(a, b):
    z = jnp.zeros_like(a)
    rows = [jnp.concatenate([a, z, z, z], -1), jnp.concatenate([z, a, z, z], -1),
            jnp.concatenate([z, z, b, z], -1), jnp.concatenate([z, z, z, b], -1)]
    return jnp.concatenate(rows, -2)


def compress_params(cmp_pe, cmp_w1, cmp_b1, cmp_w2):
    w1 = _block_diag4(cmp_w1[0], cmp_w1[1]).astype(BF16)
    w2 = _block_diag4(cmp_w2[0], cmp_w2[1]).astype(BF16)
    pe = jnp.concatenate([cmp_pe[0], cmp_pe[0], cmp_pe[1], cmp_pe[1]], -1)
    b1 = jnp.concatenate([cmp_b1[0], cmp_b1[0], cmp_b1[1], cmp_b1[1]], -1).reshape(1, 256)
    return w1, pe, b1, w2


def _compress_rows(row_loader, ncp, w1_ref, pe_ref, b1_ref, w2_ref):
    acc_lo = jnp.zeros((ncp, 256), F32)
    acc_hi = jnp.zeros((ncp, 256), F32)
    for l in range(CMP_STRIDE):
        x = row_loader(l)
        acc_lo += jnp.dot((x + pe_ref[l:l + 1, :]).astype(BF16), w1_ref[l], preferred_element_type=F32)
        acc_hi += jnp.dot((x + pe_ref[l + CMP_STRIDE:l + CMP_STRIDE + 1, :]).astype(BF16),
                          w1_ref[l + CMP_STRIDE], preferred_element_type=F32)
    hid = jax.nn.gelu(acc_lo + pltpu.roll(acc_hi, ncp - 1, axis=0) + b1_ref[...])
    return jnp.dot(hid.astype(BF16), w2_ref[...], preferred_element_type=F32)


def _cmp_attend(q_rows, kc, vc, pos, ovl, n_heads, tq, ncp):
    s = lax.dot_general(q_rows, kc, (((1,), (1,)), ((), ())), preferred_element_type=F32)
    cend = lax.broadcasted_iota(jnp.int32, (tq, ncp), 1) * CMP_STRIDE + (CMP_BLOCK - 1)
    s3 = jnp.where((cend <= pos)[None], s.reshape(n_heads, tq, ncp), -jnp.inf)
    m = jnp.max(s3, axis=-1, keepdims=True)
    m = jnp.where(m == -jnp.inf, 0.0, m)
    p = jnp.exp(s3 - m)
    pn = p / jnp.maximum(jnp.sum(p, axis=-1, keepdims=True), 1e-30)
    o_cmp = jnp.dot(pn.reshape(n_heads * tq, ncp).astype(BF16), vc, preferred_element_type=F32)
    psum = pn[0]
    for g in range(1, n_heads):
        psum = psum + pn[g]
    p_hi = psum.astype(BF16)
    p_lo = (psum - p_hi.astype(F32)).astype(BF16)
    imp = (jnp.dot(p_hi, ovl, preferred_element_type=F32) + jnp.dot(p_lo, ovl, preferred_element_type=F32))
    return o_cmp, imp


def _select_topn(imp, pos, fillers=()):
    fillers = list(fillers)
    r, ns = imp.shape
    blk = lax.broadcasted_iota(jnp.int32, (r, ns), 1)
    blk_f = blk.astype(F32)
    forced = (blk == 0) | (blk == lax.shift_right_logical(pos, 6))
    valid = blk * SEL_BLOCK <= pos
    v = jnp.where(forced, jnp.inf, jnp.where(valid, imp, -jnp.inf))
    sel = jnp.zeros((r, ns), F32)
    for _ in range(min(SEL_TOPN, ns)):
        mx = jnp.max(v, axis=-1, keepdims=True)
        idx = jnp.min(jnp.where(v == mx, blk_f, float(ns)), axis=-1, keepdims=True)
        hit = blk_f == idx
        sel = jnp.where(hit, 1.0, sel)
        v = jnp.where(hit, -jnp.inf, v)
        if fillers:
            fillers.pop(0)()
    for f in fillers:
        f()
    return sel


def _nsa_prompt_kernel(q_ref, cmp_ref, slc_ref, win_ref, sm_ref, w1_ref, pe_ref, b1_ref, w2_ref, ovl_ref, exp_ref,
                       gexp_ref, y_ref, kc_s, vc_s, ks_s, vs_s, kw_s, vw_s, q_s, sel_s, ocmp_s, owin_s, m_s, acc_s, *, t_len):
    tq = KEY_CHUNK
    i = pl.program_id(1)
    ncp = t_len // CMP_STRIDE
    ns = t_len // SEL_BLOCK
    ng = NSA_GROUP
    rows = ng * tq

    @pl.when(i == 0)
    def _prepare_sequence():
        kvc = _compress_rows(lambda l: cmp_ref[0, l], ncp, w1_ref, pe_ref, b1_ref, w2_ref)
        kc_s[...] = kvc[:, 0:128].astype(BF16)
        vc_s[...] = kvc[:, 128:256].astype(BF16)
        ones = jnp.ones((512, 64), BF16)

        def cast_rows(r, carry):
            sl = pl.ds(pl.multiple_of(r * 512, 512), 512)
            for h in range(NSA_KV_HEADS):
                ks_s[h, sl, :] = slc_ref[sl, 64 * h:64 * h + 64].astype(BF16)
                vs_s[h, sl, 0:64] = slc_ref[sl, 128 + 64 * h:192 + 64 * h].astype(BF16)
                vs_s[h, sl, 64:128] = ones
                kw_s[h, sl, :] = win_ref[sl, 64 * h:64 * h + 64].astype(BF16)
                vw_s[h, sl, 0:64] = win_ref[sl, 128 + 64 * h:192 + 64 * h].astype(BF16)
                vw_s[h, sl, 64:128] = ones
            return carry

        lax.fori_loop(0, t_len // 512, cast_rows, 0)

    pos = i * tq + lax.broadcasted_iota(jnp.int32, (tq, 1), 0)
    gates = jax.nn.sigmoid(sm_ref[...])
    g_hi = gates.astype(BF16)
    gx = _mm(g_hi, gexp_ref[...]) + _mm((gates - g_hi.astype(F32)).astype(BF16), gexp_ref[...])
    nt = (((1,), (1,)), ((), ()))

    wk = min(WINDOW + KEY_CHUNK, t_len)
    w0 = pl.multiple_of(jnp.minimum(jnp.maximum(i - WINDOW // KEY_CHUNK, 0), (t_len - wk) // KEY_CHUNK) * KEY_CHUNK,
                        KEY_CHUNK)
    wpos = w0 + lax.broadcasted_iota(jnp.int32, (tq, wk), 1)
    wbias = jnp.where(wpos <= pos, jnp.where(wpos > pos - WINDOW, 0.0, NEG), NEG)

    hs = range(NSA_KV_HEADS)
    q_rows = []
    for h in hs:
        qh = q_ref[:, 256 * h:256 * h + 256] * (NSA_HD ** -0.5)
        q_rows.append(jnp.concatenate([qh[:, 64 * g:64 * g + 64] for g in range(ng)], axis=0).astype(BF16))
        q_s[h] = q_rows[h]
    cmp = [_cmp_attend(q_rows[h], kc_s[:, 64 * h:64 * h + 64], vc_s[:, 64 * h:64 * h + 64], pos, ovl_ref[...],
                       ng, tq, ncp) for h in hs]
    sw = [lax.dot_general(q_rows[h], kw_s[h, pl.ds(w0, wk), :], nt, preferred_element_type=F32) for h in hs]
    sw = [(s.reshape(ng, tq, wk) + wbias[None]).reshape(rows, wk) for s in sw]
    pw = [jnp.exp(s - jnp.max(s, axis=-1, keepdims=True)).astype(BF16) for s in sw]
    aw = [jnp.dot(pw[h], vw_s[h, pl.ds(w0, wk), :], preferred_element_type=F32) for h in hs]
    sel = _select_topn(jnp.concatenate([c[1] for c in cmp], axis=0), jnp.concatenate([pos] * NSA_KV_HEADS, axis=0))
    for h in hs:
        ocmp_s[h] = cmp[h][0]
        sel_s[h] = sel[h * tq:(h + 1) * tq].astype(BF16)
        owin_s[h] = aw[h][:, 0:64] / aw[h][:, 64:128]
        m_s[h] = jnp.full((rows, 128), NEG, F32)
        acc_s[h] = jnp.zeros((rows, 128), F32)

    lane = lax.broadcasted_iota(jnp.int32, (tq, SEL_CHUNK), 1)

    def sel_body(j, carry):
        off = pl.multiple_of(j * SEL_CHUNK, SEL_CHUNK)
        hs = range(NSA_KV_HEADS)
        pairs = [(h, slice(g * tq, (g + 1) * tq)) for h in hs for g in range(ng)]
        ks = [ks_s[h, pl.ds(off, SEL_CHUNK), :] for h in hs]
        vs = [vs_s[h, pl.ds(off, SEL_CHUNK), :] for h in hs]
        causal = off + lane <= pos
        biases = [jnp.where(causal, (jnp.dot(sel_s[h], exp_ref[:, pl.ds(off, SEL_CHUNK)],
                                            preferred_element_type=F32) - 1.0) * (-NEG), NEG) for h in hs]
        ss = [lax.dot_general(q_s[h, r, :], ks[h], nt, preferred_element_type=F32) + biases[h] for h, r in pairs]
        m_prevs = [m_s[h, r, :] for h, r in pairs]
        m_news = [jnp.maximum(mp, jnp.max(s, axis=-1, keepdims=True)) for mp, s in zip(m_prevs, ss)]
        ps = [jnp.exp(s - jnp.concatenate([mn] * (SEL_CHUNK // 128), axis=1)).astype(BF16)
              for s, mn in zip(ss, m_news)]
        pvs = [jnp.dot(p, vs[h], preferred_element_type=F32) for p, (h, r) in zip(ps, pairs)]
        for (h, r), mp, mn, pv in zip(pairs, m_prevs, m_news, pvs):
            acc_s[h, r, :] = jnp.exp(mp - mn) * acc_s[h, r, :] + pv
            m_s[h, r, :] = mn
        return carry

    lax.fori_loop(0, (i * tq) // SEL_CHUNK + 1, sel_body, 0)

    for h in range(NSA_KV_HEADS):
        acc = acc_s[h]
        o_slc = acc[:, 0:64] / acc[:, 64:128]
        o_cmp = ocmp_s[h]
        o_win = owin_s[h]
        for g in range(ng):
            hh = ng * h + g
            r = slice(g * tq, (g + 1) * tq)
            gate = lambda branch: gx[:, 128 * (NSA_HEADS * branch + hh):128 * (NSA_HEADS * branch + hh) + 64]
            y_ref[:, 64 * hh:64 * hh + 64] = gate(0) * o_cmp[r] + gate(1) * o_slc[r] + gate(2) * o_win[r]


def nsa_prompt(proj, n_seq, t_len, cparams):
    tq = KEY_CHUNK
    nt = t_len // tq
    ncp = t_len // CMP_STRIDE
    ns = t_len // SEL_BLOCK
    w1, pe, b1, w2 = cparams
    cstart = np.arange(ncp)[:, None] * CMP_STRIDE
    sstart = np.arange(ns)[None, :] * SEL_BLOCK
    ovl = jnp.asarray((cstart < sstart + SEL_BLOCK) & (cstart + CMP_BLOCK > sstart), BF16)
    expand = jnp.asarray(np.arange(t_len)[None, :] // SEL_BLOCK == np.arange(ns)[:, None], BF16)
    gexp = jnp.asarray(np.arange(128)[:, None] == SM_NG + np.arange(3 * NSA_HEADS * 128)[None, :] // 128, BF16)
    cmp_rows = proj[:, COL_KV:COL_KV + 256].reshape(n_seq, ncp, CMP_STRIDE, 256).transpose(0, 2, 1, 3)
    once = pl.Buffered(1)
    const = lambda shape: pl.BlockSpec(shape, lambda b, i: (0,) * len(shape), pipeline_mode=once)
    return pl.pallas_call(
        functools.partial(_nsa_prompt_kernel, t_len=t_len),
        grid=(n_seq, nt),
        in_specs=[pl.BlockSpec((tq, 512), lambda b, i: (b * nt + i, COL_Q // 512)),
                  pl.BlockSpec((1, CMP_STRIDE, ncp, 256), lambda b, i: (b, 0, 0, 0), pipeline_mode=once),
                  pl.BlockSpec((t_len, 256), lambda b, i: (b, (COL_KV + 256) // 256), pipeline_mode=once),
                  pl.BlockSpec((t_len, 256), lambda b, i: (b, COL_WIN // 256), pipeline_mode=once),
                  pl.BlockSpec((tq, 128), lambda b, i: (b * nt + i, COL_SMALL // 128)),
                  const((CMP_BLOCK, 256, 256)), const((CMP_BLOCK, 256)), const((1, 256)), const((256, 256)),
                  const((ncp, ns)), const((ns, t_len)), const((128, 3 * NSA_HEADS * 128))],
        out_specs=pl.BlockSpec((tq, NSA_WIDTH), lambda b, i: (b * nt + i, 0)),
        out_shape=jax.ShapeDtypeStruct((n_seq * t_len, NSA_WIDTH), F32),
        scratch_shapes=[pltpu.VMEM((ncp, 128), BF16), pltpu.VMEM((ncp, 128), BF16),
                        pltpu.VMEM((NSA_KV_HEADS, t_len, 64), BF16), pltpu.VMEM((NSA_KV_HEADS, t_len, 128), BF16),
                        pltpu.VMEM((NSA_KV_HEADS, t_len, 64), BF16), pltpu.VMEM((NSA_KV_HEADS, t_len, 128), BF16),
                        pltpu.VMEM((NSA_KV_HEADS, NSA_GROUP * tq, 64), BF16), pltpu.VMEM((NSA_KV_HEADS, tq, ns), BF16),
                        pltpu.VMEM((NSA_KV_HEADS, NSA_GROUP * tq, 64), F32),
                        pltpu.VMEM((NSA_KV_HEADS, NSA_GROUP * tq, 64), F32),
                        pltpu.VMEM((NSA_KV_HEADS, NSA_GROUP * tq, 128), F32),
                        pltpu.VMEM((NSA_KV_HEADS, NSA_GROUP * tq, 128), F32)],
        compiler_params=pltpu.CompilerParams(dimension_semantics=("arbitrary", "arbitrary"),
                                             vmem_limit_bytes=VMEM_LIMIT_BYTES),
        name="nsa_prompt",
    )(proj, cmp_rows, proj, proj, proj, w1, pe, b1, w2, ovl, expand, gexp)


PAGE_ROWS = 128
SEL_LANES = 128


def _softmax_segments(segs):
    m = segs[0].max(axis=-1, keepdims=True)
    for s in segs[1:]:
        m = jnp.maximum(m, s.max(axis=-1, keepdims=True))
    ps = [jnp.exp(s - m) for s in segs]
    den = ps[0].sum(axis=-1, keepdims=True)
    for p in ps[1:]:
        den = den + p.sum(axis=-1, keepdims=True)
    return ps, den


def _nsa_sample_kernel(pt_ref, *refs, n_pages, past_len, dec_t):
    del pt_ref
    pages = refs[:n_pages]
    (q_ref, kvn_ref, wn_ref, sm_ref, wc_ref, w1_ref, pe_ref, b1_ref, w2_ref, ovl_ref, exp_ref,
     y_ref, wst_ref, rk_s, rv_s, newpg_s, neww_s) = refs[n_pages:]
    ncp = past_len // CMP_STRIDE
    ng, nh = NSA_GROUP, NSA_KV_HEADS
    rows_h = ng * dec_t
    rows = nh * rows_h
    nt = (((1,), (1,)), ((), ()))

    for p in range(n_pages):
        rk_s[p * PAGE_ROWS:(p + 1) * PAGE_ROWS, :] = pages[p][0:128, :].T
        rv_s[p * PAGE_ROWS:(p + 1) * PAGE_ROWS, :] = pages[p][128:256, :].T
    kvc = _compress_rows(
        lambda l: jnp.concatenate([rk_s[pl.ds(l, ncp, stride=CMP_STRIDE), :],
                                   rv_s[pl.ds(l, ncp, stride=CMP_STRIDE), :]], axis=1),
        ncp, w1_ref, pe_ref, b1_ref, w2_ref)
    kc = kvc[:, 0:128].astype(BF16)
    vc = kvc[:, 128:256].astype(BF16)

    newpg_s[...] = jnp.zeros((PAGE_ROWS, 256), F32)
    newpg_s[0:dec_t, :] = kvn_ref[0][:, 256:512]
    neww_s[...] = jnp.zeros((PAGE_ROWS, 256), F32)
    neww_s[0:dec_t, :] = wn_ref[0]

    qf = q_ref[0] * (NSA_HD ** -0.5)
    zero = jnp.zeros((dec_t, 64), F32)
    qrows = []
    for h in range(nh):
        for g in range(ng):
            piece = qf[:, 64 * (ng * h + g):64 * (ng * h + g) + 64]
            qrows.append(jnp.concatenate([piece, zero] if h == 0 else [zero, piece], axis=1))
    q_bd = jnp.concatenate(qrows, axis=0).astype(BF16)
    t_row = lax.broadcasted_iota(jnp.int32, (rows, 1), 0) & (dec_t - 1)
    pos = past_len + t_row
    head0 = lax.broadcasted_iota(jnp.int32, (rows, 128), 0) < rows_h
    lane_lo = lax.broadcasted_iota(jnp.int32, (rows, 128), 1) < 64
    own = head0 == lane_lo

    def own_half(x):
        x = jnp.where(own, x, 0.0)
        return x[:, 0:64] + x[:, 64:128]

    s = lax.dot_general(q_bd, kc, nt, preferred_element_type=F32)
    cend = lax.broadcasted_iota(jnp.int32, (rows, ncp), 1) * CMP_STRIDE + (CMP_BLOCK - 1)
    s = jnp.where(cend <= pos, s, -jnp.inf)
    m = jnp.max(s, axis=-1, keepdims=True)
    m = jnp.where(m == -jnp.inf, 0.0, m)
    p = jnp.exp(s - m)
    pn = p / jnp.maximum(jnp.sum(p, axis=-1, keepdims=True), 1e-30)
    o_cmp = own_half(_mm(pn.astype(BF16), vc))

    ovl = ovl_ref[...]
    psums = []
    for h in range(nh):
        psum = pn[h * rows_h:h * rows_h + dec_t]
        for g in range(1, ng):
            psum = psum + pn[h * rows_h + g * dec_t:h * rows_h + (g + 1) * dec_t]
        psums.append(psum)
    psum = jnp.concatenate(psums, axis=0)
    p_hi = psum.astype(BF16)
    imp = _mm(p_hi, ovl) + _mm((psum - p_hi.astype(F32)).astype(BF16), ovl)
    pos_ht = past_len + (lax.broadcasted_iota(jnp.int32, (nh * dec_t, 1), 0) & (dec_t - 1))
    lane = lax.broadcasted_iota(jnp.int32, (rows, PAGE_ROWS), 1)
    wlen = wc_ref.shape[2]
    wc = wc_ref[0]
    branch = {}
    scores = [None] * (n_pages + 1)

    def window_branch():
        jw = lax.broadcasted_iota(jnp.int32, (rows, wlen), 1)
        s_old = _mm(q_bd, wc[0:128, :].astype(BF16))
        s_old = jnp.where(past_len - wlen + jw > pos - WINDOW, s_old, NEG)
        s_new = lax.dot_general(q_bd, neww_s[:, 0:128].astype(BF16), nt, preferred_element_type=F32)
        s_new = jnp.where(past_len + lane <= pos, s_new, NEG)
        pw, den_w = _softmax_segments([s_old, s_new])
        acc_w = (lax.dot_general(pw[0].astype(BF16), wc[128:256, :].astype(BF16), nt, preferred_element_type=F32)
                 + _mm(pw[1].astype(BF16), neww_s[:, 128:256].astype(BF16)))
        branch["win"] = own_half(acc_w) / den_w

    def page_scores(lo, hi):
        def run():
            for pg in range(lo, hi):
                if pg < n_pages:
                    scores[pg] = _mm(q_bd, pages[pg][256:384, :].astype(BF16))
                else:
                    scores[pg] = lax.dot_general(q_bd, newpg_s[:, 0:128].astype(BF16), nt, preferred_element_type=F32)
        return run

    per = -(-(n_pages + 1) // (SEL_TOPN - 1))
    fillers = [window_branch] + [page_scores(lo, min(lo + per, n_pages + 1)) for lo in range(0, n_pages + 1, per)]
    sel = _select_topn(imp, pos_ht, fillers)
    sel_b = jnp.concatenate([sel[h * dec_t:(h + 1) * dec_t] for h in range(nh) for _ in range(ng)],
                            axis=0).astype(BF16)

    segs = []
    for pg in range(n_pages + 1):
        chosen = _mm(sel_b, exp_ref[:, pg * PAGE_ROWS:(pg + 1) * PAGE_ROWS])
        ok = chosen > 0.5
        if pg == n_pages:
            ok = ok & (past_len + lane <= pos)
        segs.append(jnp.where(ok, scores[pg], NEG))
    ps, den = _softmax_segments(segs)
    acc = jnp.zeros((rows, 128), F32)
    for pg in range(n_pages + 1):
        if pg < n_pages:
            acc = acc + lax.dot_general(ps[pg].astype(BF16), pages[pg][384:512, :].astype(BF16), nt,
                                        preferred_element_type=F32)
        else:
            acc = acc + _mm(ps[pg].astype(BF16), newpg_s[:, 128:256].astype(BF16))
    o_slc = own_half(acc) / den
    o_win = branch["win"]

    gates = jax.nn.sigmoid(sm_ref[0][:, SM_NG:SM_NG + 3 * NSA_HEADS])
    for hh in range(NSA_HEADS):
        r = slice(hh * dec_t, (hh + 1) * dec_t)
        y_ref[0, :, 64 * hh:64 * hh + 64] = (gates[:, hh:hh + 1] * o_cmp[r]
                                             + gates[:, NSA_HEADS + hh:NSA_HEADS + hh + 1] * o_slc[r]
                                             + gates[:, 2 * NSA_HEADS + hh:2 * NSA_HEADS + hh + 1] * o_win[r])

    rolled = pltpu.roll(wc, wlen - dec_t, axis=1)
    new_t = jnp.concatenate([neww_s[:, 0:128].T, neww_s[:, 128:256].T], axis=0)
    lane_w = lax.broadcasted_iota(jnp.int32, (256, 128), 1)
    last = jnp.where(lane_w >= 128 - dec_t, pltpu.roll(new_t, 128 - dec_t, axis=1), rolled[:, wlen - 128:wlen])
    wst_ref[0, :, 0:wlen - 128] = rolled[:, 0:wlen - 128]
    wst_ref[0, :, wlen - 128:wlen] = last


def nsa_sample(q, kv_new, win_new, sm, cache_kv, layer_idx, page_table, win_cache, cparams):
    bsz, dec_t, _ = q.shape
    n_pages = page_table.shape[1]
    past_len = n_pages * PAGE_ROWS
    wlen = win_cache.shape[2]
    assert dec_t < CMP_STRIDE and dec_t % SUBLANES == 0 and dec_t & (dec_t - 1) == 0
    assert wlen == WINDOW and wlen <= past_len
    ncp = past_len // CMP_STRIDE
    ns = -(-(past_len + dec_t) // SEL_BLOCK)
    assert ns <= SEL_LANES
    w1, pe, b1, w2 = cparams
    cstart = np.arange(ncp)[:, None] * CMP_STRIDE
    sstart = np.arange(SEL_LANES)[None, :] * SEL_BLOCK
    ovl = jnp.asarray((cstart < sstart + SEL_BLOCK) & (cstart + CMP_BLOCK > sstart) & (np.arange(SEL_LANES)[None, :] < ns), BF16)
    keys = np.arange((n_pages + 1) * PAGE_ROWS)
    expand = jnp.asarray(keys[None, :] // SEL_BLOCK == np.arange(SEL_LANES)[:, None], BF16)
    const = lambda shape: pl.BlockSpec(shape, lambda b, pt: (0,) * len(shape))
    seq = lambda r, c: pl.BlockSpec((1, r, c), lambda b, pt: (b, 0, 0))
    page_specs = [pl.BlockSpec((None, None, 512, PAGE_ROWS), functools.partial(
        lambda b, pt, p: (layer_idx, pt[b, p], 0, 0), p=p)) for p in range(n_pages)]
    grid_spec = pltpu.PrefetchScalarGridSpec(
        num_scalar_prefetch=1,
        grid=(bsz,),
        in_specs=page_specs + [seq(dec_t, 512), seq(dec_t, 512), seq(dec_t, 256), seq(dec_t, 128), seq(256, wlen),
                               const((CMP_BLOCK, 256, 256)), const((CMP_BLOCK, 256)), const((1, 256)),
                               const((256, 256)), const((ncp, SEL_LANES)),
                               const((SEL_LANES, (n_pages + 1) * PAGE_ROWS))],
        out_specs=[seq(dec_t, NSA_WIDTH), seq(256, wlen)],
        scratch_shapes=[pltpu.VMEM((past_len, 128), F32), pltpu.VMEM((past_len, 128), F32),
                        pltpu.VMEM((PAGE_ROWS, 256), F32), pltpu.VMEM((PAGE_ROWS, 256), F32)])
    return pl.pallas_call(
        functools.partial(_nsa_sample_kernel, n_pages=n_pages, past_len=past_len, dec_t=dec_t),
        grid_spec=grid_spec,
        out_shape=[jax.ShapeDtypeStruct((bsz, dec_t, NSA_WIDTH), F32), jax.ShapeDtypeStruct((bsz, 256, wlen), F32)],
        compiler_params=pltpu.CompilerParams(dimension_semantics=("arbitrary",), vmem_limit_bytes=VMEM_LIMIT_BYTES),
        name="nsa_sample",
    )(page_table, *([cache_kv] * n_pages), q, kv_new, win_new, sm, win_cache, w1, pe, b1, w2, ovl, expand)


TM = 512


def _to_tb(x, b, t):
    return x.reshape(b, t, -1).transpose(1, 0, 2).reshape(t * b, -1)


def _to_bt(x, b, t):
    return x.reshape(t, b, -1).transpose(1, 0, 2).reshape(b * t, -1)


def _blocked_buf(state):
    b, k1, c = state.shape
    return state.transpose(1, 0, 2).reshape(1, k1 * b, c)


def _unblocked(st, b):
    return st.reshape(-1, b, st.shape[-1]).transpose(1, 0, 2)


def layer_prompt(x, n_seq, t_len, lw):
    tiles = t_len // TM
    carry = ("carry", tiles)
    zeros = lambda c: jnp.zeros((n_seq, SUBLANES, c), F32)
    proj = dense(x, lw["w_in"], TM, N_PROJ, "in_proj")
    qkv, gconv = gdn_conv_silu(proj, lw["gdn_conv_w"], zeros(GDN_CONV_CH), TM, carry)
    y_b, s_new = gated_delta(qkv, proj, jnp.zeros((n_seq, GDN_HEADS, GDN_DK, GDN_DV), F32), lw["gdn_a_log"],
                             lw["gdn_dt_bias"], lw["gdn_norm_g"], GDN_CHUNK, True, t_len // GDN_STEP)
    y_c = nsa_prompt(proj, n_seq, t_len, lw["cmp"])
    x1, conva = mix_out_ln(proj, y_b, y_c, x, lw["w_out"], lw["conv_a_w"], lw["ln1_g"], lw["ln1_b"],
                           zeros(A_WIDTH), TM, carry)
    h, ffnc = ffn_up_act(x1, lw["w_up"], lw["ffn_conv_w"], zeros(D_FF), TM, D_FF, carry)
    x2 = dense_res_ln(h, lw["w_down"], x1, lw["ln2_g"], lw["ln2_b"], TM, "ffn_down_ln")
    p3 = proj.reshape(n_seq, t_len, N_PROJ)
    kv_new = p3[:, :, COL_KV:COL_KV + 4 * NSA_KV_DIM].reshape(n_seq, t_len, 4, NSA_KV_HEADS, NSA_HD)
    wkeep = min(WINDOW, t_len)
    win_state = p3[:, t_len - wkeep:, COL_WIN:COL_WIN + 2 * NSA_KV_DIM].reshape(n_seq, wkeep, 2, NSA_KV_HEADS, NSA_HD)
    tail = lambda st, k: st[:, SUBLANES - (k - 1):, :]
    return x2, (kv_new, win_state, tail(conva, A_CONV), tail(gconv, GDN_CONV), s_new, tail(ffnc, FFN_CONV))


def layer_sample(x, bsz, dec_t, lw, layer_idx, cache_kv, page_table, win_cache, st_conv_a, st_gdn_conv, st_gdn,
                 st_ffn_conv):
    m = dec_t * bsz
    blocked = ("blocked", bsz)
    proj = dense(x, lw["w_in"], TM, N_PROJ, "in_proj")
    proj_bt = _to_bt(proj, bsz, dec_t)
    qkv, gconv = gdn_conv_silu(proj, lw["gdn_conv_w"], _blocked_buf(st_gdn_conv), m, blocked)
    y_b, s_new = gated_delta(_to_bt(qkv, bsz, dec_t), proj_bt, st_gdn, lw["gdn_a_log"], lw["gdn_dt_bias"],
                             lw["gdn_norm_g"], dec_t, False, 1)
    p3 = proj_bt.reshape(bsz, dec_t, N_PROJ)
    kv_new = p3[:, :, COL_KV:COL_KV + 4 * NSA_KV_DIM]
    y_c, win_state = nsa_sample(p3[:, :, COL_Q:COL_Q + NSA_WIDTH], kv_new,
                                p3[:, :, COL_WIN:COL_WIN + 2 * NSA_KV_DIM], p3[:, :, COL_SMALL:COL_SMALL + 128],
                                cache_kv, layer_idx, page_table, win_cache, lw["cmp"])
    x1, conva = mix_out_ln(proj, _to_tb(y_b, bsz, dec_t), _to_tb(y_c, bsz, dec_t), x, lw["w_out"], lw["conv_a_w"],
                           lw["ln1_g"], lw["ln1_b"], _blocked_buf(st_conv_a), m, blocked)
    h, ffnc = ffn_up_act(x1, lw["w_up"], lw["ffn_conv_w"], _blocked_buf(st_ffn_conv), m, D_FF // 2, blocked)
    x2 = dense_res_ln(h, lw["w_down"], x1, lw["ln2_g"], lw["ln2_b"], TM, "ffn_down_ln")
    wlen = win_cache.shape[2]
    return x2, (kv_new.reshape(bsz, dec_t, 4, NSA_KV_HEADS, NSA_HD),
                win_state.reshape(bsz, 2, NSA_KV_HEADS, NSA_HD, wlen).transpose(0, 4, 1, 2, 3),
                _unblocked(conva, bsz), _unblocked(gconv, bsz), s_new, _unblocked(ffnc, bsz))


def stack_layers(states, i):
    return jnp.stack([s[i] for s in states], axis=0)


def kernel(x_prompt, x_sample, cache_nsa_kv, cache_nsa_win, state_conv_a, state_gdn_conv, state_gdn, state_ffn_conv, page_table, ln_emb_g, ln_emb_b, w_in, conv_a_w, gdn_conv_w, gdn_a_log, gdn_dt_bias, gdn_norm_g, cmp_pe, cmp_w1, cmp_b1, cmp_w2, w_out, ln1_g, ln1_b, w_up, ffn_conv_w, w_down, ln2_g, ln2_b):
    n_seq, t_len = x_prompt.shape[:2]
    dec_b, dec_t = x_sample.shape[:2]
    depth = w_in.shape[0]
    xp = layer_norm_rows(x_prompt.reshape(-1, D_MODEL), ln_emb_g, ln_emb_b)
    xs = layer_norm_rows(_to_tb(x_sample.reshape(-1, D_MODEL), dec_b, dec_t), ln_emb_g, ln_emb_b)
    w_in_b = permute_w_in(w_in)
    w_out_b, w_up_b, w_down_b = (w.astype(BF16) for w in (w_out, w_up, w_down))
    cache_kv = cache_nsa_kv.transpose(0, 1, 3, 4, 5, 2).reshape(cache_nsa_kv.shape[:2] + (4 * NSA_KV_DIM, -1))
    win_cache = cache_nsa_win.transpose(0, 1, 3, 4, 5, 2).reshape(cache_nsa_win.shape[:2] + (2 * NSA_KV_DIM, -1))
    st_p, st_s = [], []
    for l in range(depth):
        lw = dict(w_in=w_in_b[l], conv_a_w=conv_a_w[l], gdn_conv_w=gdn_conv_w[l], gdn_a_log=gdn_a_log[l],
                  gdn_dt_bias=gdn_dt_bias[l], gdn_norm_g=gdn_norm_g[l],
                  cmp=compress_params(cmp_pe[l], cmp_w1[l], cmp_b1[l], cmp_w2[l]),
                  w_out=w_out_b[l], ln1_g=ln1_g[l], ln1_b=ln1_b[l], w_up=w_up_b[l], ffn_conv_w=ffn_conv_w[l],
                  w_down=w_down_b[l], ln2_g=ln2_g[l], ln2_b=ln2_b[l])
        xp, sp = layer_prompt(xp, n_seq, t_len, lw)
        xs, ss = layer_sample(xs, dec_b, dec_t, lw, l, cache_kv, page_table, win_cache[l], state_conv_a[l],
                              state_gdn_conv[l], state_gdn[l], state_ffn_conv[l])
        st_p.append(sp)
        st_s.append(ss)
    xp = xp.reshape(n_seq, t_len, D_MODEL)
    xs = _to_bt(xs, dec_b, dec_t).reshape(dec_b, dec_t, D_MODEL)
    return (xp, xs,
            stack_layers(st_p, 0), stack_layers(st_s, 0),
            stack_layers(st_p, 1), stack_layers(st_s, 1),
            stack_layers(st_p, 2), stack_layers(st_s, 2),
            stack_layers(st_p, 3), stack_layers(st_s, 3),
            stack_layers(st_p, 4), stack_layers(st_s, 4),
            stack_layers(st_p, 5), stack_layers(st_s, 5))
```

```python
import functools
import math

import jax
import jax.numpy as jnp
import numpy as np
from jax import lax
from jax.experimental import pallas as pl
from jax.experimental.pallas import tpu as pltpu

F32 = jnp.float32
BF16 = jnp.bfloat16

D_MODEL = 1024
DEPTH = 4
HEAD_DIM = 64
A_WIDTH = 256
A_CONV = 3
GDN_WIDTH = 256
GDN_HEADS = 4
GDN_DK = 64
GDN_DV = 64
GDN_QK = 256
GDN_CONV = 4
GDN_CONV_CH = 768
GDN_CHUNK = 64
NSA_WIDTH = 512
NSA_HEADS = 8
NSA_KV_HEADS = 2
NSA_GROUP = 4
NSA_HD = 64
NSA_KV_DIM = 128
CMP_STRIDE = 16
CMP_BLOCK = 32
SEL_BLOCK = 64
SEL_TOPN = 8
WINDOW = 512
Q_BLOCK = 128
D_FF = 2816
FFN_CONV = 3
ALPHA = (2.0 * DEPTH) ** 0.25
LN_EPS = 1e-5
IN_SIZES = (A_WIDTH, A_WIDTH, A_WIDTH, GDN_QK, GDN_QK, GDN_WIDTH, GDN_WIDTH, GDN_HEADS, GDN_HEADS,
            NSA_WIDTH, 4 * NSA_KV_DIM, 2 * NSA_KV_DIM, 3 * NSA_HEADS)
N_IN = sum(IN_SIZES)

VMEM_LIMIT_BYTES = 56 * 1024 * 1024


def _ln_rows(z, g, b):
    mu = jnp.mean(z, axis=-1, keepdims=True)
    zc = z - mu
    var = jnp.mean(zc * zc, axis=-1, keepdims=True)
    return zc * lax.rsqrt(var + LN_EPS) * g + b


def _ln_kernel(x_ref, g_ref, b_ref, o_ref):
    o_ref[...] = _ln_rows(x_ref[...], g_ref[...], b_ref[...])


def layer_norm_rows(x, g, b, tm=512):
    m, d = x.shape
    return pl.pallas_call(
        _ln_kernel,
        grid=(m // tm,),
        in_specs=[pl.BlockSpec((tm, d), lambda i: (i, 0)),
                  pl.BlockSpec((1, d), lambda i: (0, 0)),
                  pl.BlockSpec((1, d), lambda i: (0, 0))],
        out_specs=pl.BlockSpec((tm, d), lambda i: (i, 0)),
        out_shape=jax.ShapeDtypeStruct((m, d), F32),
        name="ln_rows",
    )(x, g.reshape(1, d), b.reshape(1, d))


def _dense_kernel(x_ref, w_ref, o_ref):
    o_ref[...] = jnp.dot(x_ref[...].astype(BF16), w_ref[...], preferred_element_type=F32)


def dense(x, w, tm, tn, name):
    m, k = x.shape
    n = w.shape[1]
    return pl.pallas_call(
        _dense_kernel,
        grid=(n // tn, m // tm),
        in_specs=[pl.BlockSpec((tm, k), lambda j, i: (i, 0)),
                  pl.BlockSpec((k, tn), lambda j, i: (0, j))],
        out_specs=pl.BlockSpec((tm, tn), lambda j, i: (i, j)),
        out_shape=jax.ShapeDtypeStruct((m, n), F32),
        compiler_params=pltpu.CompilerParams(vmem_limit_bytes=VMEM_LIMIT_BYTES),
        name=name,
    )(x, w)


def _dense_res_ln_kernel(y_ref, w_ref, x_ref, g_ref, b_ref, o_ref):
    acc = jnp.dot(y_ref[...].astype(BF16), w_ref[...], preferred_element_type=F32)
    o_ref[...] = _ln_rows(ALPHA * x_ref[...] + acc, g_ref[...], b_ref[...])


def dense_res_ln(y, w, x, g, b, tm, name):
    m, k = y.shape
    d = w.shape[1]
    return pl.pallas_call(
        _dense_res_ln_kernel,
        grid=(m // tm,),
        in_specs=[pl.BlockSpec((tm, k), lambda i: (i, 0)),
                  pl.BlockSpec((k, d), lambda i: (0, 0)),
                  pl.BlockSpec((tm, d), lambda i: (i, 0)),
                  pl.BlockSpec((1, d), lambda i: (0, 0)),
                  pl.BlockSpec((1, d), lambda i: (0, 0))],
        out_specs=pl.BlockSpec((tm, d), lambda i: (i, 0)),
        out_shape=jax.ShapeDtypeStruct((m, d), F32),
        compiler_params=pltpu.CompilerParams(vmem_limit_bytes=VMEM_LIMIT_BYTES),
        name=name,
    )(y, w, x, g.reshape(1, d), b.reshape(1, d))


SUBLANES = 8


def _conv_rows_carry(x, tail, w_ref, ksize):
    row8 = lax.broadcasted_iota(jnp.int32, (SUBLANES, x.shape[1]), 0)
    y = x * w_ref[ksize - 1:ksize, :]
    for k in range(1, ksize):
        rolled = pltpu.roll(x, k, axis=0)
        first = jnp.where(row8 < k, pltpu.roll(tail, k, axis=0), rolled[0:SUBLANES])
        y = y + jnp.concatenate([first, rolled[SUBLANES:]], axis=0) * w_ref[ksize - 1 - k:ksize - k, :]
    return y


def _conv_rows_blocked(x, buf, w_ref, ksize, step):
    r = x.shape[0]
    y = x * w_ref[ksize - 1:ksize, :]
    for k in range(1, ksize):
        prev = jnp.concatenate([buf[(ksize - 1 - k) * step:(ksize - 1) * step], x[0:r - k * step]], axis=0)
        y = y + prev * w_ref[ksize - 1 - k:ksize - k, :]
    return y


def _conv_start(buf_ref, tail_s, mode):
    kind, param = mode
    if kind == "carry":
        @pl.when(pl.program_id(0) % param == 0)
        def _sequence_start():
            tail_s[...] = buf_ref[0]


def _conv_tile(x, buf_ref, st_ref, tail_s, w_ref, ksize, mode, cols=slice(None)):
    kind, param = mode
    w = w_ref[:, cols]
    if kind == "carry":
        tm = x.shape[0]
        y = _conv_rows_carry(x, tail_s[:, cols], w, ksize)
        tail_s[:, cols] = x[tm - SUBLANES:tm]
        st_ref[0, :, cols] = x[tm - SUBLANES:tm]
        return y
    y = _conv_rows_blocked(x, buf_ref[0, :, cols], w, ksize, param)
    st_ref[0, :, cols] = x[x.shape[0] - (ksize - 1) * param:]
    return y


def _conv_specs(mode, ksize, c, tm):
    kind, param = mode
    if kind == "carry":
        rows = SUBLANES
        idx = lambda i: (i // param, 0, 0)
    else:
        rows = (ksize - 1) * param
        idx = lambda i: (0, 0, 0)
    return pl.BlockSpec((1, rows, c), idx), rows


def _mix_out_ln_kernel(ab_ref, ac_ref, ah_ref, yb_ref, yc_ref, x_ref, w_ref, cw_ref, g_ref, b_ref, buf_ref,
                       o_ref, st_ref, tail_s, *, mode):
    u = ac_ref[...] * ah_ref[...]
    _conv_start(buf_ref, tail_s, mode)
    z = _conv_tile(u, buf_ref, st_ref, tail_s, cw_ref, A_CONV, mode)
    y = jnp.concatenate([ab_ref[...] * z, yb_ref[...], yc_ref[...]], axis=1).astype(BF16)
    acc = jnp.dot(y, w_ref[...], preferred_element_type=F32)
    o_ref[...] = _ln_rows(ALPHA * x_ref[...] + acc, g_ref[...], b_ref[...])


def mix_out_ln(proj, y_b, y_c, x, w_out, conv_w, g, b, buf, tm, mode):
    m = x.shape[0]
    buf_spec, st_rows = _conv_specs(mode, A_CONV, A_WIDTH, tm)
    n_st = buf.shape[0]
    row = lambda c, w: pl.BlockSpec((tm, w), lambda i: (i, c // w))
    const = lambda shape: pl.BlockSpec(shape, lambda i: (0,) * len(shape))
    return pl.pallas_call(
        functools.partial(_mix_out_ln_kernel, mode=mode),
        grid=(m // tm,),
        in_specs=[row(COL_AB, A_WIDTH), row(COL_AC, A_WIDTH), row(COL_AH, A_WIDTH),
                  row(0, GDN_WIDTH), row(0, NSA_WIDTH), row(0, D_MODEL),
                  const((D_MODEL, D_MODEL)), const((A_CONV, A_WIDTH)), const((1, D_MODEL)), const((1, D_MODEL)),
                  buf_spec],
        out_specs=[row(0, D_MODEL), pl.BlockSpec((1, st_rows, A_WIDTH), buf_spec.index_map)],
        out_shape=[jax.ShapeDtypeStruct((m, D_MODEL), F32), jax.ShapeDtypeStruct((n_st, st_rows, A_WIDTH), F32)],
        scratch_shapes=[pltpu.VMEM((SUBLANES, A_WIDTH), F32)],
        compiler_params=pltpu.CompilerParams(dimension_semantics=("arbitrary",), vmem_limit_bytes=VMEM_LIMIT_BYTES),
        name="mix_out_ln",
    )(proj, proj, proj, y_b, y_c, x, w_out, conv_w, g.reshape(1, -1), b.reshape(1, -1), buf)


FFN_COL_CHUNK = 256


def _ffn_up_kernel(x_ref, wg_ref, wv_ref, cw_ref, buf_ref, h_ref, st_ref, tail_s, *, mode):
    xb = x_ref[...].astype(BF16)
    _conv_start(buf_ref, tail_s, mode)
    tn = h_ref.shape[1]
    cb = FFN_COL_CHUNK if tn % FFN_COL_CHUNK == 0 else tn
    for j in range(tn // cb):
        cols = slice(j * cb, (j + 1) * cb)
        gate = jnp.dot(xb, wg_ref[:, cols], preferred_element_type=F32)
        val = jnp.dot(xb, wv_ref[:, cols], preferred_element_type=F32)
        gate = _conv_tile(gate, buf_ref, st_ref, tail_s, cw_ref, FFN_CONV, mode, cols)
        h_ref[:, cols] = (jax.nn.silu(gate) * val).astype(BF16)


def ffn_up_act(x, w_up, conv_w, buf, tm, tn, mode):
    m = x.shape[0]
    kind, param = mode
    nj = D_FF // tn
    if kind == "carry":
        assert nj == 1
        st_rows, st_idx = SUBLANES, (lambda i, j: (i // param, 0, j))
    else:
        assert tm == m
        st_rows, st_idx = (FFN_CONV - 1) * param, (lambda i, j: (0, 0, j))
    n_st = buf.shape[0]
    once = pl.Buffered(1) if nj == 1 else None
    return pl.pallas_call(
        functools.partial(_ffn_up_kernel, mode=mode),
        grid=(m // tm, nj),
        in_specs=[pl.BlockSpec((tm, D_MODEL), lambda i, j: (i, 0)),
                  pl.BlockSpec((D_MODEL, tn), lambda i, j: (0, j), pipeline_mode=once),
                  pl.BlockSpec((D_MODEL, tn), lambda i, j: (0, nj + j), pipeline_mode=once),
                  pl.BlockSpec((FFN_CONV, tn), lambda i, j: (0, j)),
                  pl.BlockSpec((1, st_rows, tn), st_idx)],
        out_specs=[pl.BlockSpec((tm, tn), lambda i, j: (i, j)),
                   pl.BlockSpec((1, st_rows, tn), st_idx)],
        out_shape=[jax.ShapeDtypeStruct((m, D_FF), BF16), jax.ShapeDtypeStruct((n_st, st_rows, D_FF), F32)],
        scratch_shapes=[pltpu.VMEM((SUBLANES, tn), F32)],
        compiler_params=pltpu.CompilerParams(dimension_semantics=("arbitrary", "arbitrary"),
                                             vmem_limit_bytes=VMEM_LIMIT_BYTES),
        name="ffn_up_act",
    )(x, w_up, w_up, conv_w, buf)


def _conv_silu_kernel(x_ref, cw_ref, buf_ref, o_ref, st_ref, tail_s, *, mode):
    _conv_start(buf_ref, tail_s, mode)
    o_ref[...] = jax.nn.silu(_conv_tile(x_ref[...], buf_ref, st_ref, tail_s, cw_ref, GDN_CONV, mode))


def gdn_conv_silu(proj, conv_w, buf, tm, mode):
    m = proj.shape[0]
    buf_spec, st_rows = _conv_specs(mode, GDN_CONV, GDN_CONV_CH, tm)
    n_st = buf.shape[0]
    return pl.pallas_call(
        functools.partial(_conv_silu_kernel, mode=mode),
        grid=(m // tm,),
        in_specs=[pl.BlockSpec((tm, GDN_CONV_CH), lambda i: (i, COL_GQKV // GDN_CONV_CH)),
                  pl.BlockSpec((GDN_CONV, GDN_CONV_CH), lambda i: (0, 0)),
                  buf_spec],
        out_specs=[pl.BlockSpec((tm, GDN_CONV_CH), lambda i: (i, 0)),
                   pl.BlockSpec((1, st_rows, GDN_CONV_CH), buf_spec.index_map)],
        out_shape=[jax.ShapeDtypeStruct((m, GDN_CONV_CH), F32),
                   jax.ShapeDtypeStruct((n_st, st_rows, GDN_CONV_CH), F32)],
        scratch_shapes=[pltpu.VMEM((SUBLANES, GDN_CONV_CH), F32)],
        compiler_params=pltpu.CompilerParams(dimension_semantics=("arbitrary",), vmem_limit_bytes=VMEM_LIMIT_BYTES),
        name="gdn_conv_silu",
    )(proj, conv_w, buf)


GDN_ROWS = 128
GDN_STEP = 256


def _mm(a, b):
    return jnp.dot(a, b, preferred_element_type=F32)


def _split3(x):
    hi = x.astype(BF16)
    r1 = x - hi.astype(F32)
    mid = r1.astype(BF16)
    return hi, mid, (r1 - mid.astype(F32)).astype(BF16)


def _mm_exact_lhs(c, x):
    hi, mid, lo = _split3(x)
    return _mm(c, hi) + _mm(c, mid) + _mm(c, lo)


def _mm_exact_rhs(x, c):
    hi, mid, lo = _split3(x)
    return _mm(hi, c) + _mm(mid, c) + _mm(lo, c)


def _mm3(a, b):
    ah = a.astype(BF16)
    al = (a - ah.astype(F32)).astype(BF16)
    bh = b.astype(BF16)
    bl = (b - bh.astype(F32)).astype(BF16)
    return _mm(ah, bh) + _mm(ah, bl) + _mm(al, bh)


def _gdn_kernel(qkv_ref, gate_ref, sm_ref, alog_ref, dt_ref, gain_ref, ea_ref, eb_ref, tril_ref, cones_ref,
                hones_ref, s0_ref, y_ref, sout_ref, s_s, o_s, *, chunk, carry_state, tiles_per_seq):
    rws = GDN_ROWS
    ngrp = qkv_ref.shape[0] // rws
    nchunk = rws // chunk
    shift = chunk.bit_length() - 1
    nt = (((1,), (1,)), ((), ()))
    tn = (((0,), (0,)), ((), ()))

    if carry_state:
        @pl.when(pl.program_id(0) % tiles_per_seq == 0)
        def _sequence_start():
            s_s[...] = s0_ref[0]

    hones = hones_ref[...]

    def head_sum(x):
        xh = x.astype(BF16)
        return _mm(xh, hones) + _mm((x - xh.astype(F32)).astype(BF16), hones)

    qkv = qkv_ref[...]
    q, k, v = qkv[:, 0:GDN_QK], qkv[:, GDN_QK:2 * GDN_QK], qkv[:, 2 * GDN_QK:]
    q = q * lax.rsqrt(head_sum(q * q) + 1e-6) * (GDN_DK ** -0.5)
    k = k * lax.rsqrt(head_sum(k * k) + 1e-6)
    sm = sm_ref[...]
    g = -jnp.exp(alog_ref[...]) * jax.nn.softplus(_mm_exact_rhs(sm, ea_ref[...]) + dt_ref[...])
    beta = jax.nn.sigmoid(_mm_exact_rhs(sm, eb_ref[...]))
    gc = _mm_exact_lhs(tril_ref[...], g)
    gcl = _mm_exact_lhs(cones_ref[...], g)

    row = lax.broadcasted_iota(jnp.int32, (rws, rws), 0)
    col = lax.broadcasted_iota(jnp.int32, (rws, rws), 1)
    same = lax.shift_right_logical(row, shift) == lax.shift_right_logical(col, shift)
    eye = jnp.where(row == col, 1.0, 0.0)

    heads = range(GDN_HEADS)
    chains = [(slice(grp * rws, (grp + 1) * rws), h) for grp in range(ngrp) for h in heads]
    cid = lambda grp, h: grp * GDN_HEADS + h
    hsl = [slice(GDN_DK * h, GDN_DK * (h + 1)) for h in heads]
    incl = same & (row >= col)
    strict = same & (row > col)
    xs = [gc[rg, 128 * h:128 * (h + 1)] for rg, h in chains]
    gcols = [x[:, 0:GDN_DK] for x in xs]
    gends = [gcl[rg, 128 * h:128 * h + GDN_DK] for rg, h in chains]
    betas = [beta[rg, 128 * h:128 * h + GDN_DK] for rg, h in chains]
    decays = [jnp.where(incl, jnp.exp(jnp.where(incl, x - x.T, 0.0)), 0.0) for x in xs]
    kbs = [k[rg, hsl[h]] * betas[c] for c, (rg, h) in enumerate(chains)]
    khbs = [k[rg, hsl[h]].astype(BF16) for rg, h in chains]
    heads_all = range(len(chains))
    a_s = [jnp.where(strict, lax.dot_general(kbs[c].astype(BF16), khbs[c], nt, preferred_element_type=F32) * decays[c], 0.0)
           for c in heads_all]
    compact = lambda full: jnp.sum(full.reshape(nchunk, chunk, rws), axis=0)
    expand = lambda c: jnp.where(same, jnp.concatenate([c] * nchunk, axis=0), 0.0)
    eye_c = compact(eye)
    a_cs = [compact(a) for a in a_s]
    minv_cs = [eye_c - a_c for a_c in a_cs]
    apow_cs = [_mm3(a_cs[c], a_s[c]) for c in heads_all]
    for step in range(shift - 1):
        apow_fs = [expand(ap) for ap in apow_cs]
        minv_cs = [_mm3(minv_cs[c], eye + apow_fs[c]) for c in heads_all]
        if step < shift - 2:
            apow_cs = [_mm3(apow_cs[c], apow_fs[c]) for c in heads_all]
    minvs = [expand(m_c) for m_c in minv_cs]
    egcs = [jnp.exp(gc_h) for gc_h in gcols]
    us = [_mm3(minvs[c], v[rg, hsl[h]] * betas[c]) for c, (rg, h) in enumerate(chains)]
    wbs = [_mm3(minvs[c], kbs[c] * egcs[c]).astype(BF16) for c in heads_all]
    a_qks = [(lax.dot_general(q[rg, hsl[h]].astype(BF16), khbs[c], nt, preferred_element_type=F32) * decays[c]).astype(BF16)
             for c, (rg, h) in enumerate(chains)]
    q_decs = [(q[rg, hsl[h]] * egcs[c]).astype(BF16) for c, (rg, h) in enumerate(chains)]
    k_decs = [(k[rg, hsl[h]] * jnp.exp(gends[c] - gcols[c])).astype(BF16) for c, (rg, h) in enumerate(chains)]
    g_lasts = [jnp.exp(g_h) for g_h in gends]
    states = [s_s[h] for h in heads] if carry_state else None
    v_new = [[] for _ in heads_all]
    o_state = [[] for _ in heads_all]
    for n in range(ngrp * nchunk):
        grp, nl = divmod(n, nchunk)
        r = slice(nl * chunk, (nl + 1) * chunk)
        olds = states if carry_state else [s0_ref[n, h] for h in heads]
        sbs = [s_old.astype(BF16) for s_old in olds]
        vns = [us[cid(grp, h)][r] - _mm(wbs[cid(grp, h)][r], sbs[h]) for h in heads]
        for h in heads:
            o_state[cid(grp, h)].append(_mm(q_decs[cid(grp, h)][r], sbs[h]))
            v_new[cid(grp, h)].append(vns[h])
        news = [olds[h] * g_lasts[cid(grp, h)][nl * chunk:nl * chunk + 1, :]
                + lax.dot_general(k_decs[cid(grp, h)][r], vns[h].astype(BF16), tn, preferred_element_type=F32)
                for h in heads]
        if carry_state:
            states = news
        else:
            for h in heads:
                sout_ref[n, h] = news[h]
    for c, (rg, h) in enumerate(chains):
        v_all = jnp.concatenate(v_new[c], axis=0) if nchunk > 1 else v_new[c][0]
        o_all = jnp.concatenate(o_state[c], axis=0) if nchunk > 1 else o_state[c][0]
        o_s[rg, hsl[h]] = o_all + _mm(a_qks[c], v_all.astype(BF16))
    if carry_state:
        for h in heads:
            s_s[h] = states[h]

    if carry_state:
        sout_ref[0] = s_s[...]
    o = o_s[...]
    o = o * lax.rsqrt(head_sum(o * o) * (1.0 / GDN_DV) + 1e-6)
    y_ref[...] = o * gain_ref[...] * jax.nn.silu(gate_ref[...])


def gated_delta(qkv_act, proj, s0, a_log, dt_bias, norm_g, chunk, carry_state, tiles_per_seq):
    m = qkv_act.shape[0]
    rws = GDN_STEP
    idx = np.arange(rws)
    same = (idx[:, None] // chunk) == (idx[None, :] // chunk)
    tril = jnp.asarray(same & (idx[:, None] >= idx[None, :]), BF16)
    cones = jnp.asarray(same, BF16)
    lane = np.arange(GDN_WIDTH)
    hones = jnp.asarray(lane[:, None] // GDN_DK == lane[None, :] // GDN_DK, BF16)
    lane2 = np.arange(4 * 128) // 128
    smr = np.arange(128)
    ea = jnp.asarray(smr[:, None] == SM_GA + lane2[None, :], BF16)
    eb = jnp.asarray(smr[:, None] == SM_GB + lane2[None, :], BF16)
    alog_x = jnp.repeat(a_log.astype(F32), 128).reshape(1, 512)
    dt_x = jnp.repeat(dt_bias.astype(F32), 128).reshape(1, 512)
    gain_x = jnp.tile(norm_g.astype(F32), GDN_HEADS).reshape(1, GDN_WIDTH)
    if carry_state:
        s_blk, s_idx = (1, GDN_HEADS, GDN_DK, GDN_DV), (lambda i: (i // tiles_per_seq, 0, 0, 0))
    else:
        s_blk, s_idx = (rws // chunk, GDN_HEADS, GDN_DK, GDN_DV), (lambda i: (i, 0, 0, 0))
    const = lambda shape: pl.BlockSpec(shape, lambda i: (0,) * len(shape))
    return pl.pallas_call(
        functools.partial(_gdn_kernel, chunk=chunk, carry_state=carry_state, tiles_per_seq=tiles_per_seq),
        grid=(m // rws,),
        in_specs=[pl.BlockSpec((rws, GDN_CONV_CH), lambda i: (i, 0)),
                  pl.BlockSpec((rws, GDN_WIDTH), lambda i: (i, COL_GGATE // GDN_WIDTH)),
                  pl.BlockSpec((rws, 128), lambda i: (i, COL_SMALL // 128)),
                  const((1, 512)), const((1, 512)), const((1, GDN_WIDTH)),
                  const((128, 512)), const((128, 512)), const((rws, rws)), const((rws, rws)),
                  const((GDN_WIDTH, GDN_WIDTH)),
                  pl.BlockSpec(s_blk, s_idx)],
        out_specs=[pl.BlockSpec((rws, GDN_WIDTH), lambda i: (i, 0)), pl.BlockSpec(s_blk, s_idx)],
        out_shape=[jax.ShapeDtypeStruct((m, GDN_WIDTH), F32), jax.ShapeDtypeStruct(s0.shape, F32)],
        scratch_shapes=[pltpu.VMEM((GDN_HEADS, GDN_DK, GDN_DV), F32), pltpu.VMEM((rws, GDN_WIDTH), F32)],
        compiler_params=pltpu.CompilerParams(dimension_semantics=("arbitrary",), vmem_limit_bytes=VMEM_LIMIT_BYTES),
        name="gated_delta",
    )(qkv_act, proj, proj, alog_x, dt_x, gain_x, ea, eb, tril, cones, hones, s0)


COL_Q = 0
COL_KV = 512
COL_AB = 1024
COL_AC = 1280
COL_GQKV = 1536
COL_AH = 2304
COL_GGATE = 2560
COL_WIN = 2816
COL_SMALL = 3072
N_PROJ = 3200
SM_GA, SM_GB, SM_NG = 0, 4, 8


W_IN_ORDER = (9, 10, 0, 1, 3, 4, 5, 2, 6, 11)
W_IN_SMALL = (7, 8, 12)


def _permute_w_in_kernel(w_ref, o_ref):
    offs = np.concatenate([[0], np.cumsum(IN_SIZES)])
    grp = lambda k: w_ref[:, int(offs[k]):int(offs[k + 1])]
    col = 0
    for k in W_IN_ORDER:
        o_ref[:, col:col + IN_SIZES[k]] = grp(k).astype(BF16)
        col += IN_SIZES[k]
    rows = w_ref.shape[0]
    small = [grp(k) for k in W_IN_SMALL]
    small.append(jnp.zeros((rows, N_PROJ - col - sum(IN_SIZES[k] for k in W_IN_SMALL)), F32))
    o_ref[:, col:N_PROJ] = jnp.concatenate(small, axis=1).astype(BF16)


def permute_w_in(w_in, tk=256):
    depth, d, n_in = w_in.shape
    return pl.pallas_call(
        _permute_w_in_kernel,
        grid=(depth, d // tk),
        in_specs=[pl.BlockSpec((None, tk, n_in), lambda l, i: (l, i, 0))],
        out_specs=pl.BlockSpec((None, tk, N_PROJ), lambda l, i: (l, i, 0)),
        out_shape=jax.ShapeDtypeStruct((depth, d, N_PROJ), BF16),
        name="permute_w_in",
    )(w_in)


NEG = -1e30
KEY_CHUNK = 128
SEL_CHUNK = 512


def _block_diag4(a, b):
    z = jnp.zeros_like(a)
    rows = [jnp.concatenate([a, z, z, z], -1), jnp.concatenate([z, a, z, z], -1),
            jnp.concatenate([z, z, b, z], -1), jnp.concatenate([z, z, z, b], -1)]
    return jnp.concatenate(rows, -2)


def compress_params(cmp_pe, cmp_w1, cmp_b1, cmp_w2):
    w1 = _block_diag4(cmp_w1[0], cmp_w1[1]).astype(BF16)
    w2 = _block_diag4(cmp_w2[0], cmp_w2[1]).astype(BF16)
    pe = jnp.concatenate([cmp_pe[0], cmp_pe[0], cmp_pe[1], cmp_pe[1]], -1)
    b1 = jnp.concatenate([cmp_b1[0], cmp_b1[0], cmp_b1[1], cmp_b1[1]], -1).reshape(1, 256)
    return w1, pe, b1, w2


def _compress_rows(row_loader, ncp, w1_ref, pe_ref, b1_ref, w2_ref):
    acc_lo = jnp.zeros((ncp, 256), F32)
    acc_hi = jnp.zeros((ncp, 256), F32)
    for l in range(CMP_STRIDE):
        x = row_loader(l)
        acc_lo += jnp.dot((x + pe_ref[l:l + 1, :]).astype(BF16), w1_ref[l], preferred_element_type=F32)
        acc_hi += jnp.dot((x + pe_ref[l + CMP_STRIDE:l + CMP_STRIDE + 1, :]).astype(BF16),
                          w1_ref[l + CMP_STRIDE], preferred_element_type=F32)
    hid = jax.nn.gelu(acc_lo + pltpu.roll(acc_hi, ncp - 1, axis=0) + b1_ref[...])
    return jnp.dot(hid.astype(BF16), w2_ref[...], preferred_element_type=F32)


def _cmp_attend(q_rows, kc, vc, pos, ovl, n_heads, tq, ncp):
    s = lax.dot_general(q_rows, kc, (((1,), (1,)), ((), ())), preferred_element_type=F32)
    cend = lax.broadcasted_iota(jnp.int32, (tq, ncp), 1) * CMP_STRIDE + (CMP_BLOCK - 1)
    s3 = jnp.where((cend <= pos)[None], s.reshape(n_heads, tq, ncp), -jnp.inf)
    m = jnp.max(s3, axis=-1, keepdims=True)
    m = jnp.where(m == -jnp.inf, 0.0, m)
    p = jnp.exp(s3 - m)
    pn = p / jnp.maximum(jnp.sum(p, axis=-1, keepdims=True), 1e-30)
    o_cmp = jnp.dot(pn.reshape(n_heads * tq, ncp).astype(BF16), vc, preferred_element_type=F32)
    psum = pn[0]
    for g in range(1, n_heads):
        psum = psum + pn[g]
    p_hi = psum.astype(BF16)
    p_lo = (psum - p_hi.astype(F32)).astype(BF16)
    imp = (jnp.dot(p_hi, ovl, preferred_element_type=F32) + jnp.dot(p_lo, ovl, preferred_element_type=F32))
    return o_cmp, imp


def _select_topn(imp, pos, fillers=()):
    fillers = list(fillers)
    r, ns = imp.shape
    blk = lax.broadcasted_iota(jnp.int32, (r, ns), 1)
    blk_f = blk.astype(F32)
    forced = (blk == 0) | (blk == lax.shift_right_logical(pos, 6))
    valid = blk * SEL_BLOCK <= pos
    v = jnp.where(forced, jnp.inf, jnp.where(valid, imp, -jnp.inf))
    sel = jnp.zeros((r, ns), F32)
    for _ in range(min(SEL_TOPN, ns)):
        mx = jnp.max(v, axis=-1, keepdims=True)
        idx = jnp.min(jnp.where(v == mx, blk_f, float(ns)), axis=-1, keepdims=True)
        hit = blk_f == idx
        sel = jnp.where(hit, 1.0, sel)
        v = jnp.where(hit, -jnp.inf, v)
        if fillers:
            fillers.pop(0)()
    for f in fillers:
        f()
    return sel


def _nsa_prompt_kernel(q_ref, cmp_ref, slc_ref, win_ref, sm_ref, w1_ref, pe_ref, b1_ref, w2_ref, ovl_ref, exp_ref,
                       gexp_ref, y_ref, kc_s, vc_s, ks_s, vs_s, kw_s, vw_s, q_s, sel_s, ocmp_s, owin_s, m_s, acc_s, *, t_len):
    tq = KEY_CHUNK
    i = pl.program_id(1)
    ncp = t_len // CMP_STRIDE
    ns = t_len // SEL_BLOCK
    ng = NSA_GROUP
    rows = ng * tq

    @pl.when(i == 0)
    def _prepare_sequence():
        kvc = _compress_rows(lambda l: cmp_ref[0, l], ncp, w1_ref, pe_ref, b1_ref, w2_ref)
        kc_s[...] = kvc[:, 0:128].astype(BF16)
        vc_s[...] = kvc[:, 128:256].astype(BF16)
        ones = jnp.ones((512, 64), BF16)

        def cast_rows(r, carry):
            sl = pl.ds(pl.multiple_of(r * 512, 512), 512)
            for h in range(NSA_KV_HEADS):
                ks_s[h, sl, :] = slc_ref[sl, 64 * h:64 * h + 64].astype(BF16)
                vs_s[h, sl, 0:64] = slc_ref[sl, 128 + 64 * h:192 + 64 * h].astype(BF16)
                vs_s[h, sl, 64:128] = ones
                kw_s[h, sl, :] = win_ref[sl, 64 * h:64 * h + 64].astype(BF16)
                vw_s[h, sl, 0:64] = win_ref[sl, 128 + 64 * h:192 + 64 * h].astype(BF16)
                vw_s[h, sl, 64:128] = ones
            return carry

        lax.fori_loop(0, t_len // 512, cast_rows, 0)

    pos = i * tq + lax.broadcasted_iota(jnp.int32, (tq, 1), 0)
    gates = jax.nn.sigmoid(sm_ref[...])
    g_hi = gates.astype(BF16)
    gx = _mm(g_hi, gexp_ref[...]) + _mm((gates - g_hi.astype(F32)).astype(BF16), gexp_ref[...])
    nt = (((1,), (1,)), ((), ()))

    wk = min(WINDOW + KEY_CHUNK, t_len)
    w0 = pl.multiple_of(jnp.minimum(jnp.maximum(i - WINDOW // KEY_CHUNK, 0), (t_len - wk) // KEY_CHUNK) * KEY_CHUNK,
                        KEY_CHUNK)
    wpos = w0 + lax.broadcasted_iota(jnp.int32, (tq, wk), 1)
    wbias = jnp.where(wpos <= pos, jnp.where(wpos > pos - WINDOW, 0.0, NEG), NEG)

    hs = range(NSA_KV_HEADS)
    q_rows = []
    for h in hs:
        qh = q_ref[:, 256 * h:256 * h + 256] * (NSA_HD ** -0.5)
        q_rows.append(jnp.concatenate([qh[:, 64 * g:64 * g + 64] for g in range(ng)], axis=0).astype(BF16))
        q_s[h] = q_rows[h]
    cmp = [_cmp_attend(q_rows[h], kc_s[:, 64 * h:64 * h + 64], vc_s[:, 64 * h:64 * h + 64], pos, ovl_ref[...],
                       ng, tq, ncp) for h in hs]
    sw = [lax.dot_general(q_rows[h], kw_s[h, pl.ds(w0, wk), :], nt, preferred_element_type=F32) for h in hs]
    sw = [(s.reshape(ng, tq, wk) + wbias[None]).reshape(rows, wk) for s in sw]
    pw = [jnp.exp(s - jnp.max(s, axis=-1, keepdims=True)).astype(BF16) for s in sw]
    aw = [jnp.dot(pw[h], vw_s[h, pl.ds(w0, wk), :], preferred_element_type=F32) for h in hs]
    sel = _select_topn(jnp.concatenate([c[1] for c in cmp], axis=0), jnp.concatenate([pos] * NSA_KV_HEADS, axis=0))
    for h in hs:
        ocmp_s[h] = cmp[h][0]
        sel_s[h] = sel[h * tq:(h + 1) * tq].astype(BF16)
        owin_s[h] = aw[h][:, 0:64] / aw[h][:, 64:128]
        m_s[h] = jnp.full((rows, 128), NEG, F32)
        acc_s[h] = jnp.zeros((rows, 128), F32)

    lane = lax.broadcasted_iota(jnp.int32, (tq, SEL_CHUNK), 1)

    def sel_body(j, carry):
        off = pl.multiple_of(j * SEL_CHUNK, SEL_CHUNK)
        hs = range(NSA_KV_HEADS)
        pairs = [(h, slice(g * tq, (g + 1) * tq)) for h in hs for g in range(ng)]
        ks = [ks_s[h, pl.ds(off, SEL_CHUNK), :] for h in hs]
        vs = [vs_s[h, pl.ds(off, SEL_CHUNK), :] for h in hs]
        causal = off + lane <= pos
        biases = [jnp.where(causal, (jnp.dot(sel_s[h], exp_ref[:, pl.ds(off, SEL_CHUNK)],
                                            preferred_element_type=F32) - 1.0) * (-NEG), NEG) for h in hs]
        ss = [lax.dot_general(q_s[h, r, :], ks[h], nt, preferred_element_type=F32) + biases[h] for h, r in pairs]
        m_prevs = [m_s[h, r, :] for h, r in pairs]
        m_news = [jnp.maximum(mp, jnp.max(s, axis=-1, keepdims=True)) for mp, s in zip(m_prevs, ss)]
        ps = [jnp.exp(s - jnp.concatenate([mn] * (SEL_CHUNK // 128), axis=1)).astype(BF16)
              for s, mn in zip(ss, m_news)]
        pvs = [jnp.dot(p, vs[h], preferred_element_type=F32) for p, (h, r) in zip(ps, pairs)]
        for (h, r), mp, mn, pv in zip(pairs, m_prevs, m_news, pvs):
            acc_s[h, r, :] = jnp.exp(mp - mn) * acc_s[h, r, :] + pv
            m_s[h, r, :] = mn
        return carry

    lax.fori_loop(0, (i * tq) // SEL_CHUNK + 1, sel_body, 0)

    for h in range(NSA_KV_HEADS):
        acc = acc_s[h]
        o_slc = acc[:, 0:64] / acc[:, 64:128]
        o_cmp = ocmp_s[h]
        o_win = owin_s[h]
        for g in range(ng):
            hh = ng * h + g
            r = slice(g * tq, (g + 1) * tq)
            gate = lambda branch: gx[:, 128 * (NSA_HEADS * branch + hh):128 * (NSA_HEADS * branch + hh) + 64]
            y_ref[:, 64 * hh:64 * hh + 64] = gate(0) * o_cmp[r] + gate(1) * o_slc[r] + gate(2) * o_win[r]


def nsa_prompt(proj, n_seq, t_len, cparams):
    tq = KEY_CHUNK
    nt = t_len // tq
    ncp = t_len // CMP_STRIDE
    ns = t_len // SEL_BLOCK
    w1, pe, b1, w2 = cparams
    cstart = np.arange(ncp)[:, None] * CMP_STRIDE
    sstart = np.arange(ns)[None, :] * SEL_BLOCK
    ovl = jnp.asarray((cstart < sstart + SEL_BLOCK) & (cstart + CMP_BLOCK > sstart), BF16)
    expand = jnp.asarray(np.arange(t_len)[None, :] // SEL_BLOCK == np.arange(ns)[:, None], BF16)
    gexp = jnp.asarray(np.arange(128)[:, None] == SM_NG + np.arange(3 * NSA_HEADS * 128)[None, :] // 128, BF16)
    cmp_rows = proj[:, COL_KV:COL_KV + 256].reshape(n_seq, ncp, CMP_STRIDE, 256).transpose(0, 2, 1, 3)
    once = pl.Buffered(1)
    const = lambda shape: pl.BlockSpec(shape, lambda b, i: (0,) * len(shape), pipeline_mode=once)
    return pl.pallas_call(
        functools.partial(_nsa_prompt_kernel, t_len=t_len),
        grid=(n_seq, nt),
        in_specs=[pl.BlockSpec((tq, 512), lambda b, i: (b * nt + i, COL_Q // 512)),
                  pl.BlockSpec((1, CMP_STRIDE, ncp, 256), lambda b, i: (b, 0, 0, 0), pipeline_mode=once),
                  pl.BlockSpec((t_len, 256), lambda b, i: (b, (COL_KV + 256) // 256), pipeline_mode=once),
                  pl.BlockSpec((t_len, 256), lambda b, i: (b, COL_WIN // 256), pipeline_mode=once),
                  pl.BlockSpec((tq, 128), lambda b, i: (b * nt + i, COL_SMALL // 128)),
                  const((CMP_BLOCK, 256, 256)), const((CMP_BLOCK, 256)), const((1, 256)), const((256, 256)),
                  const((ncp, ns)), const((ns, t_len)), const((128, 3 * NSA_HEADS * 128))],
        out_specs=pl.BlockSpec((tq, NSA_WIDTH), lambda b, i: (b * nt + i, 0)),
        out_shape=jax.ShapeDtypeStruct((n_seq * t_len, NSA_WIDTH), F32),
        scratch_shapes=[pltpu.VMEM((ncp, 128), BF16), pltpu.VMEM((ncp, 128), BF16),
                        pltpu.VMEM((NSA_KV_HEADS, t_len, 64), BF16), pltpu.VMEM((NSA_KV_HEADS, t_len, 128), BF16),
                        pltpu.VMEM((NSA_KV_HEADS, t_len, 64), BF16), pltpu.VMEM((NSA_KV_HEADS, t_len, 128), BF16),
                        pltpu.VMEM((NSA_KV_HEADS, NSA_GROUP * tq, 64), BF16), pltpu.VMEM((NSA_KV_HEADS, tq, ns), BF16),
                        pltpu.VMEM((NSA_KV_HEADS, NSA_GROUP * tq, 64), F32),
                        pltpu.VMEM((NSA_KV_HEADS, NSA_GROUP * tq, 64), F32),
                        pltpu.VMEM((NSA_KV_HEADS, NSA_GROUP * tq, 128), F32),
                        pltpu.VMEM((NSA_KV_HEADS, NSA_GROUP * tq, 128), F32)],
        compiler_params=pltpu.CompilerParams(dimension_semantics=("arbitrary", "arbitrary"),
                                             vmem_limit_bytes=VMEM_LIMIT_BYTES),
        name="nsa_prompt",
    )(proj, cmp_rows, proj, proj, proj, w1, pe, b1, w2, ovl, expand, gexp)


PAGE_ROWS = 128
SEL_LANES = 128


def _softmax_segments(segs):
    m = segs[0].max(axis=-1, keepdims=True)
    for s in segs[1:]:
        m = jnp.maximum(m, s.max(axis=-1, keepdims=True))
    ps = [jnp.exp(s - m) for s in segs]
    den = ps[0].sum(axis=-1, keepdims=True)
    for p in ps[1:]:
        den = den + p.sum(axis=-1, keepdims=True)
    return ps, den


def _nsa_sample_kernel(pt_ref, *refs, n_pages, past_len, dec_t):
    del pt_ref
    pages = refs[:n_pages]
    (q_ref, kvn_ref, wn_ref, sm_ref, wc_ref, w1_ref, pe_ref, b1_ref, w2_ref, ovl_ref, exp_ref,
     y_ref, wst_ref, rk_s, rv_s, newpg_s, neww_s) = refs[n_pages:]
    ncp = past_len // CMP_STRIDE
    ng, nh = NSA_GROUP, NSA_KV_HEADS
    rows_h = ng * dec_t
    rows = nh * rows_h
    nt = (((1,), (1,)), ((), ()))

    for p in range(n_pages):
        rk_s[p * PAGE_ROWS:(p + 1) * PAGE_ROWS, :] = pages[p][0:128, :].T
        rv_s[p * PAGE_ROWS:(p + 1) * PAGE_ROWS, :] = pages[p][128:256, :].T
    kvc = _compress_rows(
        lambda l: jnp.concatenate([rk_s[pl.ds(l, ncp, stride=CMP_STRIDE), :],
                                   rv_s[pl.ds(l, ncp, stride=CMP_STRIDE), :]], axis=1),
        ncp, w1_ref, pe_ref, b1_ref, w2_ref)
    kc = kvc[:, 0:128].astype(BF16)
    vc = kvc[:, 128:256].astype(BF16)

    newpg_s[...] = jnp.zeros((PAGE_ROWS, 256), F32)
    newpg_s[0:dec_t, :] = kvn_ref[0][:, 256:512]
    neww_s[...] = jnp.zeros((PAGE_ROWS, 256), F32)
    neww_s[0:dec_t, :] = wn_ref[0]

    qf = q_ref[0] * (NSA_HD ** -0.5)
    zero = jnp.zeros((dec_t, 64), F32)
    qrows = []
    for h in range(nh):
        for g in range(ng):
            piece = qf[:, 64 * (ng * h + g):64 * (ng * h + g) + 64]
            qrows.append(jnp.concatenate([piece, zero] if h == 0 else [zero, piece], axis=1))
    q_bd = jnp.concatenate(qrows, axis=0).astype(BF16)
    t_row = lax.broadcasted_iota(jnp.int32, (rows, 1), 0) & (dec_t - 1)
    pos = past_len + t_row
    head0 = lax.broadcasted_iota(jnp.int32, (rows, 128), 0) < rows_h
    lane_lo = lax.broadcasted_iota(jnp.int32, (rows, 128), 1) < 64
    own = head0 == lane_lo

    def own_half(x):
        x = jnp.where(own, x, 0.0)
        return x[:, 0:64] + x[:, 64:128]

    s = lax.dot_general(q_bd, kc, nt, preferred_element_type=F32)
    cend = lax.broadcasted_iota(jnp.int32, (rows, ncp), 1) * CMP_STRIDE + (CMP_BLOCK - 1)
    s = jnp.where(cend <= pos, s, -jnp.inf)
    m = jnp.max(s, axis=-1, keepdims=True)
    m = jnp.where(m == -jnp.inf, 0.0, m)
    p = jnp.exp(s - m)
    pn = p / jnp.maximum(jnp.sum(p, axis=-1, keepdims=True), 1e-30)
    o_cmp = own_half(_mm(pn.astype(BF16), vc))

    ovl = ovl_ref[...]
    psums = []
    for h in range(nh):
        psum = pn[h * rows_h:h * rows_h + dec_t]
        for g in range(1, ng):
            psum = psum + pn[h * rows_h + g * dec_t:h * rows_h + (g + 1) * dec_t]
        psums.append(psum)
    psum = jnp.concatenate(psums, axis=0)
    p_hi = psum.astype(BF16)
    imp = _mm(p_hi, ovl) + _mm((psum - p_hi.astype(F32)).astype(BF16), ovl)
    pos_ht = past_len + (lax.broadcasted_iota(jnp.int32, (nh * dec_t, 1), 0) & (dec_t - 1))
    lane = lax.broadcasted_iota(jnp.int32, (rows, PAGE_ROWS), 1)
    wlen = wc_ref.shape[2]
    wc = wc_ref[0]
    branch = {}
    scores = [None] * (n_pages + 1)

    def window_branch():
        jw = lax.broadcasted_iota(jnp.int32, (rows, wlen), 1)
        s_old = _mm(q_bd, wc[0:128, :].astype(BF16))
        s_old = jnp.where(past_len - wlen + jw > pos - WINDOW, s_old, NEG)
        s_new = lax.dot_general(q_bd, neww_s[:, 0:128].astype(BF16), nt, preferred_element_type=F32)
        s_new = jnp.where(past_len + lane <= pos, s_new, NEG)
        pw, den_w = _softmax_segments([s_old, s_new])
        acc_w = (lax.dot_general(pw[0].astype(BF16), wc[128:256, :].astype(BF16), nt, preferred_element_type=F32)
                 + _mm(pw[1].astype(BF16), neww_s[:, 128:256].astype(BF16)))
        branch["win"] = own_half(acc_w) / den_w

    def page_scores(lo, hi):
        def run():
            for pg in range(lo, hi):
                if pg < n_pages:
                    scores[pg] = _mm(q_bd, pages[pg][256:384, :].astype(BF16))
                else:
                    scores[pg] = lax.dot_general(q_bd, newpg_s[:, 0:128].astype(BF16), nt, preferred_element_type=F32)
        return run

    per = -(-(n_pages + 1) // (SEL_TOPN - 1))
    fillers = [window_branch] + [page_scores(lo, min(lo + per, n_pages + 1)) for lo in range(0, n_pages + 1, per)]
    sel = _select_topn(imp, pos_ht, fillers)
    sel_b = jnp.concatenate([sel[h * dec_t:(h + 1) * dec_t] for h in range(nh) for _ in range(ng)],
                            axis=0).astype(BF16)

    segs = []
    for pg in range(n_pages + 1):
        chosen = _mm(sel_b, exp_ref[:, pg * PAGE_ROWS:(pg + 1) * PAGE_ROWS])
        ok = chosen > 0.5
        if pg == n_pages:
            ok = ok & (past_len + lane <= pos)
        segs.append(jnp.where(ok, scores[pg], NEG))
    ps, den = _softmax_segments(segs)
    acc = jnp.zeros((rows, 128), F32)
    for pg in range(n_pages + 1):
        if pg < n_pages:
            acc = acc + lax.dot_general(ps[pg].astype(BF16), pages[pg][384:512, :].astype(BF16), nt,
                                        preferred_element_type=F32)
        else:
            acc = acc + _mm(ps[pg].astype(BF16), newpg_s[:, 128:256].astype(BF16))
    o_slc = own_half(acc) / den
    o_win = branch["win"]

    gates = jax.nn.sigmoid(sm_ref[0][:, SM_NG:SM_NG + 3 * NSA_HEADS])
    for hh in range(NSA_HEADS):
        r = slice(hh * dec_t, (hh + 1) * dec_t)
        y_ref[0, :, 64 * hh:64 * hh + 64] = (gates[:, hh:hh + 1] * o_cmp[r]
                                             + gates[:, NSA_HEADS + hh:NSA_HEADS + hh + 1] * o_slc[r]
                                             + gates[:, 2 * NSA_HEADS + hh:2 * NSA_HEADS + hh + 1] * o_win[r])

    rolled = pltpu.roll(wc, wlen - dec_t, axis=1)
    new_t = jnp.concatenate([neww_s[:, 0:128].T, neww_s[:, 128:256].T], axis=0)
    lane_w = lax.broadcasted_iota(jnp.int32, (256, 128), 1)
    last = jnp.where(lane_w >= 128 - dec_t, pltpu.roll(new_t, 128 - dec_t, axis=1), rolled[:, wlen - 128:wlen])
    wst_ref[0, :, 0:wlen - 128] = rolled[:, 0:wlen - 128]
    wst_ref[0, :, wlen - 128:wlen] = last


def nsa_sample(q, kv_new, win_new, sm, cache_kv, layer_idx, page_table, win_cache, cparams):
    bsz, dec_t, _ = q.shape
    n_pages = page_table.shape[1]
    past_len = n_pages * PAGE_ROWS
    wlen = win_cache.shape[2]
    assert dec_t < CMP_STRIDE and dec_t % SUBLANES == 0 and dec_t & (dec_t - 1) == 0
    assert wlen == WINDOW and wlen <= past_len
    ncp = past_len // CMP_STRIDE
    ns = -(-(past_len + dec_t) // SEL_BLOCK)
    assert ns <= SEL_LANES
    w1, pe, b1, w2 = cparams
    cstart = np.arange(ncp)[:, None] * CMP_STRIDE
    sstart = np.arange(SEL_LANES)[None, :] * SEL_BLOCK
    ovl = jnp.asarray((cstart < sstart + SEL_BLOCK) & (cstart + CMP_BLOCK > sstart) & (np.arange(SEL_LANES)[None, :] < ns), BF16)
    keys = np.arange((n_pages + 1) * PAGE_ROWS)
    expand = jnp.asarray(keys[None, :] // SEL_BLOCK == np.arange(SEL_LANES)[:, None], BF16)
    const = lambda shape: pl.BlockSpec(shape, lambda b, pt: (0,) * len(shape))
    seq = lambda r, c: pl.BlockSpec((1, r, c), lambda b, pt: (b, 0, 0))
    page_specs = [pl.BlockSpec((None, None, 512, PAGE_ROWS), functools.partial(
        lambda b, pt, p: (layer_idx, pt[b, p], 0, 0), p=p)) for p in range(n_pages)]
    grid_spec = pltpu.PrefetchScalarGridSpec(
        num_scalar_prefetch=1,
        grid=(bsz,),
        in_specs=page_specs + [seq(dec_t, 512), seq(dec_t, 512), seq(dec_t, 256), seq(dec_t, 128), seq(256, wlen),
                               const((CMP_BLOCK, 256, 256)), const((CMP_BLOCK, 256)), const((1, 256)),
                               const((256, 256)), const((ncp, SEL_LANES)),
                               const((SEL_LANES, (n_pages + 1) * PAGE_ROWS))],
        out_specs=[seq(dec_t, NSA_WIDTH), seq(256, wlen)],
        scratch_shapes=[pltpu.VMEM((past_len, 128), F32), pltpu.VMEM((past_len, 128), F32),
                        pltpu.VMEM((PAGE_ROWS, 256), F32), pltpu.VMEM((PAGE_ROWS, 256), F32)])
    return pl.pallas_call(
        functools.partial(_nsa_sample_kernel, n_pages=n_pages, past_len=past_len, dec_t=dec_t),
        grid_spec=grid_spec,
        out_shape=[jax.ShapeDtypeStruct((bsz, dec_t, NSA_WIDTH), F32), jax.ShapeDtypeStruct((bsz, 256, wlen), F32)],
        compiler_params=pltpu.CompilerParams(dimension_semantics=("arbitrary",), vmem_limit_bytes=VMEM_LIMIT_BYTES),
        name="nsa_sample",
    )(page_table, *([cache_kv] * n_pages), q, kv_new, win_new, sm, win_cache, w1, pe, b1, w2, ovl, expand)


TM = 512


def _to_tb(x, b, t):
    return x.reshape(b, t, -1).transpose(1, 0, 2).reshape(t * b, -1)


def _to_bt(x, b, t):
    return x.reshape(t, b, -1).transpose(1, 0, 2).reshape(b * t, -1)


def _blocked_buf(state):
    b, k1, c = state.shape
    return state.transpose(1, 0, 2).reshape(1, k1 * b, c)


def _unblocked(st, b):
    return st.reshape(-1, b, st.shape[-1]).transpose(1, 0, 2)


def layer_prompt(x, n_seq, t_len, lw):
    tiles = t_len // TM
    carry = ("carry", tiles)
    zeros = lambda c: jnp.zeros((n_seq, SUBLANES, c), F32)
    proj = dense(x, lw["w_in"], TM, N_PROJ, "in_proj")
    qkv, gconv = gdn_conv_silu(proj, lw["gdn_conv_w"], zeros(GDN_CONV_CH), TM, carry)
    y_b, s_new = gated_delta(qkv, proj, jnp.zeros((n_seq, GDN_HEADS, GDN_DK, GDN_DV), F32), lw["gdn_a_log"],
                             lw["gdn_dt_bias"], lw["gdn_norm_g"], GDN_CHUNK, True, t_len // GDN_STEP)
    y_c = nsa_prompt(proj, n_seq, t_len, lw["cmp"])
    x1, conva = mix_out_ln(proj, y_b, y_c, x, lw["w_out"], lw["conv_a_w"], lw["ln1_g"], lw["ln1_b"],
                           zeros(A_WIDTH), TM, carry)
    h, ffnc = ffn_up_act(x1, lw["w_up"], lw["ffn_conv_w"], zeros(D_FF), TM, D_FF, carry)
    x2 = dense_res_ln(h, lw["w_down"], x1, lw["ln2_g"], lw["ln2_b"], TM, "ffn_down_ln")
    p3 = proj.reshape(n_seq, t_len, N_PROJ)
    kv_new = p3[:, :, COL_KV:COL_KV + 4 * NSA_KV_DIM].reshape(n_seq, t_len, 4, NSA_KV_HEADS, NSA_HD)
    wkeep = min(WINDOW, t_len)
    win_state = p3[:, t_len - wkeep:, COL_WIN:COL_WIN + 2 * NSA_KV_DIM].reshape(n_seq, wkeep, 2, NSA_KV_HEADS, NSA_HD)
    tail = lambda st, k: st[:, SUBLANES - (k - 1):, :]
    return x2, (kv_new, win_state, tail(conva, A_CONV), tail(gconv, GDN_CONV), s_new, tail(ffnc, FFN_CONV))


def layer_sample(x, bsz, dec_t, lw, layer_idx, cache_kv, page_table, win_cache, st_conv_a, st_gdn_conv, st_gdn,
                 st_ffn_conv):
    m = dec_t * bsz
    blocked = ("blocked", bsz)
    proj = dense(x, lw["w_in"], TM, N_PROJ, "in_proj")
    proj_bt = _to_bt(proj, bsz, dec_t)
    qkv, gconv = gdn_conv_silu(proj, lw["gdn_conv_w"], _blocked_buf(st_gdn_conv), m, blocked)
    y_b, s_new = gated_delta(_to_bt(qkv, bsz, dec_t), proj_bt, st_gdn, lw["gdn_a_log"], lw["gdn_dt_bias"],
                             lw["gdn_norm_g"], dec_t, False, 1)
    p3 = proj_bt.reshape(bsz, dec_t, N_PROJ)
    kv_new = p3[:, :, COL_KV:COL_KV + 4 * NSA_KV_DIM]
    y_c, win_state = nsa_sample(p3[:, :, COL_Q:COL_Q + NSA_WIDTH], kv_new,
                                p3[:, :, COL_WIN:COL_WIN + 2 * NSA_KV_DIM], p3[:, :, COL_SMALL:COL_SMALL + 128],
                                cache_kv, layer_idx, page_table, win_cache, lw["cmp"])
    x1, conva = mix_out_ln(proj, _to_tb(y_b, bsz, dec_t), _to_tb(y_c, bsz, dec_t), x, lw["w_out"], lw["conv_a_w"],
                           lw["ln1_g"], lw["ln1_b"], _blocked_buf(st_conv_a), m, blocked)
    h, ffnc = ffn_up_act(x1, lw["w_up"], lw["ffn_conv_w"], _blocked_buf(st_ffn_conv), m, D_FF // 2, blocked)
    x2 = dense_res_ln(h, lw["w_down"], x1, lw["ln2_g"], lw["ln2_b"], TM, "ffn_down_ln")
    wlen = win_cache.shape[2]
    return x2, (kv_new.reshape(bsz, dec_t, 4, NSA_KV_HEADS, NSA_HD),
                win_state.reshape(bsz, 2, NSA_KV_HEADS, NSA_HD, wlen).transpose(0, 4, 1, 2, 3),
                _unblocked(conva, bsz), _unblocked(gconv, bsz), s_new, _unblocked(ffnc, bsz))


def stack_layers(states, i):
    return jnp.stack([s[i] for s in states], axis=0)


def kernel(x_prompt, x_sample, cache_nsa_kv, cache_nsa_win, state_conv_a, state_gdn_conv, state_gdn, state_ffn_conv, page_table, ln_emb_g, ln_emb_b, w_in, conv_a_w, gdn_conv_w, gdn_a_log, gdn_dt_bias, gdn_norm_g, cmp_pe, cmp_w1, cmp_b1, cmp_w2, w_out, ln1_g, ln1_b, w_up, ffn_conv_w, w_down, ln2_g, ln2_b):
    n_seq, t_len = x_prompt.shape[:2]
    dec_b, dec_t = x_sample.shape[:2]
    depth = w_in.shape[0]
    xp = layer_norm_rows(x_prompt.reshape(-1, D_MODEL), ln_emb_g, ln_emb_b)
    xs = layer_norm_rows(_to_tb(x_sample.reshape(-1, D_MODEL), dec_b, dec_t), ln_emb_g, ln_emb_b)
    w_in_b = permute_w_in(w_in)
    w_out_b, w_up_b, w_down_b = (w.astype(BF16) for w in (w_out, w_up, w_down))
    cache_kv = cache_nsa_kv.transpose(0, 1, 3, 4, 5, 2).reshape(cache_nsa_kv.shape[:2] + (4 * NSA_KV_DIM, -1))
    win_cache = cache_nsa_win.transpose(0, 1, 3, 4, 5, 2).reshape(cache_nsa_win.shape[:2] + (2 * NSA_KV_DIM, -1))
    st_p, st_s = [], []
    for l in range(depth):
        lw = dict(w_in=w_in_b[l], conv_a_w=conv_a_w[l], gdn_conv_w=gdn_conv_w[l], gdn_a_log=gdn_a_log[l],
                  gdn_dt_bias=gdn_dt_bias[l], gdn_norm_g=gdn_norm_g[l],
                  cmp=compress_params(cmp_pe[l], cmp_w1[l], cmp_b1[l], cmp_w2[l]),
                  w_out=w_out_b[l], ln1_g=ln1_g[l], ln1_b=ln1_b[l], w_up=w_up_b[l], ffn_conv_w=ffn_conv_w[l],
                  w_down=w_down_b[l], ln2_g=ln2_g[l], ln2_b=ln2_b[l])
        xp, sp = layer_prompt(xp, n_seq, t_len, lw)
        xs, ss = layer_sample(xs, dec_b, dec_t, lw, l, cache_kv, page_table, win_cache[l], state_conv_a[l],
                              state_gdn_conv[l], state_gdn[l], state_ffn_conv[l])
        st_p.append(sp)
        st_s.append(ss)
    xp = xp.reshape(n_seq, t_len, D_MODEL)
    xs = _to_bt(xs, dec_b, dec_t).reshape(dec_b, dec_t, D_MODEL)
    return (xp, xs,
            stack_layers(st_p, 0), stack_layers(st_s, 0),
            stack_layers(st_p, 1), stack_layers(st_s, 1),
            stack_layers(st_p, 2), stack_layers(st_s, 2),
            stack_layers(st_p, 3), stack_layers(st_s, 3),
            stack_layers(st_p, 4), stack_layers(st_s, 4),
            stack_layers(st_p, 5), stack_layers(st_s, 5))
```
